```python
import math
import jax, jax.numpy as jnp
from jax import lax
import numpy as np

D_MODEL = 1024
BATCH = 32
SEQ = 2048
DEPTH = 4

GRID_W = 64
CTX_LEN = 256
EPS = 1e-6
D_FF = 4 * D_MODEL
D_FOURIER = D_MODEL // 2
FOURIER_GROUPS = 4
FOURIER_CH = D_FOURIER // FOURIER_GROUPS
D_SSM = D_MODEL - D_FOURIER
SSM_GROUP = 16
SSM_GROUPS = D_SSM // SSM_GROUP
SSM_STATE = 64
DT_MIN = 0.001
DT_MAX = 0.1
DA_HEAD_DIM = 64
DA_V_DIM = 2 * DA_HEAD_DIM
DA_HEADS = D_MODEL // DA_V_DIM
ROPE_THETA = 10000.0
Q_BLOCK = 128
N_EVEN = (DEPTH + 1) // 2
N_ODD = DEPTH // 2

kernel_name = "hybrid_fourier_s5_diffattn_prefix_dit"

F32 = jnp.float32


def rmsnorm(x, g):
    xf = x.astype(F32)
    y = xf * lax.rsqrt(jnp.mean(xf * xf, axis=-1, keepdims=True) + EPS)
    return (y * g.astype(F32)).astype(x.dtype)


def modulate(h, shift, scale):
    return h * (1.0 + scale) + shift


def adaln(cond, w, b):
    m = jax.nn.silu(cond) @ w + b
    return jnp.split(m[..., None, :], 6, axis=-1)


def sq_relu_mlp(h, w1, w2):
    return jnp.square(jax.nn.relu(h @ w1)) @ w2


def axial_rope_angles(rows):
    row = jnp.repeat(jnp.arange(rows, dtype=F32), GRID_W)
    col = jnp.tile(jnp.arange(GRID_W, dtype=F32), rows)
    half = DA_HEAD_DIM // 2
    inv_freq = ROPE_THETA ** (-jnp.arange(0, half, 2, dtype=F32) / half)
    ang_row = (row[:, None] * inv_freq)[:, None, None, :]
    ang_col = (col[:, None] * inv_freq)[:, None, None, :]
    return ang_row, ang_col


def rope_axis(x, ang):
    x1, x2 = jnp.split(x, 2, axis=-1)
    cos, sin = jnp.cos(ang), jnp.sin(ang)
    return jnp.concatenate([x1 * cos - x2 * sin, x1 * sin + x2 * cos], axis=-1).astype(x.dtype)


def apply_axial_rope(x, ang_row, ang_col):
    half = DA_HEAD_DIM // 2
    return jnp.concatenate([rope_axis(x[..., :half], ang_row), rope_axis(x[..., half:], ang_col)], axis=-1)


def fourier_mix(u):
    bsz, l, _ = u.shape
    ug = u.astype(F32).reshape(bsz, l, FOURIER_GROUPS, FOURIER_CH)
    y = jnp.fft.fft2(ug, axes=(1, 3), norm="ortho").real
    return y.reshape(bsz, l, D_FOURIER).astype(u.dtype)


def _linear_recurrence(e1, e2):
    a1, b1 = e1
    a2, b2 = e2
    return a1 * a2, a2 * b1 + b2


def s5_scan(u, a_re, a_im, log_dt, b_re, b_im, h0s):
    bsz, l, _ = u.shape
    ug = u.astype(F32).reshape(bsz, l, SSM_GROUPS, SSM_GROUP)
    states, finals = [], []
    for d in range(2):
        a = lax.complex(a_re[d].astype(F32), a_im[d].astype(F32))
        dt = jnp.exp(log_dt[d].astype(F32))[:, None]
        a_bar = jnp.exp(a * dt)
        b_c = lax.complex(b_re[d].astype(F32), b_im[d].astype(F32))
        b_bar = ((a_bar - 1.0) / a)[..., None] * b_c
        bu = jnp.einsum("blgh,gph->blgp", ug, b_bar)
        reverse = d == 1
        if h0s is not None:
            edge = -1 if reverse else 0
            bu = bu.at[:, edge].add(a_bar * h0s[d])
        h = lax.associative_scan(_linear_recurrence, (jnp.broadcast_to(a_bar, bu.shape), bu),
                                 axis=1, reverse=reverse)[1]
        states.append(h)
        finals.append(h[:, 0] if reverse else h[:, -1])
    return states, finals


def s5_readout(u, states, c_re, c_im, d_skip, w_glu, b_glu):
    bsz, l, _ = u.shape
    y = u.astype(F32) * d_skip.astype(F32)
    for d in range(2):
        cc = lax.complex(c_re[d].astype(F32), c_im[d].astype(F32))
        y = y + jnp.einsum("blgp,ghp->blgh", states[d], cc).real.reshape(bsz, l, D_SSM)
    y = jax.nn.gelu(y).astype(u.dtype)
    return y * jax.nn.sigmoid(y @ w_glu + b_glu)


def even_out(z, states, c_re, c_im, d_skip, w_glu, b_glu, w_out):
    ya = fourier_mix(z[..., :D_FOURIER])
    yb = s5_readout(z[..., D_FOURIER:], states, c_re, c_im, d_skip, w_glu, b_glu)
    return jnp.concatenate([ya, yb], axis=-1) @ w_out


def diff_qkv(h, w_in, gq, gk, ang):
    bsz, l, _ = h.shape
    q, k, v = jnp.split(h @ w_in, 3, axis=-1)
    q = rmsnorm(q.reshape(bsz, l, DA_HEADS, 2, DA_HEAD_DIM), gq)
    k = rmsnorm(k.reshape(bsz, l, DA_HEADS, 2, DA_HEAD_DIM), gk)
    v = v.reshape(bsz, l, DA_HEADS, DA_V_DIM)
    if ang is not None:
        q = apply_axial_rope(q, *ang)
        k = apply_axial_rope(k, *ang)
    return q, k, v


def diff_attend(q, k, v, lam):
    s = jnp.einsum("bqhmd,bkhmd->bhmqk", q, k).astype(F32) * (DA_HEAD_DIM ** -0.5)
    p = jax.nn.softmax(s, axis=-1)
    w = (p[:, :, 0] - lam * p[:, :, 1]).astype(v.dtype)
    return jnp.einsum("bhqk,bkhd->bqhd", w, v)


def blocked_diff_attention(q, k, v, lam):
    bsz, l = q.shape[:2]
    nb = l // Q_BLOCK
    qb = jnp.moveaxis(q.reshape(bsz, nb, Q_BLOCK, DA_HEADS, 2, DA_HEAD_DIM), 1, 0)
    ob = lax.map(lambda qi: diff_attend(qi, k, v, lam), qb)
    return jnp.moveaxis(ob, 0, 1).reshape(bsz, l, DA_HEADS, DA_V_DIM)


def diff_out(o, g_head, lam_init, w_out):
    bsz, l = o.shape[:2]
    o = rmsnorm(o, g_head) * (1.0 - lam_init)
    return o.reshape(bsz, l, DA_HEADS * DA_V_DIM) @ w_out


def setup_inputs(seed: int = 0) -> dict:
    key = jax.random.key(seed)
    keys = iter(jax.random.split(key, 40))

    def normal(shape, scale):
        return jax.random.normal(next(keys), shape, F32) * scale

    a_im_base = jnp.pi * jnp.arange(SSM_STATE, dtype=F32)
    return {
        "x": normal((BATCH, SEQ, D_MODEL), 1.0),
        "c": normal((BATCH, D_MODEL), 1.0),
        "ctx": normal((BATCH, CTX_LEN, D_MODEL), 1.0),
        "c_ctx": normal((D_MODEL,), 1.0),
        "norm1_g": 1.0 + normal((DEPTH, D_MODEL), 0.01),
        "norm2_g": 1.0 + normal((DEPTH, D_MODEL), 0.01),
        "ada_w": normal((DEPTH, D_MODEL, 6 * D_MODEL), 0.5 * D_MODEL ** -0.5),
        "ada_b": normal((DEPTH, 6 * D_MODEL), 0.01),
        "mlp_w1": normal((DEPTH, D_MODEL, D_FF), D_MODEL ** -0.5),
        "mlp_w2": normal((DEPTH, D_FF, D_MODEL), D_FF ** -0.5),
        "ev_w_in": normal((N_EVEN, D_MODEL, D_FOURIER + D_SSM), D_MODEL ** -0.5),
        "ev_w_out": normal((N_EVEN, D_FOURIER + D_SSM, D_MODEL), (D_FOURIER + D_SSM) ** -0.5),
        "ssm_a_re": -0.5 + normal((N_EVEN, 2, SSM_GROUPS, SSM_STATE), 0.01),
        "ssm_a_im": a_im_base + normal((N_EVEN, 2, SSM_GROUPS, SSM_STATE), 0.01),
        "ssm_log_dt": jax.random.uniform(next(keys), (N_EVEN, 2, SSM_GROUPS), F32,
                                         minval=math.log(DT_MIN), maxval=math.log(DT_MAX)),
        "ssm_b_re": normal((N_EVEN, 2, SSM_GROUPS, SSM_STATE, SSM_GROUP), (2 * SSM_GROUP) ** -0.5),
        "ssm_b_im": normal((N_EVEN, 2, SSM_GROUPS, SSM_STATE, SSM_GROUP), (2 * SSM_GROUP) ** -0.5),
        "ssm_c_re": normal((N_EVEN, 2, SSM_GROUPS, SSM_GROUP, SSM_STATE), (2 * SSM_STATE) ** -0.5),
        "ssm_c_im": normal((N_EVEN, 2, SSM_GROUPS, SSM_GROUP, SSM_STATE), (2 * SSM_STATE) ** -0.5),
        "ssm_d": normal((N_EVEN, D_SSM), 1.0),
        "ssm_w_glu": normal((N_EVEN, D_SSM, D_SSM), D_SSM ** -0.5),
        "ssm_b_glu": normal((N_EVEN, D_SSM), 0.01),
        "od_w_in": normal((N_ODD, D_MODEL, 3 * D_MODEL), D_MODEL ** -0.5),
        "od_w_out": normal((N_ODD, D_MODEL, D_MODEL), D_MODEL ** -0.5),
        "od_q_norm": 1.0 + normal((N_ODD, DA_HEAD_DIM), 0.01),
        "od_k_norm": 1.0 + normal((N_ODD, DA_HEAD_DIM), 0.01),
        "od_lambda": normal((N_ODD, 4, DA_HEAD_DIM), 0.1),
        "od_head_norm": 1.0 + normal((N_ODD, DA_V_DIM), 0.01),
    }


def reference(x, c, ctx, c_ctx, norm1_g, norm2_g, ada_w, ada_b, mlp_w1, mlp_w2,
              ev_w_in, ev_w_out, ssm_a_re, ssm_a_im, ssm_log_dt, ssm_b_re, ssm_b_im,
              ssm_c_re, ssm_c_im, ssm_d, ssm_w_glu, ssm_b_glu,
              od_w_in, od_w_out, od_q_norm, od_k_norm, od_lambda, od_head_norm):
    rows = x.shape[1] // GRID_W
    ang = axial_rope_angles(rows)
    for i in range(DEPTH):
        last = i == DEPTH - 1
        j = i // 2
        sh1, sc1, g1, sh2, sc2, g2 = adaln(c, ada_w[i], ada_b[i])
        csh1, csc1, cg1, csh2, csc2, cg2 = adaln(c_ctx, ada_w[i], ada_b[i])
        hx = modulate(rmsnorm(x, norm1_g[i]), sh1, sc1)
        hc = modulate(rmsnorm(ctx, norm1_g[i]), csh1, csc1)
        if i % 2 == 0:
            scan_p = (ssm_a_re[j], ssm_a_im[j], ssm_log_dt[j], ssm_b_re[j], ssm_b_im[j])
            read_p = (ssm_c_re[j], ssm_c_im[j], ssm_d[j], ssm_w_glu[j], ssm_b_glu[j], ev_w_out[j])
            zc = hc @ ev_w_in[j]
            zx = hx @ ev_w_in[j]
            st_c, fin_c = s5_scan(zc[..., D_FOURIER:], *scan_p, None)
            st_x, _ = s5_scan(zx[..., D_FOURIER:], *scan_p, fin_c)
            yx = even_out(zx, st_x, *read_p)
            yc = None if last else even_out(zc, st_c, *read_p)
        else:
            lam_init = 0.8 - 0.6 * math.exp(-0.3 * i)
            lp = od_lambda[j].astype(F32)
            lam = jnp.exp(jnp.sum(lp[0] * lp[1])) - jnp.exp(jnp.sum(lp[2] * lp[3])) + lam_init
            qx, kx, vx = diff_qkv(hx, od_w_in[j], od_q_norm[j], od_k_norm[j], ang)
            qc, kc, vc = diff_qkv(hc, od_w_in[j], od_q_norm[j], od_k_norm[j], None)
            k_all = jnp.concatenate([kx, kc], axis=1)
            v_all = jnp.concatenate([vx, vc], axis=1)
            ox = blocked_diff_attention(qx, k_all, v_all, lam)
            yx = diff_out(ox, od_head_norm[j], lam_init, od_w_out[j])
            yc = None if last else diff_out(diff_attend(qc, kc, vc, lam), od_head_norm[j], lam_init, od_w_out[j])
        x = x + g1 * yx
        x = x + g2 * sq_relu_mlp(modulate(rmsnorm(x, norm2_g[i]), sh2, sc2), mlp_w1[i], mlp_w2[i])
        if not last:
            ctx = ctx + cg1 * yc
            ctx = ctx + cg2 * sq_relu_mlp(modulate(rmsnorm(ctx, norm2_g[i]), csh2, csc2), mlp_w1[i], mlp_w2[i])
    return x
```

```python
import functools
import math

import jax
import jax.numpy as jnp
from jax import lax
from jax.experimental import pallas as pl
from jax.experimental.pallas import tpu as pltpu

F32 = jnp.float32
BF16 = jnp.bfloat16

D_MODEL = 1024
DEPTH = 4
GRID_W = 64
EPS = 1e-6
D_FF = 4 * D_MODEL
D_FOURIER = D_MODEL // 2
FOURIER_GROUPS = 4
FOURIER_CH = D_FOURIER // FOURIER_GROUPS
D_SSM = D_MODEL - D_FOURIER
SSM_GROUP = 16
SSM_GROUPS = D_SSM // SSM_GROUP
SSM_STATE = 64
DA_HEAD_DIM = 64
DA_V_DIM = 2 * DA_HEAD_DIM
DA_HEADS = D_MODEL // DA_V_DIM
ROPE_THETA = 10000.0

LANES = 128
SSM_CHUNK = 16
SLABS = D_SSM // LANES
SLAB_GROUPS = LANES // SSM_GROUP
SLAB_STATE = SLAB_GROUPS * SSM_STATE
VMEM_LIMIT = 56 * 1024 * 1024


def _params(n_axes):
    return pltpu.CompilerParams(dimension_semantics=("arbitrary",) * n_axes,
                                vmem_limit_bytes=VMEM_LIMIT)


def _tile(n, target, mult=8):
    t = min(n, target)
    while t > 1 and (n % t or t % mult):
        t -= 1
    return t if n % t == 0 else n


def _const_spec(shape):
    nd = len(shape)
    return pl.BlockSpec(shape, lambda *_: (0,) * nd, pipeline_mode=pl.Buffered(1))


def _sigmoid(x):
    return 1.0 / (1.0 + jnp.exp(-x))


def _gelu_tanh(x):
    c = math.sqrt(2.0 / math.pi)
    return x * (0.5 * (1.0 + jnp.tanh(c * (x + 0.044715 * (x * x * x)))))


def _norm_mod(x, g, shift, scale):
    ms = jnp.mean(x * x, axis=-1, keepdims=True)
    y = x * lax.rsqrt(ms + EPS) * g
    return y * (1.0 + scale) + shift


def _adaln_kernel(c_ref, w_ref, b_ref, o_ref):
    c = c_ref[...]
    s = (c * _sigmoid(c)).astype(BF16)
    o_ref[0] = jnp.dot(s, w_ref[0].astype(BF16), preferred_element_type=F32) + b_ref[0]


def _adaln(cond, ada_w, ada_b):
    depth, d, n = ada_w.shape
    r = cond.shape[0]
    tn = _tile(n, 1536, LANES)
    return pl.pallas_call(
        _adaln_kernel,
        out_shape=jax.ShapeDtypeStruct((depth, r, n), F32),
        grid=(depth, n // tn),
        in_specs=[pl.BlockSpec((r, d), lambda l, j: (0, 0)),
                  pl.BlockSpec((1, d, tn), lambda l, j: (l, 0, j)),
                  pl.BlockSpec((1, 1, tn), lambda l, j: (l, 0, j))],
        out_specs=pl.BlockSpec((1, r, tn), lambda l, j: (l, 0, j)),
        compiler_params=_params(2), name="adaln",
    )(cond, ada_w, ada_b.reshape(depth, 1, n))


class _Rows:
    def __init__(self, nb, rpb, tm, mod_row):
        self.nb, self.rpb, self.tm = nb, rpb, tm
        self.nt = rpb // tm
        self.grid = (nb, self.nt)
        self.mod_row = mod_row

    def rows(self, cols):
        nt = self.nt
        return pl.BlockSpec((self.tm, cols), lambda b, i: (b * nt + i, 0))

    def mod(self, k):
        if self.mod_row is None:
            return pl.BlockSpec((None, 1, D_MODEL), lambda b, i: (b, 0, k))
        r = self.mod_row
        return pl.BlockSpec((None, 1, D_MODEL), lambda b, i: (r, 0, k))


def _even_in_kernel(x_ref, g_ref, sh_ref, sc_ref, w_ref, cs_ref, vc_ref, vs_ref, zs_ref):
    h = _norm_mod(x_ref[...], g_ref[...], sh_ref[...], sc_ref[...]).astype(BF16)
    z = jnp.dot(h, w_ref[...], preferred_element_type=F32)
    zf = z[:, :D_FOURIER].astype(BF16)
    for gi in range(FOURIER_GROUPS):
        lo, hi = gi * FOURIER_CH, (gi + 1) * FOURIER_CH
        v = jnp.dot(zf[:, lo:hi], cs_ref[...], preferred_element_type=F32)
        vc_ref[:, lo:hi] = v[:, :FOURIER_CH].astype(BF16)
        vs_ref[:, lo:hi] = v[:, FOURIER_CH:].astype(BF16)
    zs_ref[...] = z[:, D_FOURIER:].astype(BF16)


def _even_in(x, rows, norm_g, mod, w_in, cs):
    n = x.shape[0]
    outs = tuple(jax.ShapeDtypeStruct((n, D_FOURIER), BF16) for _ in range(3))
    return pl.pallas_call(
        _even_in_kernel, out_shape=outs, grid=rows.grid,
        in_specs=[rows.rows(D_MODEL), _const_spec((1, D_MODEL)), rows.mod(0), rows.mod(1),
                  _const_spec((D_MODEL, D_MODEL)), _const_spec((FOURIER_CH, 2 * FOURIER_CH))],
        out_specs=tuple(rows.rows(D_FOURIER) for _ in range(3)),
        compiler_params=_params(2), name="even_in",
    )(x, norm_g, mod, mod, w_in, cs)


def _fourier_kernel(cl_ref, sl_ref, vc_ref, vs_ref, o_ref, *, scale):
    a = jnp.dot(cl_ref[...], vc_ref[0], preferred_element_type=F32)
    a = a - jnp.dot(sl_ref[...], vs_ref[0], preferred_element_type=F32)
    o_ref[0] = (a * scale).astype(BF16)


def _fourier(vc, vs, cl, sl, nb, length):
    vc3 = vc.reshape(nb, length, D_FOURIER)
    vs3 = vs.reshape(nb, length, D_FOURIER)
    scale = 1.0 / math.sqrt(length * FOURIER_CH)
    blk = pl.BlockSpec((1, length, D_FOURIER), lambda b: (b, 0, 0))
    out = pl.pallas_call(
        functools.partial(_fourier_kernel, scale=scale),
        out_shape=jax.ShapeDtypeStruct((nb, length, D_FOURIER), BF16),
        grid=(nb,),
        in_specs=[_const_spec((length, length)), _const_spec((length, length)), blk, blk],
        out_specs=blk,
        compiler_params=_params(1), name="fourier",
    )(cl, sl, vc3, vs3)
    return out.reshape(nb * length, D_FOURIER)


def _dft_mats(n):
    k = lax.iota(jnp.int32, n)
    ang = ((k[:, None] * k[None, :]) % n).astype(F32) * (2.0 * math.pi / n)
    return jnp.cos(ang), jnp.sin(ang)


def _ssm_mats(a_re, a_im, log_dt, b_re, b_im, c_re, c_im, d_skip):
    t = SSM_CHUNK
    a = lax.complex(a_re.astype(F32), a_im.astype(F32))
    adt = a * jnp.exp(log_dt.astype(F32))[..., None]
    a_bar = jnp.exp(adt)
    bb = ((a_bar - 1.0) / a)[..., None] * lax.complex(b_re.astype(F32), b_im.astype(F32))
    cc = lax.complex(c_re.astype(F32), c_im.astype(F32))
    apow = jnp.exp(adt[..., None] * jnp.arange(t + 1, dtype=F32))
    taps = jnp.einsum("dghp,dgpl,dgpk->dglhk", cc, apow[..., :t], bb).real
    s_idx = jnp.arange(t)[:, None]
    t_idx = jnp.arange(t)[None, :]
    fwd = jnp.where((s_idx <= t_idx)[None, :, :, None, None],
                    taps[0][:, jnp.clip(t_idx - s_idx, 0, t - 1)], 0.0)
    bwd = jnp.where((s_idx >= t_idx)[None, :, :, None, None],
                    taps[1][:, jnp.clip(s_idx - t_idx, 0, t - 1)], 0.0)
    skip = (jnp.eye(t, dtype=F32)[None, :, :, None, None]
            * (jnp.eye(SSM_GROUP, dtype=F32) * d_skip.astype(F32).reshape(SSM_GROUPS, SSM_GROUP, 1)
               )[:, None, None, :, :])
    mg = (fwd + bwd + skip).transpose(0, 1, 4, 2, 3)
    eye = jnp.eye(SLAB_GROUPS, dtype=F32)

    def slab_rows(x):
        x = x.reshape((SLABS, SLAB_GROUPS) + x.shape[1:])
        return x

    mg = slab_rows(mg)
    m = jnp.einsum("zashtk,ab->zsahtbk", mg, eye)
    m = m.reshape(SLABS, t * LANES, t * LANES)
    pf = apow[0][..., :t][..., ::-1]
    pb = apow[1][..., :t]
    cf = jnp.einsum("gps,gph->gshp", pf, bb[0])
    cb = jnp.einsum("gps,gph->gshp", pb, bb[1])
    parts = jnp.stack([cf.real, cf.imag, cb.real, cb.imag], axis=3)
    parts = slab_rows(parts)
    bm = jnp.einsum("zashqp,ab->zsahqbp", parts, eye)
    bm = bm.reshape(SLABS, t * LANES, 4 * SLAB_STATE)
    rf = jnp.einsum("ghp,gpt->gpth", cc[0], apow[0][..., 1:])
    rb = jnp.einsum("ghp,gpt->gpth", cc[1], apow[1][..., 1:][..., ::-1])
    rd = jnp.stack([rf.real, -rf.imag, rb.real, -rb.imag], axis=1)
    rd = slab_rows(rd)
    wh = jnp.einsum("zaqpth,ab->zqaptbh", rd, eye)
    wh = wh.reshape(SLABS, 4 * SLAB_STATE, t * LANES)
    at = apow[..., t].reshape(2, SLABS, SLAB_STATE)
    at = at.transpose(1, 0, 2).reshape(SLABS * 2, 1, SLAB_STATE)
    return m.astype(BF16), bm.astype(BF16), wh.astype(BF16), at.real, at.imag


def _ssm_contrib_kernel(u_ref, bm_ref, o_ref):
    o_ref[...] = jnp.dot(u_ref[...], bm_ref[...], preferred_element_type=F32)


def _ssm_contrib(u, bm):
    slabs, r, k = u.shape
    n = bm.shape[2]
    tr = _tile(r, 576)
    return pl.pallas_call(
        _ssm_contrib_kernel,
        out_shape=jax.ShapeDtypeStruct((slabs, r, n), F32),
        grid=(slabs, r // tr),
        in_specs=[pl.BlockSpec((None, tr, k), lambda s, i: (s, i, 0)),
                  pl.BlockSpec((None, k, n), lambda s, i: (s, 0, 0))],
        out_specs=pl.BlockSpec((None, tr, n), lambda s, i: (s, i, 0)),
        compiler_params=_params(2), name="ssm_contrib",
    )(u, bm)


def _ssm_scan_kernel(ar_ref, ai_ref, cr_ref, ci_ref, hr_ref, hi_ref, *, ncc, nct, rb):
    d = pl.program_id(1)
    lt = ar_ref.shape[-1]
    ar = jnp.broadcast_to(ar_ref[...], (rb, lt))
    ai = jnp.broadcast_to(ai_ref[...], (rb, lt))
    zero = jnp.zeros((rb, lt), F32)

    def advance(i_src, i_dst, hr, hi):
        r0 = pl.multiple_of(i_src * rb, rb)
        r1 = pl.multiple_of(i_dst * rb, rb)
        nr = ar * hr - ai * hi + cr_ref[pl.ds(r0, rb), :]
        ni = ar * hi + ai * hr + ci_ref[pl.ds(r0, rb), :]
        hr_ref[pl.ds(r1, rb), :] = nr
        hi_ref[pl.ds(r1, rb), :] = ni
        return nr, ni

    @pl.when(d == 0)
    def _():
        hr_ref[pl.ds(0, rb), :] = zero
        hi_ref[pl.ds(0, rb), :] = zero
        lax.fori_loop(0, nct - 1, lambda i, c: advance(i, i + 1, *c), (zero, zero))

    @pl.when(d == 1)
    def _():
        last_c = (ncc - 1) * rb
        hr_ref[pl.ds(last_c, rb), :] = zero
        hi_ref[pl.ds(last_c, rb), :] = zero
        c = lax.fori_loop(0, ncc - 1, lambda k, c: advance(ncc - 1 - k, ncc - 2 - k, *c), (zero, zero))
        c = advance(0, nct - 1, *c)
        lax.fori_loop(0, nct - 1 - ncc, lambda k, c: advance(nct - 1 - k, nct - 2 - k, *c), c)


def _ssm_scan(contrib, at_re, at_im, ncc, nct, rb):
    slabs, r, n = contrib.shape
    lt = LANES
    q = SLAB_STATE // lt
    re_spec = pl.BlockSpec((None, r, lt), lambda s, d, j: (s, 0, d * 2 * q + j))
    im_spec = pl.BlockSpec((None, r, lt), lambda s, d, j: (s, 0, d * 2 * q + q + j))
    a_spec = pl.BlockSpec((None, 1, lt), lambda s, d, j: (s * 2 + d, 0, j))
    h_spec = pl.BlockSpec((None, r, lt), lambda s, d, j: (s, 0, d * q + j))
    out = jax.ShapeDtypeStruct((slabs, r, n // 2), F32)
    return pl.pallas_call(
        functools.partial(_ssm_scan_kernel, ncc=ncc, nct=nct, rb=rb),
        out_shape=(out, out),
        grid=(slabs, 2, q),
        in_specs=[a_spec, a_spec, re_spec, im_spec],
        out_specs=(h_spec, h_spec),
        compiler_params=_params(3), name="ssm_scan",
    )(at_re, at_im, contrib, contrib)


def _ssm_read_kernel(u_ref, hr_ref, hi_ref, m_ref, wh_ref, o_ref):
    q = SLAB_STATE
    acc = jnp.dot(u_ref[...], m_ref[...], preferred_element_type=F32)
    for d in range(2):
        acc += jnp.dot(hr_ref[:, d * q:(d + 1) * q].astype(BF16), wh_ref[2 * d * q:(2 * d + 1) * q, :],
                       preferred_element_type=F32)
        acc += jnp.dot(hi_ref[:, d * q:(d + 1) * q].astype(BF16), wh_ref[(2 * d + 1) * q:(2 * d + 2) * q, :],
                       preferred_element_type=F32)
    o_ref[...] = acc.astype(BF16)


def _ssm_read(u, h_re, h_im, m, wh):
    slabs, r, k = u.shape
    ns = h_re.shape[2]
    nw = wh.shape[1]
    tr = _tile(r, 576)
    tn = k // 2
    return pl.pallas_call(
        _ssm_read_kernel,
        out_shape=jax.ShapeDtypeStruct((slabs, r, k), BF16),
        grid=(slabs, 2, r // tr),
        in_specs=[pl.BlockSpec((None, tr, k), lambda s, j, i: (s, i, 0)),
                  pl.BlockSpec((None, tr, ns), lambda s, j, i: (s, i, 0)),
                  pl.BlockSpec((None, tr, ns), lambda s, j, i: (s, i, 0)),
                  pl.BlockSpec((None, k, tn), lambda s, j, i: (s, 0, j)),
                  pl.BlockSpec((None, nw, tn), lambda s, j, i: (s, 0, j))],
        out_specs=pl.BlockSpec((None, tr, tn), lambda s, j, i: (s, i, j)),
        compiler_params=_params(3), name="ssm_read",
    )(u, h_re, h_im, m, wh)


def _to_slabs(zs, nb, length):
    nc = length // SSM_CHUNK
    u = zs.reshape(nb, nc, SSM_CHUNK, SLABS, LANES).transpose(3, 1, 0, 2, 4)
    return u.reshape(SLABS, nc * nb, SSM_CHUNK * LANES)


def _from_slabs(y, nb, length):
    nc = length // SSM_CHUNK
    y = y.reshape(SLABS, nc, nb, SSM_CHUNK, LANES).transpose(2, 1, 3, 0, 4)
    return y.reshape(nb * length, D_SSM)


def _even_out_kernel(ya_ref, ys_ref, x_ref, gate_ref, wg_ref, bg_ref, wa_ref, wb_ref, o_ref):
    y = _gelu_tanh(ys_ref[...].astype(F32))
    glu = y * _sigmoid(jnp.dot(y.astype(BF16), wg_ref[...], preferred_element_type=F32) + bg_ref[...])
    acc = jnp.dot(ya_ref[...], wa_ref[...], preferred_element_type=F32)
    acc += jnp.dot(glu.astype(BF16), wb_ref[...], preferred_element_type=F32)
    o_ref[...] = x_ref[...] + gate_ref[...] * acc


def _even_out(ya, ys, x, rows, mod, w_glu, b_glu, w_out):
    return pl.pallas_call(
        _even_out_kernel, out_shape=jax.ShapeDtypeStruct(x.shape, F32), grid=rows.grid,
        in_specs=[rows.rows(D_FOURIER), rows.rows(D_SSM), rows.rows(D_MODEL), rows.mod(2),
                  _const_spec((D_SSM, D_SSM)), _const_spec((1, D_SSM)),
                  _const_spec((D_FOURIER, D_MODEL)), _const_spec((D_SSM, D_MODEL))],
        out_specs=rows.rows(D_MODEL),
        compiler_params=_params(2), name="even_out",
    )(ya, ys, x, mod, w_glu, b_glu, w_out[:D_FOURIER], w_out[D_FOURIER:])


def _mlp_kernel(x_ref, g_ref, sh_ref, sc_ref, gate_ref, w1_ref, w2_ref, o_ref, *, fc):
    x = x_ref[...]
    h = _norm_mod(x, g_ref[...], sh_ref[...], sc_ref[...]).astype(BF16)
    acc = jnp.zeros(x.shape, F32)
    for k in range(D_FF // fc):
        a = jnp.dot(h, w1_ref[:, k * fc:(k + 1) * fc], preferred_element_type=F32)
        a = jnp.square(jnp.maximum(a, 0.0)).astype(BF16)
        acc += jnp.dot(a, w2_ref[k * fc:(k + 1) * fc, :], preferred_element_type=F32)
    o_ref[...] = x + gate_ref[...] * acc


def _mlp(x, rows, norm_g, mod, w1, w2):
    return pl.pallas_call(
        functools.partial(_mlp_kernel, fc=1024),
        out_shape=jax.ShapeDtypeStruct(x.shape, F32), grid=rows.grid,
        in_specs=[rows.rows(D_MODEL), _const_spec((1, D_MODEL)), rows.mod(3), rows.mod(4), rows.mod(5),
                  _const_spec((D_MODEL, D_FF)), _const_spec((D_FF, D_MODEL))],
        out_specs=rows.rows(D_MODEL),
        compiler_params=_params(2), name="mlp",
    )(x, norm_g, mod, mod, mod, w1, w2)


def _qkv_kernel(*refs, rope):
    if rope:
        (x_ref, g_ref, sh_ref, sc_ref, w_ref, gqk_ref, ones_ref,
         cos_ref, sa_ref, sb_ref, o_ref) = refs
    else:
        x_ref, g_ref, sh_ref, sc_ref, w_ref, gqk_ref, ones_ref, o_ref = refs
    h = _norm_mod(x_ref[...], g_ref[...], sh_ref[...], sc_ref[...]).astype(BF16)
    y = jnp.dot(h, w_ref[...], preferred_element_type=F32)
    n_qk = 2 * DA_HEADS
    for s in range(n_qk):
        blk = y[:, s * LANES:(s + 1) * LANES]
        sq = blk * blk
        sq_hi = sq.astype(BF16)
        sq_lo = (sq - sq_hi.astype(F32)).astype(BF16)
        ssq = (jnp.dot(sq_hi, ones_ref[...], preferred_element_type=F32)
               + jnp.dot(sq_lo, ones_ref[...], preferred_element_type=F32))
        nrm = blk * lax.rsqrt(ssq * (1.0 / DA_HEAD_DIM) + EPS) * gqk_ref[s // DA_HEADS]
        if rope:
            nrm = (nrm * cos_ref[...] + pltpu.roll(nrm, LANES - 16, 1) * sa_ref[...]
                   + pltpu.roll(nrm, 16, 1) * sb_ref[...])
        if s < DA_HEADS:
            nrm = nrm * (DA_HEAD_DIM ** -0.5)
        o_ref[:, s * LANES:(s + 1) * LANES] = nrm.astype(BF16)
    o_ref[:, n_qk * LANES:] = y[:, n_qk * LANES:].astype(BF16)


def _qkv(x, rows, norm_g, mod, w_in, gqk, ones_bd, rope_tabs):
    n = x.shape[0]
    rope = rope_tabs is not None
    in_specs = [rows.rows(D_MODEL), _const_spec((1, D_MODEL)), rows.mod(0), rows.mod(1),
                _const_spec((D_MODEL, 3 * D_MODEL)), _const_spec((2, 1, LANES)),
                _const_spec((LANES, LANES))]
    args = [x, norm_g, mod, mod, w_in, gqk, ones_bd]
    if rope:
        tab = pl.BlockSpec((rows.tm, LANES), lambda b, i: (i, 0))
        in_specs += [tab, tab, tab]
        args += list(rope_tabs)
    return pl.pallas_call(
        functools.partial(_qkv_kernel, rope=rope),
        out_shape=jax.ShapeDtypeStruct((n, 3 * D_MODEL), BF16), grid=rows.grid,
        in_specs=in_specs, out_specs=rows.rows(3 * D_MODEL),
        compiler_params=_params(2), name="qkv",
    )(*args)


def _rope_tables(length):
    pos = jnp.arange(length, dtype=jnp.int32)
    row = (pos // GRID_W).astype(F32)
    col = (pos % GRID_W).astype(F32)
    half = DA_HEAD_DIM // 2
    inv_freq = ROPE_THETA ** (-jnp.arange(0, half, 2, dtype=F32) / half)
    lane = jnp.arange(LANES)
    f = lane % 16
    is_col = (lane % DA_HEAD_DIM) >= half
    second = (lane % half) >= 16
    ang = jnp.where(is_col[None, :], col[:, None], row[:, None]) * inv_freq[f][None, :]
    cos, sin = jnp.cos(ang), jnp.sin(ang)
    sin_a = jnp.where(second[None, :], 0.0, -sin)
    sin_b = jnp.where(second[None, :], sin, 0.0)
    return cos, sin_a, sin_b


def _attn_kernel(*refs, n_kv, lam_init):
    lp_ref, gh_ref, q_ref = refs[:3]
    kv_refs = refs[3:3 + 2 * n_kv]
    o_ref = refs[3 + 2 * n_kv]
    lp = lp_ref[...]
    lam = (jnp.exp(jnp.sum(lp[0:1] * lp[1:2], axis=-1, keepdims=True))
           - jnp.exp(jnp.sum(lp[2:3] * lp[3:4], axis=-1, keepdims=True)) + lam_init)
    q = q_ref[...]
    tq = q.shape[0]
    lane = lax.broadcasted_iota(jnp.int32, q.shape, 1)
    zero = jnp.zeros_like(q)
    qs = jnp.concatenate([jnp.where(lane < DA_HEAD_DIM, q, zero),
                          jnp.where(lane >= DA_HEAD_DIM, q, zero)], axis=0)
    dn = (((1,), (1,)), ((), ()))
    scores = [lax.dot_general(qs, kv_refs[2 * i][...], dn, preferred_element_type=F32)
              for i in range(n_kv)]
    m = scores[0].max(axis=-1, keepdims=True)
    for s in scores[1:]:
        m = jnp.maximum(m, s.max(axis=-1, keepdims=True))
    l = jnp.zeros_like(m)
    o2 = jnp.zeros((2 * tq, DA_V_DIM), F32)
    for i, s in enumerate(scores):
        p = jnp.exp(s - m)
        l += p.sum(axis=-1, keepdims=True)
        o2 += jnp.dot(p.astype(BF16), kv_refs[2 * i + 1][...], preferred_element_type=F32)
    o2 = o2 / l
    o = o2[:tq] - lam * o2[tq:]
    o = o * lax.rsqrt(jnp.mean(o * o, axis=-1, keepdims=True) + EPS) * gh_ref[...]
    o_ref[...] = (o * (1.0 - lam_init)).astype(BF16)


def _attention(qkv_q, kv_list, lam_p, g_head, lam_init, nb, lq, tq):
    q3 = qkv_q.reshape(nb, lq, 3 * D_MODEL)
    in_specs = [_const_spec((4, DA_HEAD_DIM)), _const_spec((1, DA_V_DIM)),
                pl.BlockSpec((None, tq, LANES), lambda b, h, i: (b, i, h))]
    args = [lam_p, g_head, q3]
    for arr, length in kv_list:
        a3 = arr.reshape(nb, length, 3 * D_MODEL)
        in_specs.append(pl.BlockSpec((None, length, LANES), lambda b, h, i: (b, 0, DA_HEADS + h)))
        in_specs.append(pl.BlockSpec((None, length, LANES), lambda b, h, i: (b, 0, 2 * DA_HEADS + h)))
        args += [a3, a3]
    out = pl.pallas_call(
        functools.partial(_attn_kernel, n_kv=len(kv_list), lam_init=lam_init),
        out_shape=jax.ShapeDtypeStruct((nb, lq, D_MODEL), BF16),
        grid=(nb, DA_HEADS, lq // tq),
        in_specs=in_specs,
        out_specs=pl.BlockSpec((None, tq, LANES), lambda b, h, i: (b, i, h)),
        compiler_params=_params(3), name="diff_attn",
    )(*args)
    return out.reshape(nb * lq, D_MODEL)


def _out_res_kernel(a_ref, x_ref, gate_ref, w_ref, o_ref):
    acc = jnp.dot(a_ref[...], w_ref[...], preferred_element_type=F32)
    o_ref[...] = x_ref[...] + gate_ref[...] * acc


def _out_res(a, x, rows, mod, w):
    return pl.pallas_call(
        _out_res_kernel, out_shape=jax.ShapeDtypeStruct(x.shape, F32), grid=rows.grid,
        in_specs=[rows.rows(D_MODEL), rows.rows(D_MODEL), rows.mod(2), _const_spec((D_MODEL, D_MODEL))],
        out_specs=rows.rows(D_MODEL),
        compiler_params=_params(2), name="out_res",
    )(a, x, mod, w)


def kernel(x, c, ctx, c_ctx, norm1_g, norm2_g, ada_w, ada_b, mlp_w1, mlp_w2, ev_w_in, ev_w_out, ssm_a_re, ssm_a_im, ssm_log_dt, ssm_b_re, ssm_b_im, ssm_c_re, ssm_c_im, ssm_d, ssm_w_glu, ssm_b_glu, od_w_in, od_w_out, od_q_norm, od_k_norm, od_lambda, od_head_norm):
    nb, length, d = x.shape
    clen = ctx.shape[1]
    assert d == D_MODEL and length % GRID_W == 0
    assert length % SSM_CHUNK == 0 and clen % SSM_CHUNK == 0

    xs = x.reshape(nb * length, d)
    cs = ctx.reshape(nb * clen, d)
    rows_x = _Rows(nb, length, _tile(length, 512), None)
    rows_c = _Rows(1, nb * clen, _tile(nb * clen, 512), nb)

    n_cond = nb + 1
    pad = (-n_cond) % 8
    cond = jnp.concatenate([c, c_ctx[None, :], jnp.zeros((pad, d), F32)], axis=0)
    mods = _adaln(cond, ada_w, ada_b)[:, :n_cond].reshape(DEPTH, n_cond, 1, 6 * d)

    kf = lax.iota(jnp.int32, FOURIER_CH)
    ang = ((kf[:, None] * kf[None, :]) % FOURIER_CH).astype(F32) * (2.0 * math.pi / FOURIER_CH)
    cs_ch = jnp.concatenate([jnp.cos(ang), jnp.sin(ang)], axis=1).astype(BF16)
    cl_x, sl_x = (m.astype(BF16) for m in _dft_mats(length))
    cl_c, sl_c = (m.astype(BF16) for m in _dft_mats(clen))
    rope_tabs = _rope_tables(length)
    lane = jnp.arange(LANES)
    ones_bd = (lane[:, None] // DA_HEAD_DIM == lane[None, :] // DA_HEAD_DIM).astype(BF16)

    ncc = clen // SSM_CHUNK
    nct = ncc + length // SSM_CHUNK

    for i in range(DEPTH):
        last = i == DEPTH - 1
        j = i // 2
        mod = mods[i]
        n1 = norm1_g[i].reshape(1, d)
        n2 = norm2_g[i].reshape(1, d)
        if i % 2 == 0:
            w_in = ev_w_in[j].astype(BF16)
            vc_x, vs_x, zs_x = _even_in(xs, rows_x, n1, mod, w_in, cs_ch)
            vc_c, vs_c, zs_c = _even_in(cs, rows_c, n1, mod, w_in, cs_ch)
            ya_x = _fourier(vc_x, vs_x, cl_x, sl_x, nb, length)
            m, bm, wh, at_re, at_im = _ssm_mats(ssm_a_re[j], ssm_a_im[j], ssm_log_dt[j], ssm_b_re[j],
                                                ssm_b_im[j], ssm_c_re[j], ssm_c_im[j], ssm_d[j])
            u = jnp.concatenate([_to_slabs(zs_c, nb, clen), _to_slabs(zs_x, nb, length)], axis=1)
            contrib = _ssm_contrib(u, bm)
            h_re, h_im = _ssm_scan(contrib, at_re, at_im, ncc, nct, nb)
            y = _ssm_read(u, h_re, h_im, m, wh)
            ys_x = _from_slabs(y[:, ncc * nb:], nb, length)
            w_glu = ssm_w_glu[j].astype(BF16)
            b_glu = ssm_b_glu[j].reshape(1, D_SSM)
            w_out = ev_w_out[j].astype(BF16)
            xs = _even_out(ya_x, ys_x, xs, rows_x, mod, w_glu, b_glu, w_out)
            if not last:
                ya_c = _fourier(vc_c, vs_c, cl_c, sl_c, nb, clen)
                ys_c = _from_slabs(y[:, :ncc * nb], nb, clen)
                cs = _even_out(ya_c, ys_c, cs, rows_c, mod, w_glu, b_glu, w_out)
        else:
            lam_init = 0.8 - 0.6 * math.exp(-0.3 * i)
            w_in = od_w_in[j].astype(BF16)
            gqk = jnp.stack([jnp.tile(od_q_norm[j], 2), jnp.tile(od_k_norm[j], 2)]).reshape(2, 1, LANES)
            qkv_x = _qkv(xs, rows_x, n1, mod, w_in, gqk, ones_bd, rope_tabs)
            qkv_c = _qkv(cs, rows_c, n1, mod, w_in, gqk, ones_bd, None)
            g_head = od_head_norm[j].reshape(1, DA_V_DIM)
            w_out = od_w_out[j].astype(BF16)
            o_x = _attention(qkv_x, [(qkv_x, length), (qkv_c, clen)], od_lambda[j], g_head, lam_init,
                             nb, length, _tile(length, 256))
            xs = _out_res(o_x, xs, rows_x, mod, w_out)
            if not last:
                o_c = _attention(qkv_c, [(qkv_c, clen)], od_lambda[j], g_head, lam_init, nb, clen, clen)
                cs = _out_res(o_c, cs, rows_c, mod, w_out)
        w1 = mlp_w1[i].astype(BF16)
        w2 = mlp_w2[i].astype(BF16)
        xs = _mlp(xs, rows_x, n2, mod, w1, w2)
        if not last:
            cs = _mlp(cs, rows_c, n2, mod, w1, w2)
    return xs.reshape(nb, length, d)
```

```python
import functools
import math

import jax
import jax.numpy as jnp
from jax import lax
from jax.experimental import pallas as pl
from jax.experimental.pallas import tpu as pltpu

F32 = jnp.float32
BF16 = jnp.bfloat16

D_MODEL = 1024
DEPTH = 4
GRID_W = 64
EPS = 1e-6
D_FF = 4 * D_MODEL
D_FOURIER = D_MODEL // 2
FOURIER_GROUPS = 4
FOURIER_CH = D_FOURIER // FOURIER_GROUPS
D_SSM = D_MODEL - D_FOURIER
SSM_GROUP = 16
SSM_GROUPS = D_SSM // SSM_GROUP
SSM_STATE = 64
DA_HEAD_DIM = 64
DA_V_DIM = 2 * DA_HEAD_DIM
DA_HEADS = D_MODEL // DA_V_DIM
ROPE_THETA = 10000.0

LANES = 128
SSM_CHUNK = 16
SLABS = D_SSM // LANES
SLAB_GROUPS = LANES // SSM_GROUP
SLAB_STATE = SLAB_GROUPS * SSM_STATE
VMEM_LIMIT = 56 * 1024 * 1024


def _params(n_axes):
    return pltpu.CompilerParams(dimension_semantics=("arbitrary",) * n_axes,
                                vmem_limit_bytes=VMEM_LIMIT)


def _tile(n, target, mult=8):
    t = min(n, target)
    while t > 1 and (n % t or t % mult):
        t -= 1
    return t if n % t == 0 else n


def _const_spec(shape):
    nd = len(shape)
    return pl.BlockSpec(shape, lambda *_: (0,) * nd, pipeline_mode=pl.Buffered(1))


def _sigmoid(x):
    return 1.0 / (1.0 + jnp.exp(-x))


def _gelu_tanh(x):
    c = math.sqrt(2.0 / math.pi)
    return x * (0.5 * (1.0 + jnp.tanh(c * (x + 0.044715 * (x * x * x)))))


def _norm_mod(x, g, shift, scale):
    ms = jnp.mean(x * x, axis=-1, keepdims=True)
    y = x * lax.rsqrt(ms + EPS) * g
    return y * (1.0 + scale) + shift


def _adaln_kernel(c_ref, w_ref, b_ref, o_ref):
    c = c_ref[...]
    s = (c * _sigmoid(c)).astype(BF16)
    o_ref[0] = jnp.dot(s, w_ref[0].astype(BF16), preferred_element_type=F32) + b_ref[0]


def _adaln(cond, ada_w, ada_b):
    depth, d, n = ada_w.shape
    r = cond.shape[0]
    tn = _tile(n, 1536, LANES)
    return pl.pallas_call(
        _adaln_kernel,
        out_shape=jax.ShapeDtypeStruct((depth, r, n), F32),
        grid=(depth, n // tn),
        in_specs=[pl.BlockSpec((r, d), lambda l, j: (0, 0)),
                  pl.BlockSpec((1, d, tn), lambda l, j: (l, 0, j)),
                  pl.BlockSpec((1, 1, tn), lambda l, j: (l, 0, j))],
        out_specs=pl.BlockSpec((1, r, tn), lambda l, j: (l, 0, j)),
        compiler_params=_params(2), name="adaln",
    )(cond, ada_w, ada_b.reshape(depth, 1, n))


class _Rows:
    def __init__(self, nb, rpb, tm, mod_row):
        self.nb, self.rpb, self.tm = nb, rpb, tm
        self.nt = rpb // tm
        self.grid = (nb, self.nt)
        self.mod_row = mod_row

    def rows(self, cols):
        nt = self.nt
        return pl.BlockSpec((self.tm, cols), lambda b, i: (b * nt + i, 0))

    def rows3(self, cols):
        return pl.BlockSpec((None, self.tm, cols), lambda b, i: (b, i, 0))

    def mod(self, k):
        if self.mod_row is None:
            return pl.BlockSpec((None, 1, D_MODEL), lambda b, i: (b, 0, k))
        r = self.mod_row
        return pl.BlockSpec((None, 1, D_MODEL), lambda b, i: (r, 0, k))


def _even_in_kernel(x_ref, g_ref, sh_ref, sc_ref, w_ref, cs_ref, vc_ref, vs_ref, zs_ref):
    h = _norm_mod(x_ref[...], g_ref[...], sh_ref[...], sc_ref[...]).astype(BF16)
    z = jnp.dot(h, w_ref[...], preferred_element_type=F32)
    zf = z[:, :D_FOURIER].astype(BF16)
    for gi in range(FOURIER_GROUPS):
        lo, hi = gi * FOURIER_CH, (gi + 1) * FOURIER_CH
        v = jnp.dot(zf[:, lo:hi], cs_ref[...], preferred_element_type=F32)
        vc_ref[:, lo:hi] = v[:, :FOURIER_CH].astype(BF16)
        vs_ref[:, lo:hi] = v[:, FOURIER_CH:].astype(BF16)
    zs_ref[...] = z[:, D_FOURIER:].astype(BF16)


def _even_in(x, rows, norm_g, mod, w_in, cs):
    n = x.shape[0]
    outs = tuple(jax.ShapeDtypeStruct((n, D_FOURIER), BF16) for _ in range(3))
    return pl.pallas_call(
        _even_in_kernel, out_shape=outs, grid=rows.grid,
        in_specs=[rows.rows(D_MODEL), _const_spec((1, D_MODEL)), rows.mod(0), rows.mod(1),
                  _const_spec((D_MODEL, D_MODEL)), _const_spec((FOURIER_CH, 2 * FOURIER_CH))],
        out_specs=tuple(rows.rows(D_FOURIER) for _ in range(3)),
        compiler_params=_params(2), name="even_in",
    )(x, norm_g, mod, mod, w_in, cs)


def _fourier_kernel(cl_ref, sl_ref, vc_ref, vs_ref, o_ref, *, scale):
    a = jnp.dot(cl_ref[...], vc_ref[0], preferred_element_type=F32)
    a = a - jnp.dot(sl_ref[...], vs_ref[0], preferred_element_type=F32)
    o_ref[0] = (a * scale).astype(BF16)


def _fourier(vc, vs, cl, sl, nb, length):
    vc3 = vc.reshape(nb, length, D_FOURIER)
    vs3 = vs.reshape(nb, length, D_FOURIER)
    scale = 1.0 / math.sqrt(length * FOURIER_CH)
    blk = pl.BlockSpec((1, length, D_FOURIER), lambda b: (b, 0, 0))
    out = pl.pallas_call(
        functools.partial(_fourier_kernel, scale=scale),
        out_shape=jax.ShapeDtypeStruct((nb, length, D_FOURIER), BF16),
        grid=(nb,),
        in_specs=[_const_spec((length, length)), _const_spec((length, length)), blk, blk],
        out_specs=blk,
        compiler_params=_params(1), name="fourier",
    )(cl, sl, vc3, vs3)
    return out.reshape(nb * length, D_FOURIER)


def _dft_mats(n):
    k = lax.iota(jnp.int32, n)
    ang = ((k[:, None] * k[None, :]) % n).astype(F32) * (2.0 * math.pi / n)
    return jnp.cos(ang), jnp.sin(ang)


def _ssm_mats(a_re, a_im, log_dt, b_re, b_im, c_re, c_im, d_skip):
    t = SSM_CHUNK
    a = lax.complex(a_re.astype(F32), a_im.astype(F32))
    adt = a * jnp.exp(log_dt.astype(F32))[..., None]
    a_bar = jnp.exp(adt)
    bb = ((a_bar - 1.0) / a)[..., None] * lax.complex(b_re.astype(F32), b_im.astype(F32))
    cc = lax.complex(c_re.astype(F32), c_im.astype(F32))
    apow = jnp.exp(adt[..., None] * jnp.arange(t + 1, dtype=F32))
    taps = jnp.einsum("dghp,dgpl,dgpk->dglhk", cc, apow[..., :t], bb).real
    s_idx = jnp.arange(t)[:, None]
    t_idx = jnp.arange(t)[None, :]
    fwd = jnp.where((s_idx <= t_idx)[None, :, :, None, None],
                    taps[0][:, jnp.clip(t_idx - s_idx, 0, t - 1)], 0.0)
    bwd = jnp.where((s_idx >= t_idx)[None, :, :, None, None],
                    taps[1][:, jnp.clip(s_idx - t_idx, 0, t - 1)], 0.0)
    skip = (jnp.eye(t, dtype=F32)[None, :, :, None, None]
            * (jnp.eye(SSM_GROUP, dtype=F32) * d_skip.astype(F32).reshape(SSM_GROUPS, SSM_GROUP, 1)
               )[:, None, None, :, :])
    mg = (fwd + bwd + skip).transpose(0, 1, 4, 2, 3)
    eye = jnp.eye(SLAB_GROUPS, dtype=F32)

    def slab_rows(x):
        x = x.reshape((SLABS, SLAB_GROUPS) + x.shape[1:])
        return x

    mg = slab_rows(mg)
    m = jnp.einsum("zashtk,ab->zsahtbk", mg, eye)
    m = m.reshape(SLABS, t * LANES, t * LANES)
    pf = apow[0][..., :t][..., ::-1]
    pb = apow[1][..., :t]
    cf = jnp.einsum("gps,gph->gshp", pf, bb[0])
    cb = jnp.einsum("gps,gph->gshp", pb, bb[1])
    parts = jnp.stack([cf.real, cf.imag, cb.real, cb.imag], axis=3)
    parts = slab_rows(parts)
    bm = jnp.einsum("zashqp,ab->zsahqbp", parts, eye)
    bm = bm.reshape(SLABS, t * LANES, 4 * SLAB_STATE)
    rf = jnp.einsum("ghp,gpt->gpth", cc[0], apow[0][..., 1:])
    rb = jnp.einsum("ghp,gpt->gpth", cc[1], apow[1][..., 1:][..., ::-1])
    rd = jnp.stack([rf.real, -rf.imag, rb.real, -rb.imag], axis=1)
    rd = slab_rows(rd)
    wh = jnp.einsum("zaqpth,ab->zqaptbh", rd, eye)
    wh = wh.reshape(SLABS, 4 * SLAB_STATE, t * LANES)
    at = apow[..., t].reshape(2, SLABS, SLAB_STATE)
    at = at.transpose(1, 0, 2).reshape(SLABS * 2, 1, SLAB_STATE)
    return m.astype(BF16), bm.astype(BF16), wh.astype(BF16), at.real, at.imag


def _ssm_contrib_kernel(u_ref, bm_ref, o_ref):
    o_ref[...] = jnp.dot(u_ref[...], bm_ref[...], preferred_element_type=F32)


def _ssm_contrib(u, bm):
    slabs, r, k = u.shape
    n = bm.shape[2]
    tr = _tile(r, 576)
    return pl.pallas_call(
        _ssm_contrib_kernel,
        out_shape=jax.ShapeDtypeStruct((slabs, r, n), F32),
        grid=(slabs, r // tr),
        in_specs=[pl.BlockSpec((None, tr, k), lambda s, i: (s, i, 0)),
                  pl.BlockSpec((None, k, n), lambda s, i: (s, 0, 0))],
        out_specs=pl.BlockSpec((None, tr, n), lambda s, i: (s, i, 0)),
        compiler_params=_params(2), name="ssm_contrib",
    )(u, bm)


def _ssm_scan_kernel(ar_ref, ai_ref, cr_ref, ci_ref, hr_ref, hi_ref, *, ncc, nct, rb):
    d = pl.program_id(1)
    lt = ar_ref.shape[-1]
    ar = jnp.broadcast_to(ar_ref[...], (rb, lt))
    ai = jnp.broadcast_to(ai_ref[...], (rb, lt))
    zero = jnp.zeros((rb, lt), F32)

    def advance(i_src, i_dst, hr, hi):
        r0 = pl.multiple_of(i_src * rb, rb)
        r1 = pl.multiple_of(i_dst * rb, rb)
        nr = ar * hr - ai * hi + cr_ref[pl.ds(r0, rb), :]
        ni = ar * hi + ai * hr + ci_ref[pl.ds(r0, rb), :]
        hr_ref[pl.ds(r1, rb), :] = nr
        hi_ref[pl.ds(r1, rb), :] = ni
        return nr, ni

    @pl.when(d == 0)
    def _():
        hr_ref[pl.ds(0, rb), :] = zero
        hi_ref[pl.ds(0, rb), :] = zero
        lax.fori_loop(0, nct - 1, lambda i, c: advance(i, i + 1, *c), (zero, zero))

    @pl.when(d == 1)
    def _():
        last_c = (ncc - 1) * rb
        hr_ref[pl.ds(last_c, rb), :] = zero
        hi_ref[pl.ds(last_c, rb), :] = zero
        c = lax.fori_loop(0, ncc - 1, lambda k, c: advance(ncc - 1 - k, ncc - 2 - k, *c), (zero, zero))
        c = advance(0, nct - 1, *c)
        lax.fori_loop(0, nct - 1 - ncc, lambda k, c: advance(nct - 1 - k, nct - 2 - k, *c), c)


def _ssm_scan(contrib, at_re, at_im, ncc, nct, rb):
    slabs, r, n = contrib.shape
    lt = LANES
    q = SLAB_STATE // lt
    re_spec = pl.BlockSpec((None, r, lt), lambda s, d, j: (s, 0, d * 2 * q + j))
    im_spec = pl.BlockSpec((None, r, lt), lambda s, d, j: (s, 0, d * 2 * q + q + j))
    a_spec = pl.BlockSpec((None, 1, lt), lambda s, d, j: (s * 2 + d, 0, j))
    h_spec = pl.BlockSpec((None, r, lt), lambda s, d, j: (s, 0, d * q + j))
    out = jax.ShapeDtypeStruct((slabs, r, n // 2), F32)
    return pl.pallas_call(
        functools.partial(_ssm_scan_kernel, ncc=ncc, nct=nct, rb=rb),
        out_shape=(out, out),
        grid=(slabs, 2, q),
        in_specs=[a_spec, a_spec, re_spec, im_spec],
        out_specs=(h_spec, h_spec),
        compiler_params=_params(3), name="ssm_scan",
    )(at_re, at_im, contrib, contrib)


def _ssm_read_kernel(u_ref, hr_ref, hi_ref, m_ref, wh_ref, o_ref):
    q = SLAB_STATE
    acc = jnp.dot(u_ref[...], m_ref[...], preferred_element_type=F32)
    for d in range(2):
        acc += jnp.dot(hr_ref[:, d * q:(d + 1) * q].astype(BF16), wh_ref[2 * d * q:(2 * d + 1) * q, :],
                       preferred_element_type=F32)
        acc += jnp.dot(hi_ref[:, d * q:(d + 1) * q].astype(BF16), wh_ref[(2 * d + 1) * q:(2 * d + 2) * q, :],
                       preferred_element_type=F32)
    o_ref[...] = acc.astype(BF16)


def _ssm_read(u, h_re, h_im, m, wh):
    slabs, r, k = u.shape
    ns = h_re.shape[2]
    nw = wh.shape[1]
    tr = _tile(r, 576)
    tn = k // 2
    return pl.pallas_call(
        _ssm_read_kernel,
        out_shape=jax.ShapeDtypeStruct((slabs, r, k), BF16),
        grid=(slabs, 2, r // tr),
        in_specs=[pl.BlockSpec((None, tr, k), lambda s, j, i: (s, i, 0)),
                  pl.BlockSpec((None, tr, ns), lambda s, j, i: (s, i, 0)),
                  pl.BlockSpec((None, tr, ns), lambda s, j, i: (s, i, 0)),
                  pl.BlockSpec((None, k, tn), lambda s, j, i: (s, 0, j)),
                  pl.BlockSpec((None, nw, tn), lambda s, j, i: (s, 0, j))],
        out_specs=pl.BlockSpec((None, tr, tn), lambda s, j, i: (s, i, j)),
        compiler_params=_params(3), name="ssm_read",
    )(u, h_re, h_im, m, wh)


def _to_slabs(zs, nb, length):
    nc = length // SSM_CHUNK
    u = zs.reshape(nb, nc, SSM_CHUNK, SLABS, LANES).transpose(3, 1, 0, 2, 4)
    return u.reshape(SLABS, nc * nb, SSM_CHUNK * LANES)


def _from_slabs(y, nb, length):
    nc = length // SSM_CHUNK
    y = y.reshape(SLABS, nc, nb, SSM_CHUNK, LANES).transpose(2, 1, 3, 0, 4)
    return y.reshape(nb * length, D_SSM)


def _even_out_kernel(ya_ref, ys_ref, x_ref, gate_ref, wg_ref, bg_ref, wa_ref, wb_ref, o_ref):
    y = _gelu_tanh(ys_ref[...].astype(F32))
    glu = y * _sigmoid(jnp.dot(y.astype(BF16), wg_ref[...], preferred_element_type=F32) + bg_ref[...])
    acc = jnp.dot(ya_ref[...], wa_ref[...], preferred_element_type=F32)
    acc += jnp.dot(glu.astype(BF16), wb_ref[...], preferred_element_type=F32)
    o_ref[...] = x_ref[...] + gate_ref[...] * acc


def _even_out(ya, ys, x, rows, mod, w_glu, b_glu, w_out):
    return pl.pallas_call(
        _even_out_kernel, out_shape=jax.ShapeDtypeStruct(x.shape, F32), grid=rows.grid,
        in_specs=[rows.rows(D_FOURIER), rows.rows(D_SSM), rows.rows(D_MODEL), rows.mod(2),
                  _const_spec((D_SSM, D_SSM)), _const_spec((1, D_SSM)),
                  _const_spec((D_FOURIER, D_MODEL)), _const_spec((D_SSM, D_MODEL))],
        out_specs=rows.rows(D_MODEL),
        compiler_params=_params(2), name="even_out",
    )(ya, ys, x, mod, w_glu, b_glu, w_out[:D_FOURIER], w_out[D_FOURIER:])


def _mlp_kernel(x_ref, g_ref, sh_ref, sc_ref, gate_ref, w1_ref, w2_ref, o_ref, *, fc):
    x = x_ref[...]
    h = _norm_mod(x, g_ref[...], sh_ref[...], sc_ref[...]).astype(BF16)
    acc = jnp.zeros(x.shape, F32)
    for k in range(D_FF // fc):
        a = jnp.dot(h, w1_ref[:, k * fc:(k + 1) * fc], preferred_element_type=F32)
        a = jnp.square(jnp.maximum(a, 0.0)).astype(BF16)
        acc += jnp.dot(a, w2_ref[k * fc:(k + 1) * fc, :], preferred_element_type=F32)
    o_ref[...] = x + gate_ref[...] * acc


def _mlp(x, rows, norm_g, mod, w1, w2):
    return pl.pallas_call(
        functools.partial(_mlp_kernel, fc=1024),
        out_shape=jax.ShapeDtypeStruct(x.shape, F32), grid=rows.grid,
        in_specs=[rows.rows(D_MODEL), _const_spec((1, D_MODEL)), rows.mod(3), rows.mod(4), rows.mod(5),
                  _const_spec((D_MODEL, D_FF)), _const_spec((D_FF, D_MODEL))],
        out_specs=rows.rows(D_MODEL),
        compiler_params=_params(2), name="mlp",
    )(x, norm_g, mod, mod, mod, w1, w2)


KEY_CHUNK = 256
V_ROWS = DA_V_DIM + 16
Q_SCALE = DA_HEAD_DIM ** -0.5 * math.log2(math.e)


def _qkv_kernel(*refs, rope):
    if rope:
        (x_ref, g_ref, sh_ref, sc_ref, w_ref, gqk_ref, ones_ref,
         cos_ref, sa_ref, sb_ref, qt_ref, k_ref, vt_ref) = refs
    else:
        x_ref, g_ref, sh_ref, sc_ref, w_ref, gqk_ref, ones_ref, qt_ref, k_ref, vt_ref = refs
    h = _norm_mod(x_ref[...], g_ref[...], sh_ref[...], sc_ref[...]).astype(BF16)
    y = jnp.dot(h, w_ref[...], preferred_element_type=F32)
    tm = y.shape[0]
    for s in range(2 * DA_HEADS):
        blk = y[:, s * LANES:(s + 1) * LANES]
        sq = blk * blk
        sq_hi = sq.astype(BF16)
        sq_lo = (sq - sq_hi.astype(F32)).astype(BF16)
        ssq = (jnp.dot(sq_hi, ones_ref[...], preferred_element_type=F32)
               + jnp.dot(sq_lo, ones_ref[...], preferred_element_type=F32))
        nrm = blk * lax.rsqrt(ssq * (1.0 / DA_HEAD_DIM) + EPS) * gqk_ref[s // DA_HEADS]
        if rope:
            nrm = (nrm * cos_ref[...] + pltpu.roll(nrm, LANES - 16, 1) * sa_ref[...]
                   + pltpu.roll(nrm, 16, 1) * sb_ref[...])
        if s < DA_HEADS:
            qt_ref[s] = (nrm * Q_SCALE).T.astype(BF16)
        else:
            hd = s - DA_HEADS
            k_ref[:, hd * LANES:(hd + 1) * LANES] = nrm.astype(BF16)
    ones = jnp.ones((V_ROWS - DA_V_DIM, tm), BF16)
    for hd in range(DA_HEADS):
        v = y[:, (2 * DA_HEADS + hd) * LANES:(2 * DA_HEADS + hd + 1) * LANES]
        vt_ref[hd, :DA_V_DIM, :] = v.T.astype(BF16)
        vt_ref[hd, DA_V_DIM:, :] = ones


def _qkv(x, rows, norm_g, mod, w_in, gqk, ones_bd, rope_tabs):
    nb, rpb, tm = rows.nb, rows.rpb, rows.tm
    assert tm % KEY_CHUNK == 0
    rope = rope_tabs is not None
    in_specs = [rows.rows(D_MODEL), _const_spec((1, D_MODEL)), rows.mod(0), rows.mod(1),
                _const_spec((D_MODEL, 3 * D_MODEL)), _const_spec((2, 1, LANES)),
                _const_spec((LANES, LANES))]
    args = [x, norm_g, mod, mod, w_in, gqk, ones_bd]
    if rope:
        tab = pl.BlockSpec((tm, LANES), lambda b, i: (i, 0))
        in_specs += [tab, tab, tab]
        args += list(rope_tabs)
    out_shape = (jax.ShapeDtypeStruct((nb, DA_HEADS, LANES, rpb), BF16),
                 jax.ShapeDtypeStruct((nb, rpb, D_MODEL), BF16),
                 jax.ShapeDtypeStruct((nb, DA_HEADS, V_ROWS, rpb), BF16))
    out_specs = (pl.BlockSpec((None, DA_HEADS, LANES, tm), lambda b, i: (b, 0, 0, i)),
                 pl.BlockSpec((None, tm, D_MODEL), lambda b, i: (b, i, 0)),
                 pl.BlockSpec((None, DA_HEADS, V_ROWS, tm), lambda b, i: (b, 0, 0, i)))
    return pl.pallas_call(
        functools.partial(_qkv_kernel, rope=rope),
        out_shape=out_shape, grid=rows.grid, in_specs=in_specs, out_specs=out_specs,
        compiler_params=_params(2), name="qkv",
    )(*args)


def _rope_tables(length):
    pos = jnp.arange(length, dtype=jnp.int32)
    row = (pos // GRID_W).astype(F32)
    col = (pos % GRID_W).astype(F32)
    half = DA_HEAD_DIM // 2
    inv_freq = ROPE_THETA ** (-jnp.arange(0, half, 2, dtype=F32) / half)
    lane = jnp.arange(LANES)
    f = lane % 16
    is_col = (lane % DA_HEAD_DIM) >= half
    second = (lane % half) >= 16
    ang = jnp.where(is_col[None, :], col[:, None], row[:, None]) * inv_freq[f][None, :]
    cos, sin = jnp.cos(ang), jnp.sin(ang)
    sin_a = jnp.where(second[None, :], 0.0, -sin)
    sin_b = jnp.where(second[None, :], sin, 0.0)
    return cos, sin_a, sin_b


def _attn_kernel(*refs, n_kv, lam_init):
    lp_ref, gh_ref, qt_ref = refs[:3]
    kv_refs = refs[3:3 + 2 * n_kv]
    o_ref, s_scr, p_scr = refs[3 + 2 * n_kv:]
    lp = lp_ref[...]
    lam = (jnp.exp(jnp.sum(lp[0:1] * lp[1:2], axis=-1, keepdims=True))
           - jnp.exp(jnp.sum(lp[2:3] * lp[3:4], axis=-1, keepdims=True)) + lam_init)
    qt = qt_ref[...]
    tq = qt.shape[1]
    row = lax.broadcasted_iota(jnp.int32, qt.shape, 0)
    zero = jnp.zeros_like(qt)
    qst = jnp.concatenate([jnp.where(row < DA_HEAD_DIM, qt, zero),
                           jnp.where(row >= DA_HEAD_DIM, qt, zero)], axis=1)
    kc = KEY_CHUNK
    m = jnp.full((8, 2 * tq), -jnp.inf, F32)
    total = 0
    for i in range(n_kv):
        k_ref = kv_refs[2 * i]
        for c in range(k_ref.shape[0] // kc):
            s = jnp.dot(k_ref[c * kc:(c + 1) * kc, :], qst, preferred_element_type=F32)
            s_scr[total:total + kc, :] = s
            m = jnp.maximum(m, s.reshape(kc // 8, 8, 2 * tq).max(axis=0))
            total += kc
    m = m.max(axis=0, keepdims=True)

    def expo(c, carry):
        r = pl.multiple_of(c * kc, kc)
        p_scr[pl.ds(r, kc), :] = jnp.exp2((s_scr[pl.ds(r, kc), :] - m).astype(BF16))
        return carry
    lax.fori_loop(0, total // kc, expo, 0)

    acc = jnp.zeros((V_ROWS, 2 * tq), F32)
    off = 0
    for i in range(n_kv):
        vt_ref = kv_refs[2 * i + 1]
        lk = vt_ref.shape[1]
        acc += jnp.dot(vt_ref[...], p_scr[off:off + lk, :], preferred_element_type=F32)
        off += lk
    o2 = acc[:DA_V_DIM] / acc[DA_V_DIM:DA_V_DIM + 1]
    o = o2[:, :tq] - lam * o2[:, tq:]
    o = o * lax.rsqrt(jnp.mean(o * o, axis=0, keepdims=True) + EPS) * gh_ref[...]
    o_ref[...] = (o * (1.0 - lam_init)).T.astype(BF16)


def _attention(qt, kv_list, lam_p, g_head, lam_init, tq):
    nb, _, _, lq = qt.shape
    in_specs = [_const_spec((4, DA_HEAD_DIM)), _const_spec((DA_V_DIM, 1)),
                pl.BlockSpec((None, None, LANES, tq), lambda b, h, i: (b, h, 0, i))]
    args = [lam_p, g_head, qt]
    total = 0
    for k, vt in kv_list:
        lk = k.shape[1]
        total += lk
        in_specs.append(pl.BlockSpec((None, lk, LANES), lambda b, h, i: (b, 0, h)))
        in_specs.append(pl.BlockSpec((None, None, V_ROWS, lk), lambda b, h, i: (b, h, 0, 0)))
        args += [k, vt]
    return pl.pallas_call(
        functools.partial(_attn_kernel, n_kv=len(kv_list), lam_init=lam_init),
        out_shape=jax.ShapeDtypeStruct((nb, lq, D_MODEL), BF16),
        grid=(nb, DA_HEADS, lq // tq),
        in_specs=in_specs,
        out_specs=pl.BlockSpec((None, tq, LANES), lambda b, h, i: (b, i, h)),
        scratch_shapes=[pltpu.VMEM((total, 2 * tq), F32), pltpu.VMEM((total, 2 * tq), BF16)],
        compiler_params=_params(3), name="diff_attn",
    )(*args)


def _out_res_kernel(a_ref, x_ref, gate_ref, w_ref, o_ref):
    acc = jnp.dot(a_ref[...], w_ref[...], preferred_element_type=F32)
    o_ref[...] = x_ref[...] + gate_ref[...] * acc


def _out_res(a, x, rows, mod, w):
    return pl.pallas_call(
        _out_res_kernel, out_shape=jax.ShapeDtypeStruct(x.shape, F32), grid=rows.grid,
        in_specs=[rows.rows3(D_MODEL), rows.rows(D_MODEL), rows.mod(2), _const_spec((D_MODEL, D_MODEL))],
        out_specs=rows.rows(D_MODEL),
        compiler_params=_params(2), name="out_res",
    )(a, x, mod, w)


def kernel(x, c, ctx, c_ctx, norm1_g, norm2_g, ada_w, ada_b, mlp_w1, mlp_w2, ev_w_in, ev_w_out, ssm_a_re, ssm_a_im, ssm_log_dt, ssm_b_re, ssm_b_im, ssm_c_re, ssm_c_im, ssm_d, ssm_w_glu, ssm_b_glu, od_w_in, od_w_out, od_q_norm, od_k_norm, od_lambda, od_head_norm):
    nb, length, d = x.shape
    clen = ctx.shape[1]
    assert d == D_MODEL and length % GRID_W == 0
    assert length % SSM_CHUNK == 0 and clen % SSM_CHUNK == 0

    xs = x.reshape(nb * length, d)
    cs = ctx.reshape(nb * clen, d)
    rows_x = _Rows(nb, length, _tile(length, 512), None)
    rows_c = _Rows(1, nb * clen, _tile(nb * clen, 512), nb)
    rows_cb = _Rows(nb, clen, clen, nb)

    n_cond = nb + 1
    pad = (-n_cond) % 8
    cond = jnp.concatenate([c, c_ctx[None, :], jnp.zeros((pad, d), F32)], axis=0)
    mods = _adaln(cond, ada_w, ada_b)[:, :n_cond].reshape(DEPTH, n_cond, 1, 6 * d)

    kf = lax.iota(jnp.int32, FOURIER_CH)
    ang = ((kf[:, None] * kf[None, :]) % FOURIER_CH).astype(F32) * (2.0 * math.pi / FOURIER_CH)
    cs_ch = jnp.concatenate([jnp.cos(ang), jnp.sin(ang)], axis=1).astype(BF16)
    cl_x, sl_x = (m.astype(BF16) for m in _dft_mats(length))
    cl_c, sl_c = (m.astype(BF16) for m in _dft_mats(clen))
    rope_tabs = _rope_tables(length)
    lane = jnp.arange(LANES)
    ones_bd = (lane[:, None] // DA_HEAD_DIM == lane[None, :] // DA_HEAD_DIM).astype(BF16)

    ncc = clen // SSM_CHUNK
    nct = ncc + length // SSM_CHUNK

    for i in range(DEPTH):
        last = i == DEPTH - 1
        j = i // 2
        mod = mods[i]
        n1 = norm1_g[i].reshape(1, d)
        n2 = norm2_g[i].reshape(1, d)
        if i % 2 == 0:
            w_in = ev_w_in[j].astype(BF16)
            vc_x, vs_x, zs_x = _even_in(xs, rows_x, n1, mod, w_in, cs_ch)
            vc_c, vs_c, zs_c = _even_in(cs, rows_c, n1, mod, w_in, cs_ch)
            ya_x = _fourier(vc_x, vs_x, cl_x, sl_x, nb, length)
            m, bm, wh, at_re, at_im = _ssm_mats(ssm_a_re[j], ssm_a_im[j], ssm_log_dt[j], ssm_b_re[j],
                                                ssm_b_im[j], ssm_c_re[j], ssm_c_im[j], ssm_d[j])
            u = jnp.concatenate([_to_slabs(zs_c, nb, clen), _to_slabs(zs_x, nb, length)], axis=1)
            contrib = _ssm_contrib(u, bm)
            h_re, h_im = _ssm_scan(contrib, at_re, at_im, ncc, nct, nb)
            y = _ssm_read(u, h_re, h_im, m, wh)
            ys_x = _from_slabs(y[:, ncc * nb:], nb, length)
            w_glu = ssm_w_glu[j].astype(BF16)
            b_glu = ssm_b_glu[j].reshape(1, D_SSM)
            w_out = ev_w_out[j].astype(BF16)
            xs = _even_out(ya_x, ys_x, xs, rows_x, mod, w_glu, b_glu, w_out)
            if not last:
                ya_c = _fourier(vc_c, vs_c, cl_c, sl_c, nb, clen)
                ys_c = _from_slabs(y[:, :ncc * nb], nb, clen)
                cs = _even_out(ya_c, ys_c, cs, rows_c, mod, w_glu, b_glu, w_out)
        else:
            lam_init = 0.8 - 0.6 * math.exp(-0.3 * i)
            w_in = od_w_in[j].astype(BF16)
            gqk = jnp.stack([jnp.tile(od_q_norm[j], 2), jnp.tile(od_k_norm[j], 2)]).reshape(2, 1, LANES)
            qt_x, k_x, vt_x = _qkv(xs, rows_x, n1, mod, w_in, gqk, ones_bd, rope_tabs)
            qt_c, k_c, vt_c = _qkv(cs, rows_cb, n1, mod, w_in, gqk, ones_bd, None)
            g_head = od_head_norm[j].reshape(DA_V_DIM, 1)
            w_out = od_w_out[j].astype(BF16)
            o_x = _attention(qt_x, [(k_x, vt_x), (k_c, vt_c)], od_lambda[j], g_head, lam_init,
                             _tile(length, 512, LANES))
            xs = _out_res(o_x, xs, rows_x, mod, w_out)
            if not last:
                o_c = _attention(qt_c, [(k_c, vt_c)], od_lambda[j], g_head, lam_init, clen)
                cs = _out_res(o_c, cs, rows_cb, mod, w_out)
        w1 = mlp_w1[i].astype(BF16)
        w2 = mlp_w2[i].astype(BF16)
        xs = _mlp(xs, rows_x, n2, mod, w1, w2)
        if not last:
            cs = _mlp(cs, rows_c, n2, mod, w1, w2)
    return xs.reshape(nb, length, d)
```

```python
import functools
import math

import jax
import jax.numpy as jnp
from jax import lax
from jax.experimental import pallas as pl
from jax.experimental.pallas import tpu as pltpu

F32 = jnp.float32
BF16 = jnp.bfloat16

D_MODEL = 1024
DEPTH = 4
GRID_W = 64
EPS = 1e-6
D_FF = 4 * D_MODEL
D_FOURIER = D_MODEL // 2
FOURIER_GROUPS = 4
FOURIER_CH = D_FOURIER // FOURIER_GROUPS
D_SSM = D_MODEL - D_FOURIER
SSM_GROUP = 16
SSM_GROUPS = D_SSM // SSM_GROUP
SSM_STATE = 64
DA_HEAD_DIM = 64
DA_V_DIM = 2 * DA_HEAD_DIM
DA_HEADS = D_MODEL // DA_V_DIM
ROPE_THETA = 10000.0

LANES = 128
SSM_CHUNK = 16
SLABS = D_SSM // LANES
SLAB_GROUPS = LANES // SSM_GROUP
SLAB_STATE = SLAB_GROUPS * SSM_STATE
VMEM_LIMIT = 56 * 1024 * 1024


def _params(n_axes, flags=None):
    return pltpu.CompilerParams(dimension_semantics=("arbitrary",) * n_axes,
                                vmem_limit_bytes=VMEM_LIMIT, flags=flags)


def _tile(n, target, mult=8):
    t = min(n, target)
    while t > 1 and (n % t or t % mult):
        t -= 1
    return t if n % t == 0 else n


def _const_spec(shape):
    nd = len(shape)
    return pl.BlockSpec(shape, lambda *_: (0,) * nd, pipeline_mode=pl.Buffered(1))


def _sigmoid(x):
    return 1.0 / (1.0 + jnp.exp(-x))


def _gelu_tanh(x):
    c = math.sqrt(2.0 / math.pi)
    return x * (0.5 * (1.0 + jnp.tanh(c * (x + 0.044715 * (x * x * x)))))


def _norm_mod(x, g, shift, scale):
    ms = jnp.mean(x * x, axis=-1, keepdims=True)
    y = x * lax.rsqrt(ms + EPS) * g
    return y * (1.0 + scale) + shift


def _adaln_kernel(c_ref, w_ref, b_ref, o_ref):
    c = c_ref[...]
    s = (c * _sigmoid(c)).astype(BF16)
    o_ref[0] = jnp.dot(s, w_ref[0].astype(BF16), preferred_element_type=F32) + b_ref[0]


def _adaln(cond, ada_w, ada_b):
    depth, d, n = ada_w.shape
    r = cond.shape[0]
    tn = _tile(n, 1536, LANES)
    return pl.pallas_call(
        _adaln_kernel,
        out_shape=jax.ShapeDtypeStruct((depth, r, n), F32),
        grid=(depth, n // tn),
        in_specs=[pl.BlockSpec((r, d), lambda l, j: (0, 0)),
                  pl.BlockSpec((1, d, tn), lambda l, j: (l, 0, j)),
                  pl.BlockSpec((1, 1, tn), lambda l, j: (l, 0, j))],
        out_specs=pl.BlockSpec((1, r, tn), lambda l, j: (l, 0, j)),
        compiler_params=_params(2), name="adaln",
    )(cond, ada_w, ada_b.reshape(depth, 1, n))


class _Rows:
    def __init__(self, nb, rpb, tm, mod_row):
        self.nb, self.rpb, self.tm = nb, rpb, tm
        self.nt = rpb // tm
        self.grid = (nb, self.nt)
        self.mod_row = mod_row

    def rows(self, cols):
        nt = self.nt
        return pl.BlockSpec((self.tm, cols), lambda b, i: (b * nt + i, 0))

    def rows3(self, cols):
        return pl.BlockSpec((None, self.tm, cols), lambda b, i: (b, i, 0))

    def mod(self, k):
        if self.mod_row is None:
            return pl.BlockSpec((None, 1, D_MODEL), lambda b, i: (b, 0, k))
        r = self.mod_row
        return pl.BlockSpec((None, 1, D_MODEL), lambda b, i: (r, 0, k))


def _even_in_kernel(x_ref, g_ref, sh_ref, sc_ref, w_ref, cs_ref, vc_ref, vs_ref, zs_ref):
    h = _norm_mod(x_ref[...], g_ref[...], sh_ref[...], sc_ref[...]).astype(BF16)
    z = jnp.dot(h, w_ref[...], preferred_element_type=F32)
    zf = z[:, :D_FOURIER].astype(BF16)
    for gi in range(FOURIER_GROUPS):
        lo, hi = gi * FOURIER_CH, (gi + 1) * FOURIER_CH
        v = jnp.dot(zf[:, lo:hi], cs_ref[...], preferred_element_type=F32)
        vc_ref[:, lo:hi] = v[:, :FOURIER_CH].astype(BF16)
        vs_ref[:, lo:hi] = v[:, FOURIER_CH:].astype(BF16)
    zs_ref[...] = z[:, D_FOURIER:].astype(BF16)


def _even_in(x, rows, norm_g, mod, w_in, cs):
    n = x.shape[0]
    outs = tuple(jax.ShapeDtypeStruct((n, D_FOURIER), BF16) for _ in range(3))
    return pl.pallas_call(
        _even_in_kernel, out_shape=outs, grid=rows.grid,
        in_specs=[rows.rows(D_MODEL), _const_spec((1, D_MODEL)), rows.mod(0), rows.mod(1),
                  _const_spec((D_MODEL, D_MODEL)), _const_spec((FOURIER_CH, 2 * FOURIER_CH))],
        out_specs=tuple(rows.rows(D_FOURIER) for _ in range(3)),
        compiler_params=_params(2), name="even_in",
    )(x, norm_g, mod, mod, w_in, cs)


def _fourier_kernel(cl_ref, sl_ref, vc_ref, vs_ref, o_ref, *, scale):
    a = jnp.dot(cl_ref[...], vc_ref[0], preferred_element_type=F32)
    a = a - jnp.dot(sl_ref[...], vs_ref[0], preferred_element_type=F32)
    o_ref[0] = (a * scale).astype(BF16)


def _fourier(vc, vs, cl, sl, nb, length):
    vc3 = vc.reshape(nb, length, D_FOURIER)
    vs3 = vs.reshape(nb, length, D_FOURIER)
    scale = 1.0 / math.sqrt(length * FOURIER_CH)
    blk = pl.BlockSpec((1, length, D_FOURIER), lambda b: (b, 0, 0))
    out = pl.pallas_call(
        functools.partial(_fourier_kernel, scale=scale),
        out_shape=jax.ShapeDtypeStruct((nb, length, D_FOURIER), BF16),
        grid=(nb,),
        in_specs=[_const_spec((length, length)), _const_spec((length, length)), blk, blk],
        out_specs=blk,
        compiler_params=_params(1), name="fourier",
    )(cl, sl, vc3, vs3)
    return out.reshape(nb * length, D_FOURIER)


def _dft_mats(n):
    k = lax.iota(jnp.int32, n)
    ang = ((k[:, None] * k[None, :]) % n).astype(F32) * (2.0 * math.pi / n)
    return jnp.cos(ang), jnp.sin(ang)


def _ssm_ops_kernel(ar_ref, ai_ref, ldt_ref, btr_ref, bti_ref, ctr_ref, cti_ref,
                    bm_ref, wh_ref, atr_ref, ati_ref, e_ref):
    t = SSM_CHUNK
    n = SLAB_STATE
    rows = lax.broadcasted_iota(jnp.int32, (LANES, n), 0)
    cols = lax.broadcasted_iota(jnp.int32, (LANES, n), 1)
    same_group = (rows // SSM_GROUP) == (cols // SSM_STATE)
    lag = lax.broadcasted_iota(jnp.int32, (t + 8, n), 0).astype(F32)
    nt_dims = (((1,), (1,)), ((), ()))
    for d in range(2):
        ar, ai = ar_ref[d], ai_ref[d]
        dt = jnp.exp(ldt_ref[d])
        xr, xi = ar * dt, ai * dt
        mag = jnp.exp(lag * xr)
        pr = mag * jnp.cos(lag * xi)
        pi = mag * jnp.sin(lag * xi)
        abr, abi = pr[1:2], pi[1:2]
        den = ar * ar + ai * ai
        fr = ((abr - 1.0) * ar + abi * ai) / den
        fi = (abi * ar - (abr - 1.0) * ai) / den
        br = jnp.where(same_group, btr_ref[d], 0.0)
        bi = jnp.where(same_group, bti_ref[d], 0.0)
        bbr = fr * br - fi * bi
        bbi = fr * bi + fi * br
        cr = jnp.where(same_group, ctr_ref[d], 0.0)
        ci = jnp.where(same_group, cti_ref[d], 0.0)
        for k in range(t):
            pkr, pki = pr[k:k + 1], pi[k:k + 1]
            x_r = bbr * pkr - bbi * pki
            x_i = bbr * pki + bbi * pkr
            s = t - 1 - k if d == 0 else k
            bm_ref[s * LANES:(s + 1) * LANES, 2 * d * n:(2 * d + 1) * n] = x_r.astype(BF16)
            bm_ref[s * LANES:(s + 1) * LANES, (2 * d + 1) * n:(2 * d + 2) * n] = x_i.astype(BF16)
            e_ref[d, k] = (lax.dot_general(x_r, cr, nt_dims, precision=lax.Precision.HIGHEST,
                                           preferred_element_type=F32)
                           - lax.dot_general(x_i, ci, nt_dims, precision=lax.Precision.HIGHEST,
                                             preferred_element_type=F32))
        for k in range(1, t + 1):
            pkr, pki = pr[k:k + 1], pi[k:k + 1]
            w_r = cr * pkr - ci * pki
            w_i = cr * pki + ci * pkr
            tt = k - 1 if d == 0 else t - k
            wh_ref[2 * d * n:(2 * d + 1) * n, tt * LANES:(tt + 1) * LANES] = w_r.T.astype(BF16)
            wh_ref[(2 * d + 1) * n:(2 * d + 2) * n, tt * LANES:(tt + 1) * LANES] = (-w_i).T.astype(BF16)
        atr_ref[d] = pr[t:t + 1]
        ati_ref[d] = pi[t:t + 1]


def _ssm_toeplitz_kernel(e_ref, dsk_ref, m_ref):
    t = SSM_CHUNK
    s = pl.program_id(1)
    r = lax.broadcasted_iota(jnp.int32, (LANES, LANES), 0)
    c = lax.broadcasted_iota(jnp.int32, (LANES, LANES), 1)
    skip = jnp.where(r == c, jnp.broadcast_to(dsk_ref[...], (LANES, LANES)), 0.0)
    for tt in range(t):
        blk = ((tt >= s).astype(F32) * e_ref[0, jnp.maximum(tt - s, 0)]
               + (s >= tt).astype(F32) * e_ref[1, jnp.maximum(s - tt, 0)]
               + (s == tt).astype(F32) * skip)
        m_ref[:, tt * LANES:(tt + 1) * LANES] = blk.astype(BF16)


def _ssm_ops(a_re, a_im, log_dt, b_re, b_im, c_re, c_im, d_skip):
    t = SSM_CHUNK
    n = SLAB_STATE

    def per_state(x):
        return x.astype(F32).reshape(2, SLABS, 1, n)

    def per_channel(x):
        x = x.astype(F32).reshape(2, SLABS, LANES, SSM_STATE)
        return jnp.tile(x, (1, 1, 1, SLAB_GROUPS))

    ldt = jnp.broadcast_to(log_dt[..., None], a_re.shape)
    vec = pl.BlockSpec((2, None, 1, n), lambda z: (0, z, 0, 0))
    mat = pl.BlockSpec((2, None, LANES, n), lambda z: (0, z, 0, 0))
    bm, wh, at_re, at_im, e = pl.pallas_call(
        _ssm_ops_kernel,
        out_shape=(jax.ShapeDtypeStruct((SLABS, t * LANES, 4 * n), BF16),
                   jax.ShapeDtypeStruct((SLABS, 4 * n, t * LANES), BF16),
                   jax.ShapeDtypeStruct((SLABS, 2, 1, n), F32),
                   jax.ShapeDtypeStruct((SLABS, 2, 1, n), F32),
                   jax.ShapeDtypeStruct((SLABS, 2, t, LANES, LANES), F32)),
        grid=(SLABS,),
        in_specs=[vec, vec, vec, mat, mat, mat, mat],
        out_specs=(pl.BlockSpec((None, t * LANES, 4 * n), lambda z: (z, 0, 0)),
                   pl.BlockSpec((None, 4 * n, t * LANES), lambda z: (z, 0, 0)),
                   pl.BlockSpec((None, 2, 1, n), lambda z: (z, 0, 0, 0)),
                   pl.BlockSpec((None, 2, 1, n), lambda z: (z, 0, 0, 0)),
                   pl.BlockSpec((None, 2, t, LANES, LANES), lambda z: (z, 0, 0, 0, 0))),
        compiler_params=_params(1), name="ssm_ops",
    )(per_state(a_re), per_state(a_im), per_state(ldt),
      per_channel(b_re.transpose(0, 1, 3, 2)), per_channel(b_im.transpose(0, 1, 3, 2)),
      per_channel(c_re), per_channel(c_im))
    m = pl.pallas_call(
        _ssm_toeplitz_kernel,
        out_shape=jax.ShapeDtypeStruct((SLABS, t * LANES, t * LANES), BF16),
        grid=(SLABS, t),
        in_specs=[pl.BlockSpec((None, 2, t, LANES, LANES), lambda z, s: (z, 0, 0, 0, 0)),
                  pl.BlockSpec((None, 1, LANES), lambda z, s: (z, 0, 0))],
        out_specs=pl.BlockSpec((None, LANES, t * LANES), lambda z, s: (z, s, 0)),
        compiler_params=_params(2), name="ssm_toeplitz",
    )(e, d_skip.astype(F32).reshape(SLABS, 1, LANES))
    return m, bm, wh, at_re.reshape(SLABS * 2, 1, n), at_im.reshape(SLABS * 2, 1, n)


def _ssm_contrib_kernel(u_ref, bm_ref, o_ref):
    o_ref[...] = jnp.dot(u_ref[...], bm_ref[...], preferred_element_type=F32)


def _ssm_contrib(u, bm):
    slabs, r, k = u.shape
    n = bm.shape[2]
    tr = _tile(r, 576)
    return pl.pallas_call(
        _ssm_contrib_kernel,
        out_shape=jax.ShapeDtypeStruct((slabs, r, n), F32),
        grid=(slabs, r // tr),
        in_specs=[pl.BlockSpec((None, tr, k), lambda s, i: (s, i, 0)),
                  pl.BlockSpec((None, k, n), lambda s, i: (s, 0, 0))],
        out_specs=pl.BlockSpec((None, tr, n), lambda s, i: (s, i, 0)),
        compiler_params=_params(2), name="ssm_contrib",
    )(u, bm)


def _ssm_scan_kernel(ar_ref, ai_ref, cr_ref, ci_ref, hr_ref, hi_ref, *, ncc, nct, rb):
    d = pl.program_id(1)
    lt = ar_ref.shape[-1]
    ar = jnp.broadcast_to(ar_ref[...], (rb, lt))
    ai = jnp.broadcast_to(ai_ref[...], (rb, lt))
    zero = jnp.zeros((rb, lt), F32)

    def advance(i_src, i_dst, hr, hi):
        r0 = pl.multiple_of(i_src * rb, rb)
        r1 = pl.multiple_of(i_dst * rb, rb)
        nr = ar * hr - ai * hi + cr_ref[pl.ds(r0, rb), :]
        ni = ar * hi + ai * hr + ci_ref[pl.ds(r0, rb), :]
        hr_ref[pl.ds(r1, rb), :] = nr
        hi_ref[pl.ds(r1, rb), :] = ni
        return nr, ni

    @pl.when(d == 0)
    def _():
        hr_ref[pl.ds(0, rb), :] = zero
        hi_ref[pl.ds(0, rb), :] = zero
        lax.fori_loop(0, nct - 1, lambda i, c: advance(i, i + 1, *c), (zero, zero))

    @pl.when(d == 1)
    def _():
        last_c = (ncc - 1) * rb
        hr_ref[pl.ds(last_c, rb), :] = zero
        hi_ref[pl.ds(last_c, rb), :] = zero
        c = lax.fori_loop(0, ncc - 1, lambda k, c: advance(ncc - 1 - k, ncc - 2 - k, *c), (zero, zero))
        c = advance(0, nct - 1, *c)
        lax.fori_loop(0, nct - 1 - ncc, lambda k, c: advance(nct - 1 - k, nct - 2 - k, *c), c)


def _ssm_scan(contrib, at_re, at_im, ncc, nct, rb):
    slabs, r, n = contrib.shape
    lt = LANES
    q = SLAB_STATE // lt
    re_spec = pl.BlockSpec((None, r, lt), lambda s, d, j: (s, 0, d * 2 * q + j))
    im_spec = pl.BlockSpec((None, r, lt), lambda s, d, j: (s, 0, d * 2 * q + q + j))
    a_spec = pl.BlockSpec((None, 1, lt), lambda s, d, j: (s * 2 + d, 0, j))
    h_spec = pl.BlockSpec((None, r, lt), lambda s, d, j: (s, 0, d * q + j))
    out = jax.ShapeDtypeStruct((slabs, r, n // 2), F32)
    return pl.pallas_call(
        functools.partial(_ssm_scan_kernel, ncc=ncc, nct=nct, rb=rb),
        out_shape=(out, out),
        grid=(slabs, 2, q),
        in_specs=[a_spec, a_spec, re_spec, im_spec],
        out_specs=(h_spec, h_spec),
        compiler_params=_params(3), name="ssm_scan",
    )(at_re, at_im, contrib, contrib)


def _ssm_read_kernel(u_ref, hr_ref, hi_ref, m_ref, wh_ref, o_ref):
    q = SLAB_STATE
    acc = jnp.dot(u_ref[...], m_ref[...], preferred_element_type=F32)
    for d in range(2):
        acc += jnp.dot(hr_ref[:, d * q:(d + 1) * q].astype(BF16), wh_ref[2 * d * q:(2 * d + 1) * q, :],
                       preferred_element_type=F32)
        acc += jnp.dot(hi_ref[:, d * q:(d + 1) * q].astype(BF16), wh_ref[(2 * d + 1) * q:(2 * d + 2) * q, :],
                       preferred_element_type=F32)
    o_ref[...] = acc.astype(BF16)


def _ssm_read(u, h_re, h_im, m, wh):
    slabs, r, k = u.shape
    ns = h_re.shape[2]
    nw = wh.shape[1]
    tr = _tile(r, 576)
    tn = k // 2
    return pl.pallas_call(
        _ssm_read_kernel,
        out_shape=jax.ShapeDtypeStruct((slabs, r, k), BF16),
        grid=(slabs, 2, r // tr),
        in_specs=[pl.BlockSpec((None, tr, k), lambda s, j, i: (s, i, 0)),
                  pl.BlockSpec((None, tr, ns), lambda s, j, i: (s, i, 0)),
                  pl.BlockSpec((None, tr, ns), lambda s, j, i: (s, i, 0)),
                  pl.BlockSpec((None, k, tn), lambda s, j, i: (s, 0, j)),
                  pl.BlockSpec((None, nw, tn), lambda s, j, i: (s, 0, j))],
        out_specs=pl.BlockSpec((None, tr, tn), lambda s, j, i: (s, i, j)),
        compiler_params=_params(3), name="ssm_read",
    )(u, h_re, h_im, m, wh)


def _to_slabs(zs, nb, length):
    nc = length // SSM_CHUNK
    u = zs.reshape(nb, nc, SSM_CHUNK, SLABS, LANES).transpose(3, 1, 0, 2, 4)
    return u.reshape(SLABS, nc * nb, SSM_CHUNK * LANES)


def _from_slabs(y, nb, length):
    nc = length // SSM_CHUNK
    y = y.reshape(SLABS, nc, nb, SSM_CHUNK, LANES).transpose(2, 1, 3, 0, 4)
    return y.reshape(nb * length, D_SSM)


def _even_out_kernel(ya_ref, ys_ref, x_ref, gate_ref, wg_ref, bg_ref, wa_ref, wb_ref, o_ref):
    y = _gelu_tanh(ys_ref[...].astype(F32))
    glu = y * _sigmoid(jnp.dot(y.astype(BF16), wg_ref[...], preferred_element_type=F32) + bg_ref[...])
    acc = jnp.dot(ya_ref[...], wa_ref[...], preferred_element_type=F32)
    acc += jnp.dot(glu.astype(BF16), wb_ref[...], preferred_element_type=F32)
    o_ref[...] = x_ref[...] + gate_ref[...] * acc


def _even_out(ya, ys, x, rows, mod, w_glu, b_glu, w_out):
    return pl.pallas_call(
        _even_out_kernel, out_shape=jax.ShapeDtypeStruct(x.shape, F32), grid=rows.grid,
        in_specs=[rows.rows(D_FOURIER), rows.rows(D_SSM), rows.rows(D_MODEL), rows.mod(2),
                  _const_spec((D_SSM, D_SSM)), _const_spec((1, D_SSM)),
                  _const_spec((D_FOURIER, D_MODEL)), _const_spec((D_SSM, D_MODEL))],
        out_specs=rows.rows(D_MODEL),
        compiler_params=_params(2), name="even_out",
    )(ya, ys, x, mod, w_glu, b_glu, w_out[:D_FOURIER], w_out[D_FOURIER:])


def _mlp_kernel(x_ref, g_ref, sh_ref, sc_ref, gate_ref, w1_ref, w2_ref, o_ref, *, fc):
    x = x_ref[...]
    h = _norm_mod(x, g_ref[...], sh_ref[...], sc_ref[...]).astype(BF16)
    acc = jnp.zeros(x.shape, F32)
    for k in range(D_FF // fc):
        a = jnp.dot(h, w1_ref[:, k * fc:(k + 1) * fc], preferred_element_type=F32)
        a = jnp.square(jnp.maximum(a, 0.0)).astype(BF16)
        acc += jnp.dot(a, w2_ref[k * fc:(k + 1) * fc, :], preferred_element_type=F32)
    o_ref[...] = x + gate_ref[...] * acc


def _mlp(x, rows, norm_g, mod, w1, w2):
    return pl.pallas_call(
        functools.partial(_mlp_kernel, fc=1024),
        out_shape=jax.ShapeDtypeStruct(x.shape, F32), grid=rows.grid,
        in_specs=[rows.rows(D_MODEL), _const_spec((1, D_MODEL)), rows.mod(3), rows.mod(4), rows.mod(5),
                  _const_spec((D_MODEL, D_FF)), _const_spec((D_FF, D_MODEL))],
        out_specs=rows.rows(D_MODEL),
        compiler_params=_params(2), name="mlp",
    )(x, norm_g, mod, mod, mod, w1, w2)


KEY_CHUNK = 256
V_ROWS = DA_V_DIM + 16
Q_SCALE = DA_HEAD_DIM ** -0.5 * math.log2(math.e)


def _qkv_kernel(*refs, rope):
    if rope:
        (x_ref, g_ref, sh_ref, sc_ref, w_ref, gqk_ref, ones_ref,
         cos_ref, sa_ref, sb_ref, qt_ref, k_ref, vt_ref) = refs
    else:
        x_ref, g_ref, sh_ref, sc_ref, w_ref, gqk_ref, ones_ref, qt_ref, k_ref, vt_ref = refs
    h = _norm_mod(x_ref[...], g_ref[...], sh_ref[...], sc_ref[...]).astype(BF16)
    y = jnp.dot(h, w_ref[...], preferred_element_type=F32)
    tm = y.shape[0]
    for s in range(2 * DA_HEADS):
        blk = y[:, s * LANES:(s + 1) * LANES]
        sq = blk * blk
        sq_hi = sq.astype(BF16)
        sq_lo = (sq - sq_hi.astype(F32)).astype(BF16)
        ssq = (jnp.dot(sq_hi, ones_ref[...], preferred_element_type=F32)
               + jnp.dot(sq_lo, ones_ref[...], preferred_element_type=F32))
        nrm = blk * lax.rsqrt(ssq * (1.0 / DA_HEAD_DIM) + EPS) * gqk_ref[s // DA_HEADS]
        if rope:
            nrm = (nrm * cos_ref[...] + pltpu.roll(nrm, LANES - 16, 1) * sa_ref[...]
                   + pltpu.roll(nrm, 16, 1) * sb_ref[...])
        if s < DA_HEADS:
            qt_ref[s] = (nrm * Q_SCALE).T.astype(BF16)
        else:
            hd = s - DA_HEADS
            k_ref[:, hd * LANES:(hd + 1) * LANES] = nrm.astype(BF16)
    ones = jnp.ones((V_ROWS - DA_V_DIM, tm), BF16)
    for hd in range(DA_HEADS):
        v = y[:, (2 * DA_HEADS + hd) * LANES:(2 * DA_HEADS + hd + 1) * LANES]
        vt_ref[hd, :DA_V_DIM, :] = v.T.astype(BF16)
        vt_ref[hd, DA_V_DIM:, :] = ones


def _qkv(x, rows, norm_g, mod, w_in, gqk, ones_bd, rope_tabs):
    nb, rpb, tm = rows.nb, rows.rpb, rows.tm
    assert tm % KEY_CHUNK == 0
    rope = rope_tabs is not None
    in_specs = [rows.rows(D_MODEL), _const_spec((1, D_MODEL)), rows.mod(0), rows.mod(1),
                _const_spec((D_MODEL, 3 * D_MODEL)), _const_spec((2, 1, LANES)),
                _const_spec((LANES, LANES))]
    args = [x, norm_g, mod, mod, w_in, gqk, ones_bd]
    if rope:
        tab = pl.BlockSpec((tm, LANES), lambda b, i: (i, 0))
        in_specs += [tab, tab, tab]
        args += list(rope_tabs)
    out_shape = (jax.ShapeDtypeStruct((nb, DA_HEADS, LANES, rpb), BF16),
                 jax.ShapeDtypeStruct((nb, rpb, D_MODEL), BF16),
                 jax.ShapeDtypeStruct((nb, DA_HEADS, V_ROWS, rpb), BF16))
    out_specs = (pl.BlockSpec((None, DA_HEADS, LANES, tm), lambda b, i: (b, 0, 0, i)),
                 pl.BlockSpec((None, tm, D_MODEL), lambda b, i: (b, i, 0)),
                 pl.BlockSpec((None, DA_HEADS, V_ROWS, tm), lambda b, i: (b, 0, 0, i)))
    return pl.pallas_call(
        functools.partial(_qkv_kernel, rope=rope),
        out_shape=out_shape, grid=rows.grid, in_specs=in_specs, out_specs=out_specs,
        compiler_params=_params(2), name="qkv",
    )(*args)


def _rope_tables(length):
    pos = jnp.arange(length, dtype=jnp.int32)
    row = (pos // GRID_W).astype(F32)
    col = (pos % GRID_W).astype(F32)
    half = DA_HEAD_DIM // 2
    inv_freq = ROPE_THETA ** (-jnp.arange(0, half, 2, dtype=F32) / half)
    lane = jnp.arange(LANES)
    f = lane % 16
    is_col = (lane % DA_HEAD_DIM) >= half
    second = (lane % half) >= 16
    ang = jnp.where(is_col[None, :], col[:, None], row[:, None]) * inv_freq[f][None, :]
    cos, sin = jnp.cos(ang), jnp.sin(ang)
    sin_a = jnp.where(second[None, :], 0.0, -sin)
    sin_b = jnp.where(second[None, :], sin, 0.0)
    return cos, sin_a, sin_b


def _attn_kernel(*refs, n_kv, lam_init, tq):
    lp_ref, gh_ref, qt_ref = refs[:3]
    kv_refs = refs[3:3 + 2 * n_kv]
    o_ref = refs[3 + 2 * n_kv]
    bufs = refs[4 + 2 * n_kv:]
    lp = lp_ref[...]
    lam = (jnp.exp(jnp.sum(lp[0:1] * lp[1:2], axis=-1, keepdims=True))
           - jnp.exp(jnp.sum(lp[2:3] * lp[3:4], axis=-1, keepdims=True)) + lam_init)
    nq = qt_ref.shape[1] // tq
    kc = KEY_CHUNK

    key_chunks = []
    for i in range(n_kv):
        for c in range(kv_refs[2 * i].shape[0] // kc):
            key_chunks.append((kv_refs[2 * i], kv_refs[2 * i + 1], c))

    def masked_q(j):
        qt = qt_ref[:, j * tq:(j + 1) * tq]
        row = lax.broadcasted_iota(jnp.int32, qt.shape, 0)
        zero = jnp.zeros_like(qt)
        return jnp.concatenate([jnp.where(row < DA_HEAD_DIM, qt, zero),
                                jnp.where(row >= DA_HEAD_DIM, qt, zero)], axis=1)

    def score_chunk(n, qst, s_scr, m):
        k_ref, _, c = key_chunks[n]
        s = jnp.dot(k_ref[c * kc:(c + 1) * kc, :], qst, preferred_element_type=F32)
        s_scr[n * kc:(n + 1) * kc, :] = s
        return jnp.maximum(m, s.reshape(kc // 8, 8, 2 * tq).max(axis=0))

    def value_chunk(n, s_scr, m, acc):
        _, vt_ref, c = key_chunks[n]
        p = jnp.exp2((s_scr[n * kc:(n + 1) * kc, :] - m).astype(BF16))
        return acc + jnp.dot(vt_ref[:, c * kc:(c + 1) * kc], p, preferred_element_type=F32)

    def finish(j, acc):
        o2 = acc[:DA_V_DIM] / acc[DA_V_DIM:DA_V_DIM + 1]
        o = o2[:, :tq] - lam * o2[:, tq:]
        o = o * lax.rsqrt(jnp.mean(o * o, axis=0, keepdims=True) + EPS) * gh_ref[...]
        o_ref[j * tq:(j + 1) * tq, :] = (o * (1.0 - lam_init)).T.astype(BF16)

    neg = jnp.full((8, 2 * tq), -jnp.inf, F32)
    qst = masked_q(0)
    m8 = neg
    for n in range(len(key_chunks)):
        m8 = score_chunk(n, qst, bufs[0], m8)
    for j in range(nq):
        m = m8.max(axis=0, keepdims=True)
        acc = jnp.zeros((V_ROWS, 2 * tq), F32)
        if j + 1 < nq:
            qst = masked_q(j + 1)
            m8 = neg
        for n in range(len(key_chunks)):
            if j + 1 < nq:
                m8 = score_chunk(n, qst, bufs[(j + 1) % 2], m8)
            acc = value_chunk(n, bufs[j % 2], m, acc)
        finish(j, acc)


def _attention(qt, kv_list, lam_p, g_head, lam_init, tq):
    nb, _, _, lq = qt.shape
    in_specs = [_const_spec((4, DA_HEAD_DIM)), _const_spec((DA_V_DIM, 1)),
                pl.BlockSpec((None, None, LANES, lq), lambda b, h: (b, h, 0, 0))]
    args = [lam_p, g_head, qt]
    total = 0
    for k, vt in kv_list:
        lk = k.shape[1]
        total += lk
        in_specs.append(pl.BlockSpec((None, lk, LANES), lambda b, h: (b, 0, h)))
        in_specs.append(pl.BlockSpec((None, None, V_ROWS, lk), lambda b, h: (b, h, 0, 0)))
        args += [k, vt]
    n_bufs = min(2, lq // tq)
    return pl.pallas_call(
        functools.partial(_attn_kernel, n_kv=len(kv_list), lam_init=lam_init, tq=tq),
        out_shape=jax.ShapeDtypeStruct((nb, lq, D_MODEL), BF16),
        grid=(nb, DA_HEADS),
        in_specs=in_specs,
        out_specs=pl.BlockSpec((None, lq, LANES), lambda b, h: (b, 0, h)),
        scratch_shapes=[pltpu.VMEM((total, 2 * tq), F32)] * n_bufs,
        compiler_params=_params(2), name="diff_attn",
    )(*args)


def _out_res_kernel(a_ref, x_ref, gate_ref, w_ref, o_ref):
    acc = jnp.dot(a_ref[...], w_ref[...], preferred_element_type=F32)
    o_ref[...] = x_ref[...] + gate_ref[...] * acc


def _out_res(a, x, rows, mod, w):
    return pl.pallas_call(
        _out_res_kernel, out_shape=jax.ShapeDtypeStruct(x.shape, F32), grid=rows.grid,
        in_specs=[rows.rows3(D_MODEL), rows.rows(D_MODEL), rows.mod(2), _const_spec((D_MODEL, D_MODEL))],
        out_specs=rows.rows(D_MODEL),
        compiler_params=_params(2), name="out_res",
    )(a, x, mod, w)


def kernel(x, c, ctx, c_ctx, norm1_g, norm2_g, ada_w, ada_b, mlp_w1, mlp_w2, ev_w_in, ev_w_out, ssm_a_re, ssm_a_im, ssm_log_dt, ssm_b_re, ssm_b_im, ssm_c_re, ssm_c_im, ssm_d, ssm_w_glu, ssm_b_glu, od_w_in, od_w_out, od_q_norm, od_k_norm, od_lambda, od_head_norm):
    nb, length, d = x.shape
    clen = ctx.shape[1]
    assert d == D_MODEL and length % GRID_W == 0
    assert length % SSM_CHUNK == 0 and clen % SSM_CHUNK == 0

    xs = x.reshape(nb * length, d)
    cs = ctx.reshape(nb * clen, d)
    rows_x = _Rows(nb, length, _tile(length, 512), None)
    rows_c = _Rows(1, nb * clen, _tile(nb * clen, 512), nb)
    rows_cb = _Rows(nb, clen, clen, nb)

    n_cond = nb + 1
    pad = (-n_cond) % 8
    cond = jnp.concatenate([c, c_ctx[None, :], jnp.zeros((pad, d), F32)], axis=0)
    mods = _adaln(cond, ada_w, ada_b)[:, :n_cond].reshape(DEPTH, n_cond, 1, 6 * d)

    kf = lax.iota(jnp.int32, FOURIER_CH)
    ang = ((kf[:, None] * kf[None, :]) % FOURIER_CH).astype(F32) * (2.0 * math.pi / FOURIER_CH)
    cs_ch = jnp.concatenate([jnp.cos(ang), jnp.sin(ang)], axis=1).astype(BF16)
    cl_x, sl_x = (m.astype(BF16) for m in _dft_mats(length))
    cl_c, sl_c = (m.astype(BF16) for m in _dft_mats(clen))
    rope_tabs = _rope_tables(length)
    lane = jnp.arange(LANES)
    ones_bd = (lane[:, None] // DA_HEAD_DIM == lane[None, :] // DA_HEAD_DIM).astype(BF16)

    ncc = clen // SSM_CHUNK
    nct = ncc + length // SSM_CHUNK

    for i in range(DEPTH):
        last = i == DEPTH - 1
        j = i // 2
        mod = mods[i]
        n1 = norm1_g[i].reshape(1, d)
        n2 = norm2_g[i].reshape(1, d)
        if i % 2 == 0:
            w_in = ev_w_in[j].astype(BF16)
            vc_x, vs_x, zs_x = _even_in(xs, rows_x, n1, mod, w_in, cs_ch)
            vc_c, vs_c, zs_c = _even_in(cs, rows_c, n1, mod, w_in, cs_ch)
            ya_x = _fourier(vc_x, vs_x, cl_x, sl_x, nb, length)
            m, bm, wh, at_re, at_im = _ssm_ops(ssm_a_re[j], ssm_a_im[j], ssm_log_dt[j], ssm_b_re[j],
                                                ssm_b_im[j], ssm_c_re[j], ssm_c_im[j], ssm_d[j])
            u = jnp.concatenate([_to_slabs(zs_c, nb, clen), _to_slabs(zs_x, nb, length)], axis=1)
            contrib = _ssm_contrib(u, bm)
            h_re, h_im = _ssm_scan(contrib, at_re, at_im, ncc, nct, nb)
            y = _ssm_read(u, h_re, h_im, m, wh)
            ys_x = _from_slabs(y[:, ncc * nb:], nb, length)
            w_glu = ssm_w_glu[j].astype(BF16)
            b_glu = ssm_b_glu[j].reshape(1, D_SSM)
            w_out = ev_w_out[j].astype(BF16)
            xs = _even_out(ya_x, ys_x, xs, rows_x, mod, w_glu, b_glu, w_out)
            if not last:
                ya_c = _fourier(vc_c, vs_c, cl_c, sl_c, nb, clen)
                ys_c = _from_slabs(y[:, :ncc * nb], nb, clen)
                cs = _even_out(ya_c, ys_c, cs, rows_c, mod, w_glu, b_glu, w_out)
        else:
            lam_init = 0.8 - 0.6 * math.exp(-0.3 * i)
            w_in = od_w_in[j].astype(BF16)
            gqk = jnp.stack([jnp.tile(od_q_norm[j], 2), jnp.tile(od_k_norm[j], 2)]).reshape(2, 1, LANES)
            qt_x, k_x, vt_x = _qkv(xs, rows_x, n1, mod, w_in, gqk, ones_bd, rope_tabs)
            qt_c, k_c, vt_c = _qkv(cs, rows_cb, n1, mod, w_in, gqk, ones_bd, None)
            g_head = od_head_norm[j].reshape(DA_V_DIM, 1)
            w_out = od_w_out[j].astype(BF16)
            o_x = _attention(qt_x, [(k_x, vt_x), (k_c, vt_c)], od_lambda[j], g_head, lam_init,
                             _tile(length, 512, LANES))
            xs = _out_res(o_x, xs, rows_x, mod, w_out)
            if not last:
                o_c = _attention(qt_c, [(k_c, vt_c)], od_lambda[j], g_head, lam_init, clen)
                cs = _out_res(o_c, cs, rows_cb, mod, w_out)
        w1 = mlp_w1[i].astype(BF16)
        w2 = mlp_w2[i].astype(BF16)
        xs = _mlp(xs, rows_x, n2, mod, w1, w2)
        if not last:
            cs = _mlp(cs, rows_c, n2, mod, w1, w2)
    return xs.reshape(nb, length, d)
```

```python
import functools
import math

import jax
import jax.numpy as jnp
from jax import lax
from jax.experimental import pallas as pl
from jax.experimental.pallas import tpu as pltpu

F32 = jnp.float32
BF16 = jnp.bfloat16

D_MODEL = 1024
DEPTH = 4
GRID_W = 64
EPS = 1e-6
D_FF = 4 * D_MODEL
D_FOURIER = D_MODEL // 2
FOURIER_GROUPS = 4
FOURIER_CH = D_FOURIER // FOURIER_GROUPS
D_SSM = D_MODEL - D_FOURIER
SSM_GROUP = 16
SSM_GROUPS = D_SSM // SSM_GROUP
SSM_STATE = 64
DA_HEAD_DIM = 64
DA_V_DIM = 2 * DA_HEAD_DIM
DA_HEADS = D_MODEL // DA_V_DIM
ROPE_THETA = 10000.0

LANES = 128
SSM_CHUNK = 16
SLABS = D_SSM // LANES
SLAB_GROUPS = LANES // SSM_GROUP
SLAB_STATE = SLAB_GROUPS * SSM_STATE
VMEM_LIMIT = 56 * 1024 * 1024


def _params(n_axes, flags=None):
    return pltpu.CompilerParams(dimension_semantics=("arbitrary",) * n_axes,
                                vmem_limit_bytes=VMEM_LIMIT, flags=flags)


def _tile(n, target, mult=8):
    t = min(n, target)
    while t > 1 and (n % t or t % mult):
        t -= 1
    return t if n % t == 0 else n


def _const_spec(shape):
    nd = len(shape)
    return pl.BlockSpec(shape, lambda *_: (0,) * nd, pipeline_mode=pl.Buffered(1))


def _sigmoid(x):
    return 1.0 / (1.0 + jnp.exp(-x))


def _gelu_tanh(x):
    c = math.sqrt(2.0 / math.pi)
    return x * (0.5 * (1.0 + jnp.tanh(c * (x + 0.044715 * (x * x * x)))))


def _norm_mod(x, g, shift, scale):
    ms = jnp.mean(x * x, axis=-1, keepdims=True)
    y = x * lax.rsqrt(ms + EPS) * g
    return y * (1.0 + scale) + shift


def _adaln_kernel(c_ref, w_ref, b_ref, o_ref):
    c = c_ref[...]
    s = (c * _sigmoid(c)).astype(BF16)
    o_ref[0] = jnp.dot(s, w_ref[0].astype(BF16), preferred_element_type=F32) + b_ref[0]


def _adaln(cond, ada_w, ada_b):
    depth, d, n = ada_w.shape
    r = cond.shape[0]
    tn = _tile(n, 1536, LANES)
    return pl.pallas_call(
        _adaln_kernel,
        out_shape=jax.ShapeDtypeStruct((depth, r, n), F32),
        grid=(depth, n // tn),
        in_specs=[pl.BlockSpec((r, d), lambda l, j: (0, 0)),
                  pl.BlockSpec((1, d, tn), lambda l, j: (l, 0, j)),
                  pl.BlockSpec((1, 1, tn), lambda l, j: (l, 0, j))],
        out_specs=pl.BlockSpec((1, r, tn), lambda l, j: (l, 0, j)),
        compiler_params=_params(2), name="adaln",
    )(cond, ada_w, ada_b.reshape(depth, 1, n))


class _Rows:
    def __init__(self, nb, rpb, tm, mod_row):
        self.nb, self.rpb, self.tm = nb, rpb, tm
        self.nt = rpb // tm
        self.grid = (nb, self.nt)
        self.mod_row = mod_row

    def rows(self, cols):
        nt = self.nt
        return pl.BlockSpec((self.tm, cols), lambda b, i: (b * nt + i, 0))

    def rows3(self, per, cols):
        nt, tm = self.nt, self.tm
        if tm <= per:
            k = per // tm
            return pl.BlockSpec((None, tm, cols), lambda b, i: ((b * nt + i) // k, (b * nt + i) % k, 0))
        return pl.BlockSpec((tm // per, per, cols), lambda b, i: (b * nt + i, 0, 0))

    def mod(self, k):
        if self.mod_row is None:
            return pl.BlockSpec((None, 1, D_MODEL), lambda b, i: (b, 0, k))
        r = self.mod_row
        return pl.BlockSpec((None, 1, D_MODEL), lambda b, i: (r, 0, k))


def _even_in_kernel(x_ref, g_ref, sh_ref, sc_ref, w_ref, cs_ref, vc_ref, vs_ref, zs_ref):
    h = _norm_mod(x_ref[...], g_ref[...], sh_ref[...], sc_ref[...]).astype(BF16)
    z = jnp.dot(h, w_ref[...], preferred_element_type=F32)
    zf = z[:, :D_FOURIER].astype(BF16)
    for gi in range(FOURIER_GROUPS):
        lo, hi = gi * FOURIER_CH, (gi + 1) * FOURIER_CH
        v = jnp.dot(zf[:, lo:hi], cs_ref[...], preferred_element_type=F32)
        vc_ref[:, lo:hi] = v[:, :FOURIER_CH].astype(BF16)
        vs_ref[:, lo:hi] = v[:, FOURIER_CH:].astype(BF16)
    zs_ref[...] = z[:, D_FOURIER:].astype(BF16)


def _even_in(x, rows, norm_g, mod, w_in, cs):
    n = x.shape[0]
    outs = tuple(jax.ShapeDtypeStruct((n, D_FOURIER), BF16) for _ in range(3))
    return pl.pallas_call(
        _even_in_kernel, out_shape=outs, grid=rows.grid,
        in_specs=[rows.rows(D_MODEL), _const_spec((1, D_MODEL)), rows.mod(0), rows.mod(1),
                  _const_spec((D_MODEL, D_MODEL)), _const_spec((FOURIER_CH, 2 * FOURIER_CH))],
        out_specs=tuple(rows.rows(D_FOURIER) for _ in range(3)),
        compiler_params=_params(2), name="even_in",
    )(x, norm_g, mod, mod, w_in, cs)


def _fourier_kernel(cl_ref, sl_ref, vc_ref, vs_ref, o_ref, *, scale):
    a = jnp.dot(cl_ref[...], vc_ref[0], preferred_element_type=F32)
    a = a - jnp.dot(sl_ref[...], vs_ref[0], preferred_element_type=F32)
    o_ref[0] = (a * scale).astype(BF16)


def _fourier(vc, vs, cl, sl, nb, length):
    vc3 = vc.reshape(nb, length, D_FOURIER)
    vs3 = vs.reshape(nb, length, D_FOURIER)
    scale = 1.0 / math.sqrt(length * FOURIER_CH)
    blk = pl.BlockSpec((1, length, D_FOURIER), lambda b: (b, 0, 0))
    out = pl.pallas_call(
        functools.partial(_fourier_kernel, scale=scale),
        out_shape=jax.ShapeDtypeStruct((nb, length, D_FOURIER), BF16),
        grid=(nb,),
        in_specs=[_const_spec((length, length)), _const_spec((length, length)), blk, blk],
        out_specs=blk,
        compiler_params=_params(1), name="fourier",
    )(cl, sl, vc3, vs3)
    return out.reshape(nb * length, D_FOURIER)


def _dft_mats(n):
    k = lax.iota(jnp.int32, n)
    ang = ((k[:, None] * k[None, :]) % n).astype(F32) * (2.0 * math.pi / n)
    return jnp.cos(ang), jnp.sin(ang)


def _ssm_ops_kernel(ar_ref, ai_ref, ldt_ref, btr_ref, bti_ref, ctr_ref, cti_ref,
                    bm_ref, wh_ref, atr_ref, ati_ref, e_ref):
    t = SSM_CHUNK
    n = SLAB_STATE
    rows = lax.broadcasted_iota(jnp.int32, (LANES, n), 0)
    cols = lax.broadcasted_iota(jnp.int32, (LANES, n), 1)
    same_group = (rows // SSM_GROUP) == (cols // SSM_STATE)
    lag = lax.broadcasted_iota(jnp.int32, (t + 8, n), 0).astype(F32)
    nt_dims = (((1,), (1,)), ((), ()))
    for d in range(2):
        ar, ai = ar_ref[d], ai_ref[d]
        dt = jnp.exp(ldt_ref[d])
        xr, xi = ar * dt, ai * dt
        mag = jnp.exp(lag * xr)
        pr = mag * jnp.cos(lag * xi)
        pi = mag * jnp.sin(lag * xi)
        abr, abi = pr[1:2], pi[1:2]
        den = ar * ar + ai * ai
        fr = ((abr - 1.0) * ar + abi * ai) / den
        fi = (abi * ar - (abr - 1.0) * ai) / den
        br = jnp.where(same_group, btr_ref[d], 0.0)
        bi = jnp.where(same_group, bti_ref[d], 0.0)
        bbr = fr * br - fi * bi
        bbi = fr * bi + fi * br
        cr = jnp.where(same_group, ctr_ref[d], 0.0)
        ci = jnp.where(same_group, cti_ref[d], 0.0)
        for k in range(t):
            pkr, pki = pr[k:k + 1], pi[k:k + 1]
            x_r = bbr * pkr - bbi * pki
            x_i = bbr * pki + bbi * pkr
            s = t - 1 - k if d == 0 else k
            bm_ref[s * LANES:(s + 1) * LANES, 2 * d * n:(2 * d + 1) * n] = x_r.astype(BF16)
            bm_ref[s * LANES:(s + 1) * LANES, (2 * d + 1) * n:(2 * d + 2) * n] = x_i.astype(BF16)
            e_ref[d, k] = (lax.dot_general(x_r, cr, nt_dims, precision=lax.Precision.HIGHEST,
                                           preferred_element_type=F32)
                           - lax.dot_general(x_i, ci, nt_dims, precision=lax.Precision.HIGHEST,
                                             preferred_element_type=F32))
        for k in range(1, t + 1):
            pkr, pki = pr[k:k + 1], pi[k:k + 1]
            w_r = cr * pkr - ci * pki
            w_i = cr * pki + ci * pkr
            tt = k - 1 if d == 0 else t - k
            wh_ref[2 * d * n:(2 * d + 1) * n, tt * LANES:(tt + 1) * LANES] = w_r.T.astype(BF16)
            wh_ref[(2 * d + 1) * n:(2 * d + 2) * n, tt * LANES:(tt + 1) * LANES] = (-w_i).T.astype(BF16)
        atr_ref[d] = pr[t:t + 1]
        ati_ref[d] = pi[t:t + 1]


def _ssm_toeplitz_kernel(e_ref, dsk_ref, m_ref):
    t = SSM_CHUNK
    s = pl.program_id(1)
    r = lax.broadcasted_iota(jnp.int32, (LANES, LANES), 0)
    c = lax.broadcasted_iota(jnp.int32, (LANES, LANES), 1)
    skip = jnp.where(r == c, jnp.broadcast_to(dsk_ref[...], (LANES, LANES)), 0.0)
    for tt in range(t):
        blk = ((tt >= s).astype(F32) * e_ref[0, jnp.maximum(tt - s, 0)]
               + (s >= tt).astype(F32) * e_ref[1, jnp.maximum(s - tt, 0)]
               + (s == tt).astype(F32) * skip)
        m_ref[:, tt * LANES:(tt + 1) * LANES] = blk.astype(BF16)


def _ssm_ops(a_re, a_im, log_dt, b_re, b_im, c_re, c_im, d_skip):
    t = SSM_CHUNK
    n = SLAB_STATE

    def per_state(x):
        return x.astype(F32).reshape(2, SLABS, 1, n)

    def per_channel(x):
        x = x.astype(F32).reshape(2, SLABS, LANES, SSM_STATE)
        return jnp.tile(x, (1, 1, 1, SLAB_GROUPS))

    ldt = jnp.broadcast_to(log_dt[..., None], a_re.shape)
    vec = pl.BlockSpec((2, None, 1, n), lambda z: (0, z, 0, 0))
    mat = pl.BlockSpec((2, None, LANES, n), lambda z: (0, z, 0, 0))
    bm, wh, at_re, at_im, e = pl.pallas_call(
        _ssm_ops_kernel,
        out_shape=(jax.ShapeDtypeStruct((SLABS, t * LANES, 4 * n), BF16),
                   jax.ShapeDtypeStruct((SLABS, 4 * n, t * LANES), BF16),
                   jax.ShapeDtypeStruct((SLABS, 2, 1, n), F32),
                   jax.ShapeDtypeStruct((SLABS, 2, 1, n), F32),
                   jax.ShapeDtypeStruct((SLABS, 2, t, LANES, LANES), F32)),
        grid=(SLABS,),
        in_specs=[vec, vec, vec, mat, mat, mat, mat],
        out_specs=(pl.BlockSpec((None, t * LANES, 4 * n), lambda z: (z, 0, 0)),
                   pl.BlockSpec((None, 4 * n, t * LANES), lambda z: (z, 0, 0)),
                   pl.BlockSpec((None, 2, 1, n), lambda z: (z, 0, 0, 0)),
                   pl.BlockSpec((None, 2, 1, n), lambda z: (z, 0, 0, 0)),
                   pl.BlockSpec((None, 2, t, LANES, LANES), lambda z: (z, 0, 0, 0, 0))),
        compiler_params=_params(1), name="ssm_ops",
    )(per_state(a_re), per_state(a_im), per_state(ldt),
      per_channel(b_re.transpose(0, 1, 3, 2)), per_channel(b_im.transpose(0, 1, 3, 2)),
      per_channel(c_re), per_channel(c_im))
    m = pl.pallas_call(
        _ssm_toeplitz_kernel,
        out_shape=jax.ShapeDtypeStruct((SLABS, t * LANES, t * LANES), BF16),
        grid=(SLABS, t),
        in_specs=[pl.BlockSpec((None, 2, t, LANES, LANES), lambda z, s: (z, 0, 0, 0, 0)),
                  pl.BlockSpec((None, 1, LANES), lambda z, s: (z, 0, 0))],
        out_specs=pl.BlockSpec((None, LANES, t * LANES), lambda z, s: (z, s, 0)),
        compiler_params=_params(2), name="ssm_toeplitz",
    )(e, d_skip.astype(F32).reshape(SLABS, 1, LANES))
    return m, bm, wh, at_re.reshape(SLABS * 2, 1, n), at_im.reshape(SLABS * 2, 1, n)


def _ssm_contrib_kernel(u_ref, bm_ref, o_ref):
    o_ref[...] = jnp.dot(u_ref[...], bm_ref[...], preferred_element_type=F32)


def _ssm_contrib(u, bm):
    slabs, r, k = u.shape
    n = bm.shape[2]
    tr = _tile(r, 576)
    return pl.pallas_call(
        _ssm_contrib_kernel,
        out_shape=jax.ShapeDtypeStruct((slabs, r, n), F32),
        grid=(slabs, r // tr),
        in_specs=[pl.BlockSpec((None, tr, k), lambda s, i: (s, i, 0)),
                  pl.BlockSpec((None, k, n), lambda s, i: (s, 0, 0))],
        out_specs=pl.BlockSpec((None, tr, n), lambda s, i: (s, i, 0)),
        compiler_params=_params(2), name="ssm_contrib",
    )(u, bm)


def _ssm_scan_kernel(ar_ref, ai_ref, cr_ref, ci_ref, hr_ref, hi_ref, *, ncc, nct, rb):
    d = pl.program_id(1)
    lt = ar_ref.shape[-1]
    ar = jnp.broadcast_to(ar_ref[...], (rb, lt))
    ai = jnp.broadcast_to(ai_ref[...], (rb, lt))
    zero = jnp.zeros((rb, lt), F32)

    def advance(i_src, i_dst, hr, hi):
        r0 = pl.multiple_of(i_src * rb, rb)
        r1 = pl.multiple_of(i_dst * rb, rb)
        nr = ar * hr - ai * hi + cr_ref[pl.ds(r0, rb), :]
        ni = ar * hi + ai * hr + ci_ref[pl.ds(r0, rb), :]
        hr_ref[pl.ds(r1, rb), :] = nr
        hi_ref[pl.ds(r1, rb), :] = ni
        return nr, ni

    @pl.when(d == 0)
    def _():
        hr_ref[pl.ds(0, rb), :] = zero
        hi_ref[pl.ds(0, rb), :] = zero
        lax.fori_loop(0, nct - 1, lambda i, c: advance(i, i + 1, *c), (zero, zero))

    @pl.when(d == 1)
    def _():
        last_c = (ncc - 1) * rb
        hr_ref[pl.ds(last_c, rb), :] = zero
        hi_ref[pl.ds(last_c, rb), :] = zero
        c = lax.fori_loop(0, ncc - 1, lambda k, c: advance(ncc - 1 - k, ncc - 2 - k, *c), (zero, zero))
        c = advance(0, nct - 1, *c)
        lax.fori_loop(0, nct - 1 - ncc, lambda k, c: advance(nct - 1 - k, nct - 2 - k, *c), c)


def _ssm_scan(contrib, at_re, at_im, ncc, nct, rb):
    slabs, r, n = contrib.shape
    lt = LANES
    q = SLAB_STATE // lt
    re_spec = pl.BlockSpec((None, r, lt), lambda s, d, j: (s, 0, d * 2 * q + j))
    im_spec = pl.BlockSpec((None, r, lt), lambda s, d, j: (s, 0, d * 2 * q + q + j))
    a_spec = pl.BlockSpec((None, 1, lt), lambda s, d, j: (s * 2 + d, 0, j))
    h_spec = pl.BlockSpec((None, r, lt), lambda s, d, j: (s, 0, d * q + j))
    out = jax.ShapeDtypeStruct((slabs, r, n // 2), F32)
    return pl.pallas_call(
        functools.partial(_ssm_scan_kernel, ncc=ncc, nct=nct, rb=rb),
        out_shape=(out, out),
        grid=(slabs, 2, q),
        in_specs=[a_spec, a_spec, re_spec, im_spec],
        out_specs=(h_spec, h_spec),
        compiler_params=_params(3), name="ssm_scan",
    )(at_re, at_im, contrib, contrib)


def _ssm_read_kernel(u_ref, hr_ref, hi_ref, m_ref, wh_ref, o_ref):
    q = SLAB_STATE
    acc = jnp.dot(u_ref[...], m_ref[...], preferred_element_type=F32)
    for d in range(2):
        acc += jnp.dot(hr_ref[:, d * q:(d + 1) * q].astype(BF16), wh_ref[2 * d * q:(2 * d + 1) * q, :],
                       preferred_element_type=F32)
        acc += jnp.dot(hi_ref[:, d * q:(d + 1) * q].astype(BF16), wh_ref[(2 * d + 1) * q:(2 * d + 2) * q, :],
                       preferred_element_type=F32)
    o_ref[...] = acc.astype(BF16)


def _ssm_read(u, h_re, h_im, m, wh):
    slabs, r, k = u.shape
    ns = h_re.shape[2]
    nw = wh.shape[1]
    tr = _tile(r, 576)
    tn = k // 2
    return pl.pallas_call(
        _ssm_read_kernel,
        out_shape=jax.ShapeDtypeStruct((slabs, r, k), BF16),
        grid=(slabs, 2, r // tr),
        in_specs=[pl.BlockSpec((None, tr, k), lambda s, j, i: (s, i, 0)),
                  pl.BlockSpec((None, tr, ns), lambda s, j, i: (s, i, 0)),
                  pl.BlockSpec((None, tr, ns), lambda s, j, i: (s, i, 0)),
                  pl.BlockSpec((None, k, tn), lambda s, j, i: (s, 0, j)),
                  pl.BlockSpec((None, nw, tn), lambda s, j, i: (s, 0, j))],
        out_specs=pl.BlockSpec((None, tr, tn), lambda s, j, i: (s, i, j)),
        compiler_params=_params(3), name="ssm_read",
    )(u, h_re, h_im, m, wh)


def _to_slabs(zs, nb, length):
    nc = length // SSM_CHUNK
    u = zs.reshape(nb, nc, SSM_CHUNK, SLABS, LANES).transpose(3, 1, 0, 2, 4)
    return u.reshape(SLABS, nc * nb, SSM_CHUNK * LANES)


def _from_slabs(y, nb, length):
    nc = length // SSM_CHUNK
    y = y.reshape(SLABS, nc, nb, SSM_CHUNK, LANES).transpose(2, 1, 3, 0, 4)
    return y.reshape(nb * length, D_SSM)


MLP_FF_CHUNK = 1024


def _mlp_tail(x, g_ref, sh_ref, sc_ref, gate_ref, w1_ref, w2_ref, o_ref):
    fc = MLP_FF_CHUNK
    h = _norm_mod(x, g_ref[...], sh_ref[...], sc_ref[...]).astype(BF16)
    acc = jnp.zeros(x.shape, F32)
    for k in range(D_FF // fc):
        a = jnp.dot(h, w1_ref[:, k * fc:(k + 1) * fc], preferred_element_type=F32)
        a = jnp.square(jnp.maximum(a, 0.0)).astype(BF16)
        acc += jnp.dot(a, w2_ref[k * fc:(k + 1) * fc, :], preferred_element_type=F32)
    o_ref[...] = x + gate_ref[...] * acc


def _even_tail_kernel(ya_ref, ys_ref, x_ref, gate1_ref, wg_ref, bg_ref, wa_ref, wb_ref,
                      g_ref, sh_ref, sc_ref, gate2_ref, w1_ref, w2_ref, o_ref):
    y = _gelu_tanh(ys_ref[...].astype(F32))
    glu = y * _sigmoid(jnp.dot(y.astype(BF16), wg_ref[...], preferred_element_type=F32) + bg_ref[...])
    acc = jnp.dot(ya_ref[...], wa_ref[...], preferred_element_type=F32)
    acc += jnp.dot(glu.astype(BF16), wb_ref[...], preferred_element_type=F32)
    x1 = x_ref[...] + gate1_ref[...] * acc
    _mlp_tail(x1, g_ref, sh_ref, sc_ref, gate2_ref, w1_ref, w2_ref, o_ref)


def _odd_tail_kernel(a_ref, x_ref, gate1_ref, wo_ref, g_ref, sh_ref, sc_ref, gate2_ref, w1_ref, w2_ref, o_ref):
    a = a_ref[...].reshape(x_ref.shape)
    x1 = x_ref[...] + gate1_ref[...] * jnp.dot(a, wo_ref[...], preferred_element_type=F32)
    _mlp_tail(x1, g_ref, sh_ref, sc_ref, gate2_ref, w1_ref, w2_ref, o_ref)


def _mlp_specs(rows):
    return [_const_spec((1, D_MODEL)), rows.mod(3), rows.mod(4), rows.mod(5),
            _const_spec((D_MODEL, D_FF)), _const_spec((D_FF, D_MODEL))]


def _even_tail(ya, ys, x, rows, mod, w_glu, b_glu, w_out, norm_g, w1, w2):
    return pl.pallas_call(
        _even_tail_kernel, out_shape=jax.ShapeDtypeStruct(x.shape, F32), grid=rows.grid,
        in_specs=[rows.rows(D_FOURIER), rows.rows(D_SSM), rows.rows(D_MODEL), rows.mod(2),
                  _const_spec((D_SSM, D_SSM)), _const_spec((1, D_SSM)),
                  _const_spec((D_FOURIER, D_MODEL)), _const_spec((D_SSM, D_MODEL))] + _mlp_specs(rows),
        out_specs=rows.rows(D_MODEL),
        compiler_params=_params(2), name="even_tail",
    )(ya, ys, x, mod, w_glu, b_glu, w_out[:D_FOURIER], w_out[D_FOURIER:], norm_g, mod, mod, mod, w1, w2)


def _odd_tail(a, x, rows, mod, w_out, norm_g, w1, w2):
    return pl.pallas_call(
        _odd_tail_kernel, out_shape=jax.ShapeDtypeStruct(x.shape, F32), grid=rows.grid,
        in_specs=[rows.rows3(a.shape[1], D_MODEL), rows.rows(D_MODEL), rows.mod(2),
                  _const_spec((D_MODEL, D_MODEL))] + _mlp_specs(rows),
        out_specs=rows.rows(D_MODEL),
        compiler_params=_params(2), name="odd_tail",
    )(a, x, mod, w_out, norm_g, mod, mod, mod, w1, w2)


KEY_CHUNK = 256
V_ROWS = DA_V_DIM + 16
Q_SCALE = DA_HEAD_DIM ** -0.5 * math.log2(math.e)


def _qkv_kernel(*refs, rope):
    if rope:
        (x_ref, g_ref, sh_ref, sc_ref, w_ref, gqk_ref, ones_ref,
         cos_ref, sa_ref, sb_ref, qt_ref, k_ref, vt_ref) = refs
    else:
        x_ref, g_ref, sh_ref, sc_ref, w_ref, gqk_ref, ones_ref, qt_ref, k_ref, vt_ref = refs
    h = _norm_mod(x_ref[...], g_ref[...], sh_ref[...], sc_ref[...]).astype(BF16)
    y = jnp.dot(h, w_ref[...], preferred_element_type=F32)
    tm = y.shape[0]
    for s in range(2 * DA_HEADS):
        blk = y[:, s * LANES:(s + 1) * LANES]
        ssq = jnp.dot((blk * blk).astype(BF16), ones_ref[...], preferred_element_type=F32)
        nrm = blk * gqk_ref[s // DA_HEADS]
        if rope:
            nrm = (nrm * cos_ref[...] + pltpu.roll(nrm, LANES - 16, 1) * sa_ref[...]
                   + pltpu.roll(nrm, 16, 1) * sb_ref[...])
        nrm = nrm * lax.rsqrt(ssq * (1.0 / DA_HEAD_DIM) + EPS)
        if s < DA_HEADS:
            qt_ref[s] = nrm.T.astype(BF16)
        else:
            hd = s - DA_HEADS
            k_ref[:, hd * LANES:(hd + 1) * LANES] = nrm.astype(BF16)
    ones = jnp.ones((V_ROWS - DA_V_DIM, tm), BF16)
    for hd in range(DA_HEADS):
        v = y[:, (2 * DA_HEADS + hd) * LANES:(2 * DA_HEADS + hd + 1) * LANES]
        vt_ref[hd, :DA_V_DIM, :] = v.T.astype(BF16)
        vt_ref[hd, DA_V_DIM:, :] = ones


def _qkv(x, rows, norm_g, mod, w_in, gqk, ones_bd, rope_tabs):
    nb, rpb, tm = rows.nb, rows.rpb, rows.tm
    assert tm % KEY_CHUNK == 0
    rope = rope_tabs is not None
    in_specs = [rows.rows(D_MODEL), _const_spec((1, D_MODEL)), rows.mod(0), rows.mod(1),
                _const_spec((D_MODEL, 3 * D_MODEL)), _const_spec((2, 1, LANES)),
                _const_spec((LANES, LANES))]
    args = [x, norm_g, mod, mod, w_in, gqk, ones_bd]
    if rope:
        tab = pl.BlockSpec((tm, LANES), lambda b, i: (i, 0))
        in_specs += [tab, tab, tab]
        args += list(rope_tabs)
    out_shape = (jax.ShapeDtypeStruct((nb, DA_HEADS, LANES, rpb), BF16),
                 jax.ShapeDtypeStruct((nb, rpb, D_MODEL), BF16),
                 jax.ShapeDtypeStruct((nb, DA_HEADS, V_ROWS, rpb), BF16))
    out_specs = (pl.BlockSpec((None, DA_HEADS, LANES, tm), lambda b, i: (b, 0, 0, i)),
                 pl.BlockSpec((None, tm, D_MODEL), lambda b, i: (b, i, 0)),
                 pl.BlockSpec((None, DA_HEADS, V_ROWS, tm), lambda b, i: (b, 0, 0, i)))
    return pl.pallas_call(
        functools.partial(_qkv_kernel, rope=rope),
        out_shape=out_shape, grid=rows.grid, in_specs=in_specs, out_specs=out_specs,
        compiler_params=_params(2), name="qkv",
    )(*args)


def _rope_tables(length):
    pos = jnp.arange(length, dtype=jnp.int32)
    row = (pos // GRID_W).astype(F32)
    col = (pos % GRID_W).astype(F32)
    half = DA_HEAD_DIM // 2
    inv_freq = ROPE_THETA ** (-jnp.arange(0, half, 2, dtype=F32) / half)
    lane = jnp.arange(LANES)
    f = lane % 16
    is_col = (lane % DA_HEAD_DIM) >= half
    second = (lane % half) >= 16
    ang = jnp.where(is_col[None, :], col[:, None], row[:, None]) * inv_freq[f][None, :]
    cos, sin = jnp.cos(ang), jnp.sin(ang)
    sin_a = jnp.where(second[None, :], 0.0, -sin)
    sin_b = jnp.where(second[None, :], sin, 0.0)
    return cos, sin_a, sin_b


def _attn_kernel(*refs, n_kv, lam_init, tq):
    lp_ref, gh_ref, qt_ref = refs[:3]
    kv_refs = refs[3:3 + 2 * n_kv]
    o_ref = refs[3 + 2 * n_kv]
    bufs = refs[4 + 2 * n_kv:]
    lp = lp_ref[...]
    lam = (jnp.exp(jnp.sum(lp[0:1] * lp[1:2], axis=-1, keepdims=True))
           - jnp.exp(jnp.sum(lp[2:3] * lp[3:4], axis=-1, keepdims=True)) + lam_init)
    nq = qt_ref.shape[1] // tq
    kc = KEY_CHUNK

    key_chunks = []
    for i in range(n_kv):
        for c in range(kv_refs[2 * i].shape[0] // kc):
            key_chunks.append((kv_refs[2 * i], kv_refs[2 * i + 1], c))

    def masked_q(j):
        qt = qt_ref[:, j * tq:(j + 1) * tq]
        row = lax.broadcasted_iota(jnp.int32, qt.shape, 0)
        zero = jnp.zeros_like(qt)
        return jnp.concatenate([jnp.where(row < DA_HEAD_DIM, qt, zero),
                                jnp.where(row >= DA_HEAD_DIM, qt, zero)], axis=1)

    def score_chunk(n, qst, s_scr, m):
        k_ref, _, c = key_chunks[n]
        s = jnp.dot(k_ref[c * kc:(c + 1) * kc, :], qst, preferred_element_type=F32)
        s_scr[n * kc:(n + 1) * kc, :] = s
        return jnp.maximum(m, s.reshape(kc // 8, 8, 2 * tq).max(axis=0))

    def value_chunk(n, s_scr, m, acc):
        _, vt_ref, c = key_chunks[n]
        p = jnp.exp2((s_scr[n * kc:(n + 1) * kc, :] - m).astype(BF16))
        return acc + jnp.dot(vt_ref[:, c * kc:(c + 1) * kc], p, preferred_element_type=F32)

    def finish(j, acc):
        o2 = acc[:DA_V_DIM] / acc[DA_V_DIM:DA_V_DIM + 1]
        o = o2[:, :tq] - lam * o2[:, tq:]
        o = o * lax.rsqrt(jnp.mean(o * o, axis=0, keepdims=True) + EPS) * gh_ref[...]
        o_ref[j * tq:(j + 1) * tq, :] = (o * (1.0 - lam_init)).T.astype(BF16)

    neg = jnp.full((8, 2 * tq), -jnp.inf, F32)
    qst = masked_q(0)
    m8 = neg
    for n in range(len(key_chunks)):
        m8 = score_chunk(n, qst, bufs[0], m8)
    for j in range(nq):
        m = m8.max(axis=0, keepdims=True)
        acc = jnp.zeros((V_ROWS, 2 * tq), F32)
        if j + 1 < nq:
            qst = masked_q(j + 1)
            m8 = neg
        for n in range(len(key_chunks)):
            if j + 1 < nq:
                m8 = score_chunk(n, qst, bufs[(j + 1) % 2], m8)
            acc = value_chunk(n, bufs[j % 2], m, acc)
        finish(j, acc)


def _attention(qt, kv_list, lam_p, g_head, lam_init, tq):
    nb, _, _, lq = qt.shape
    in_specs = [_const_spec((4, DA_HEAD_DIM)), _const_spec((DA_V_DIM, 1)),
                pl.BlockSpec((None, None, LANES, lq), lambda b, h: (b, h, 0, 0))]
    args = [lam_p, g_head, qt]
    total = 0
    for k, vt in kv_list:
        lk = k.shape[1]
        total += lk
        in_specs.append(pl.BlockSpec((None, lk, LANES), lambda b, h: (b, 0, h)))
        in_specs.append(pl.BlockSpec((None, None, V_ROWS, lk), lambda b, h: (b, h, 0, 0)))
        args += [k, vt]
    n_bufs = min(2, lq // tq)
    return pl.pallas_call(
        functools.partial(_attn_kernel, n_kv=len(kv_list), lam_init=lam_init, tq=tq),
        out_shape=jax.ShapeDtypeStruct((nb, lq, D_MODEL), BF16),
        grid=(nb, DA_HEADS),
        in_specs=in_specs,
        out_specs=pl.BlockSpec((None, lq, LANES), lambda b, h: (b, 0, h)),
        scratch_shapes=[pltpu.VMEM((total, 2 * tq), F32)] * n_bufs,
        compiler_params=_params(2), name="diff_attn",
    )(*args)


def kernel(x, c, ctx, c_ctx, norm1_g, norm2_g, ada_w, ada_b, mlp_w1, mlp_w2, ev_w_in, ev_w_out, ssm_a_re, ssm_a_im, ssm_log_dt, ssm_b_re, ssm_b_im, ssm_c_re, ssm_c_im, ssm_d, ssm_w_glu, ssm_b_glu, od_w_in, od_w_out, od_q_norm, od_k_norm, od_lambda, od_head_norm):
    nb, length, d = x.shape
    clen = ctx.shape[1]
    assert d == D_MODEL and length % GRID_W == 0
    assert length % SSM_CHUNK == 0 and clen % SSM_CHUNK == 0

    xs = x.reshape(nb * length, d)
    cs = ctx.reshape(nb * clen, d)
    rows_x = _Rows(nb, length, _tile(length, 512), None)
    rows_c = _Rows(1, nb * clen, _tile(nb * clen, 512), nb)
    rows_cb = _Rows(nb, clen, clen, nb)

    n_cond = nb + 1
    pad = (-n_cond) % 8
    cond = jnp.concatenate([c, c_ctx[None, :], jnp.zeros((pad, d), F32)], axis=0)
    mods = _adaln(cond, ada_w, ada_b)[:, :n_cond].reshape(DEPTH, n_cond, 1, 6 * d)

    kf = lax.iota(jnp.int32, FOURIER_CH)
    ang = ((kf[:, None] * kf[None, :]) % FOURIER_CH).astype(F32) * (2.0 * math.pi / FOURIER_CH)
    cs_ch = jnp.concatenate([jnp.cos(ang), jnp.sin(ang)], axis=1).astype(BF16)
    cl_x, sl_x = (m.astype(BF16) for m in _dft_mats(length))
    cl_c, sl_c = (m.astype(BF16) for m in _dft_mats(clen))
    rope_tabs = _rope_tables(length)
    lane = jnp.arange(LANES)
    ones_bd = (lane[:, None] // DA_HEAD_DIM == lane[None, :] // DA_HEAD_DIM).astype(BF16)

    ncc = clen // SSM_CHUNK
    nct = ncc + length // SSM_CHUNK

    for i in range(DEPTH):
        last = i == DEPTH - 1
        j = i // 2
        mod = mods[i]
        n1 = norm1_g[i].reshape(1, d)
        n2 = norm2_g[i].reshape(1, d)
        if i % 2 == 0:
            w_in = ev_w_in[j].astype(BF16)
            vc_x, vs_x, zs_x = _even_in(xs, rows_x, n1, mod, w_in, cs_ch)
            vc_c, vs_c, zs_c = _even_in(cs, rows_c, n1, mod, w_in, cs_ch)
            ya_x = _fourier(vc_x, vs_x, cl_x, sl_x, nb, length)
            m, bm, wh, at_re, at_im = _ssm_ops(ssm_a_re[j], ssm_a_im[j], ssm_log_dt[j], ssm_b_re[j],
                                                ssm_b_im[j], ssm_c_re[j], ssm_c_im[j], ssm_d[j])
            u = jnp.concatenate([_to_slabs(zs_c, nb, clen), _to_slabs(zs_x, nb, length)], axis=1)
            contrib = _ssm_contrib(u, bm)
            h_re, h_im = _ssm_scan(contrib, at_re, at_im, ncc, nct, nb)
            y = _ssm_read(u, h_re, h_im, m, wh)
            ys_x = _from_slabs(y[:, ncc * nb:], nb, length)
            w_glu = ssm_w_glu[j].astype(BF16)
            b_glu = ssm_b_glu[j].reshape(1, D_SSM)
            w_out = ev_w_out[j].astype(BF16)
            w1 = mlp_w1[i].astype(BF16)
            w2 = mlp_w2[i].astype(BF16)
            xs = _even_tail(ya_x, ys_x, xs, rows_x, mod, w_glu, b_glu, w_out, n2, w1, w2)
            if not last:
                ya_c = _fourier(vc_c, vs_c, cl_c, sl_c, nb, clen)
                ys_c = _from_slabs(y[:, :ncc * nb], nb, clen)
                cs = _even_tail(ya_c, ys_c, cs, rows_c, mod, w_glu, b_glu, w_out, n2, w1, w2)
        else:
            lam_init = 0.8 - 0.6 * math.exp(-0.3 * i)
            w_in = od_w_in[j].astype(BF16)
            gqk = jnp.stack([jnp.tile(od_q_norm[j], 2) * Q_SCALE,
                             jnp.tile(od_k_norm[j], 2)]).reshape(2, 1, LANES)
            qt_x, k_x, vt_x = _qkv(xs, rows_x, n1, mod, w_in, gqk, ones_bd, rope_tabs)
            qt_c, k_c, vt_c = _qkv(cs, rows_cb, n1, mod, w_in, gqk, ones_bd, None)
            g_head = od_head_norm[j].reshape(DA_V_DIM, 1)
            w_out = od_w_out[j].astype(BF16)
            o_x = _attention(qt_x, [(k_x, vt_x), (k_c, vt_c)], od_lambda[j], g_head, lam_init,
                             _tile(length, 512, LANES))
            w1 = mlp_w1[i].astype(BF16)
            w2 = mlp_w2[i].astype(BF16)
            xs = _odd_tail(o_x, xs, rows_x, mod, w_out, n2, w1, w2)
            if not last:
                o_c = _attention(qt_c, [(k_c, vt_c)], od_lambda[j], g_head, lam_init, clen)
                cs = _odd_tail(o_c, cs, rows_c, mod, w_out, n2, w1, w2)
    return xs.reshape(nb, length, d)
```

```python
import functools
import math

import jax
import jax.numpy as jnp
from jax import lax
from jax.experimental import pallas as pl
from jax.experimental.pallas import tpu as pltpu

F32 = jnp.float32
BF16 = jnp.bfloat16

D_MODEL = 1024
DEPTH = 4
GRID_W = 64
EPS = 1e-6
D_FF = 4 * D_MODEL
D_FOURIER = D_MODEL // 2
FOURIER_GROUPS = 4
FOURIER_CH = D_FOURIER // FOURIER_GROUPS
D_SSM = D_MODEL - D_FOURIER
SSM_GROUP = 16
SSM_GROUPS = D_SSM // SSM_GROUP
SSM_STATE = 64
DA_HEAD_DIM = 64
DA_V_DIM = 2 * DA_HEAD_DIM
DA_HEADS = D_MODEL // DA_V_DIM
ROPE_THETA = 10000.0

LANES = 128
SSM_CHUNK = 16
SLABS = D_SSM // LANES
SLAB_GROUPS = LANES // SSM_GROUP
SLAB_STATE = SLAB_GROUPS * SSM_STATE
VMEM_LIMIT = 56 * 1024 * 1024


def _params(n_axes, flags=None):
    return pltpu.CompilerParams(dimension_semantics=("arbitrary",) * n_axes,
                                vmem_limit_bytes=VMEM_LIMIT, flags=flags)


def _tile(n, target, mult=8):
    t = min(n, target)
    while t > 1 and (n % t or t % mult):
        t -= 1
    return t if n % t == 0 else n


def _const_spec(shape):
    nd = len(shape)
    return pl.BlockSpec(shape, lambda *_: (0,) * nd, pipeline_mode=pl.Buffered(1))


def _sigmoid(x):
    return 1.0 / (1.0 + jnp.exp(-x))


def _gelu_tanh(x):
    c = math.sqrt(2.0 / math.pi)
    return x * (0.5 * (1.0 + jnp.tanh(c * (x + 0.044715 * (x * x * x)))))


def _norm_mod(x, g, shift, scale):
    ms = jnp.mean(x * x, axis=-1, keepdims=True)
    y = x * lax.rsqrt(ms + EPS) * g
    return y * (1.0 + scale) + shift


def _adaln_kernel(c_ref, w_ref, b_ref, o_ref):
    c = c_ref[...]
    s = (c * _sigmoid(c)).astype(BF16)
    o_ref[0] = jnp.dot(s, w_ref[0].astype(BF16), preferred_element_type=F32) + b_ref[0]


def _adaln(cond, ada_w, ada_b):
    depth, d, n = ada_w.shape
    r = cond.shape[0]
    tn = _tile(n, 1536, LANES)
    return pl.pallas_call(
        _adaln_kernel,
        out_shape=jax.ShapeDtypeStruct((depth, r, n), F32),
        grid=(depth, n // tn),
        in_specs=[pl.BlockSpec((r, d), lambda l, j: (0, 0)),
                  pl.BlockSpec((1, d, tn), lambda l, j: (l, 0, j)),
                  pl.BlockSpec((1, 1, tn), lambda l, j: (l, 0, j))],
        out_specs=pl.BlockSpec((1, r, tn), lambda l, j: (l, 0, j)),
        compiler_params=_params(2), name="adaln",
    )(cond, ada_w, ada_b.reshape(depth, 1, n))


class _Rows:
    def __init__(self, nb, rpb, tm, mod_row):
        self.nb, self.rpb, self.tm = nb, rpb, tm
        self.nt = rpb // tm
        self.grid = (nb, self.nt)
        self.mod_row = mod_row

    def rows(self, cols):
        nt = self.nt
        return pl.BlockSpec((self.tm, cols), lambda b, i: (b * nt + i, 0))

    def rows3(self, per, cols):
        nt, tm = self.nt, self.tm
        if tm <= per:
            k = per // tm
            return pl.BlockSpec((None, tm, cols), lambda b, i: ((b * nt + i) // k, (b * nt + i) % k, 0))
        return pl.BlockSpec((tm // per, per, cols), lambda b, i: (b * nt + i, 0, 0))

    def mod(self, k):
        if self.mod_row is None:
            return pl.BlockSpec((None, 1, D_MODEL), lambda b, i: (b, 0, k))
        r = self.mod_row
        return pl.BlockSpec((None, 1, D_MODEL), lambda b, i: (r, 0, k))


class _ChunkRows:
    def __init__(self, nb, n_chunks, mod_row):
        self.nb, self.grid, self.mod_row = nb, (n_chunks,), mod_row
        self.tm = nb * SSM_CHUNK

    def rows(self, cols):
        return pl.BlockSpec((self.nb, SSM_CHUNK, cols), lambda c: (0, c, 0))

    def mod(self, k):
        if self.mod_row is None:
            return pl.BlockSpec((self.nb, 1, D_MODEL), lambda c: (0, 0, k))
        r = self.mod_row
        return pl.BlockSpec((1, 1, D_MODEL), lambda c: (r, 0, k))

    def chunk(self, first_chunk):
        return pl.BlockSpec((SLABS, None, self.nb, SSM_CHUNK * LANES), lambda c: (0, first_chunk + c, 0, 0))


def _even_in_kernel(*refs, aliased):
    if aliased:
        refs = refs[1:]
    x_ref, g_ref, sh_ref, sc_ref, w_ref, cs_ref, vc_ref, vs_ref, u_ref = refs[:9]
    zs_scr = refs[9:]
    nb, t_len, _ = x_ref.shape
    h = _norm_mod(x_ref[...], g_ref[...], sh_ref[...], sc_ref[...])
    z = jnp.dot(h.reshape(nb * t_len, D_MODEL).astype(BF16), w_ref[...], preferred_element_type=F32)
    zf = z[:, :D_FOURIER].astype(BF16)
    vc, vs = [], []
    for gi in range(FOURIER_GROUPS):
        v = jnp.dot(zf[:, gi * FOURIER_CH:(gi + 1) * FOURIER_CH], cs_ref[...], preferred_element_type=F32)
        vc.append(v[:, :FOURIER_CH].astype(BF16))
        vs.append(v[:, FOURIER_CH:].astype(BF16))
    vc_ref[...] = jnp.concatenate(vc, axis=1).reshape(nb, t_len, D_FOURIER)
    vs_ref[...] = jnp.concatenate(vs, axis=1).reshape(nb, t_len, D_FOURIER)
    for s in range(SLABS):
        zs_scr[s][...] = z[:, D_FOURIER + s * LANES:D_FOURIER + (s + 1) * LANES]
        for t in range(t_len):
            u_ref[s, :, t * LANES:(t + 1) * LANES] = zs_scr[s][pl.ds(t, nb, stride=t_len), :].astype(BF16)


def _even_in(x, rows, norm_g, mod, w_in, cs, u_shape, first_chunk, u=None):
    nb, n_tok, _ = x.shape
    outs = (jax.ShapeDtypeStruct((nb, n_tok, D_FOURIER), BF16), jax.ShapeDtypeStruct((nb, n_tok, D_FOURIER), BF16),
            jax.ShapeDtypeStruct(u_shape, BF16))
    in_specs = [rows.rows(D_MODEL), _const_spec((1, D_MODEL)), rows.mod(0), rows.mod(1),
                _const_spec((D_MODEL, D_MODEL)), _const_spec((FOURIER_CH, 2 * FOURIER_CH))]
    args = [x, norm_g, mod, mod, w_in, cs]
    aliases = {}
    if u is not None:
        in_specs = [pl.BlockSpec(memory_space=pl.ANY)] + in_specs
        args = [u] + args
        aliases = {0: 2}
    return pl.pallas_call(
        functools.partial(_even_in_kernel, aliased=u is not None), out_shape=outs, grid=rows.grid,
        in_specs=in_specs,
        out_specs=(rows.rows(D_FOURIER), rows.rows(D_FOURIER), rows.chunk(first_chunk)),
        scratch_shapes=[pltpu.VMEM((rows.tm, LANES), F32)] * SLABS,
        input_output_aliases=aliases,
        compiler_params=_params(1), name="even_in",
    )(*args)


def _fourier_kernel(cl_ref, sl_ref, vc_ref, vs_ref, o_ref, *, scale):
    a = jnp.dot(cl_ref[...], vc_ref[0], preferred_element_type=F32)
    a = a - jnp.dot(sl_ref[...], vs_ref[0], preferred_element_type=F32)
    o_ref[0] = (a * scale).astype(BF16)


def _fourier(vc3, vs3, cl, sl):
    nb, length, _ = vc3.shape
    scale = 1.0 / math.sqrt(length * FOURIER_CH)
    blk = pl.BlockSpec((1, length, D_FOURIER), lambda b: (b, 0, 0))
    return pl.pallas_call(
        functools.partial(_fourier_kernel, scale=scale),
        out_shape=jax.ShapeDtypeStruct((nb, length, D_FOURIER), BF16),
        grid=(nb,),
        in_specs=[_const_spec((length, length)), _const_spec((length, length)), blk, blk],
        out_specs=blk,
        compiler_params=_params(1), name="fourier",
    )(cl, sl, vc3, vs3)


def _dft_mats(n):
    k = lax.iota(jnp.int32, n)
    ang = ((k[:, None] * k[None, :]) % n).astype(F32) * (2.0 * math.pi / n)
    return jnp.cos(ang), jnp.sin(ang)


def _ssm_ops_kernel(ar_ref, ai_ref, ldt_ref, btr_ref, bti_ref, ctr_ref, cti_ref,
                    bm_ref, wh_ref, atr_ref, ati_ref, e_ref):
    t = SSM_CHUNK
    n = SLAB_STATE
    rows = lax.broadcasted_iota(jnp.int32, (LANES, n), 0)
    cols = lax.broadcasted_iota(jnp.int32, (LANES, n), 1)
    same_group = (rows // SSM_GROUP) == (cols // SSM_STATE)
    lag = lax.broadcasted_iota(jnp.int32, (t + 8, n), 0).astype(F32)
    nt_dims = (((1,), (1,)), ((), ()))
    for d in range(2):
        ar, ai = ar_ref[d], ai_ref[d]
        dt = jnp.exp(ldt_ref[d])
        xr, xi = ar * dt, ai * dt
        mag = jnp.exp(lag * xr)
        pr = mag * jnp.cos(lag * xi)
        pi = mag * jnp.sin(lag * xi)
        abr, abi = pr[1:2], pi[1:2]
        den = ar * ar + ai * ai
        fr = ((abr - 1.0) * ar + abi * ai) / den
        fi = (abi * ar - (abr - 1.0) * ai) / den
        br = jnp.where(same_group, btr_ref[d], 0.0)
        bi = jnp.where(same_group, bti_ref[d], 0.0)
        bbr = fr * br - fi * bi
        bbi = fr * bi + fi * br
        cr = jnp.where(same_group, ctr_ref[d], 0.0)
        ci = jnp.where(same_group, cti_ref[d], 0.0)
        for k in range(t):
            pkr, pki = pr[k:k + 1], pi[k:k + 1]
            x_r = bbr * pkr - bbi * pki
            x_i = bbr * pki + bbi * pkr
            s = t - 1 - k if d == 0 else k
            bm_ref[s * LANES:(s + 1) * LANES, 2 * d * n:(2 * d + 1) * n] = x_r.astype(BF16)
            bm_ref[s * LANES:(s + 1) * LANES, (2 * d + 1) * n:(2 * d + 2) * n] = x_i.astype(BF16)
            e_ref[d, k] = (lax.dot_general(x_r, cr, nt_dims, precision=lax.Precision.HIGHEST,
                                           preferred_element_type=F32)
                           - lax.dot_general(x_i, ci, nt_dims, precision=lax.Precision.HIGHEST,
                                             preferred_element_type=F32))
        for k in range(1, t + 1):
            pkr, pki = pr[k:k + 1], pi[k:k + 1]
            w_r = cr * pkr - ci * pki
            w_i = cr * pki + ci * pkr
            tt = k - 1 if d == 0 else t - k
            wh_ref[2 * d * n:(2 * d + 1) * n, tt * LANES:(tt + 1) * LANES] = w_r.T.astype(BF16)
            wh_ref[(2 * d + 1) * n:(2 * d + 2) * n, tt * LANES:(tt + 1) * LANES] = (-w_i).T.astype(BF16)
        atr_ref[d] = pr[t:t + 1]
        ati_ref[d] = pi[t:t + 1]


def _ssm_toeplitz_kernel(e_ref, dsk_ref, m_ref):
    t = SSM_CHUNK
    s = pl.program_id(1)
    r = lax.broadcasted_iota(jnp.int32, (LANES, LANES), 0)
    c = lax.broadcasted_iota(jnp.int32, (LANES, LANES), 1)
    skip = jnp.where(r == c, jnp.broadcast_to(dsk_ref[...], (LANES, LANES)), 0.0)
    for tt in range(t):
        blk = ((tt >= s).astype(F32) * e_ref[0, jnp.maximum(tt - s, 0)]
               + (s >= tt).astype(F32) * e_ref[1, jnp.maximum(s - tt, 0)]
               + (s == tt).astype(F32) * skip)
        m_ref[:, tt * LANES:(tt + 1) * LANES] = blk.astype(BF16)


def _ssm_ops(a_re, a_im, log_dt, b_re, b_im, c_re, c_im, d_skip):
    t = SSM_CHUNK
    n = SLAB_STATE

    def per_state(x):
        return x.astype(F32).reshape(2, SLABS, 1, n)

    def per_channel(x):
        x = x.astype(F32).reshape(2, SLABS, LANES, SSM_STATE)
        return jnp.tile(x, (1, 1, 1, SLAB_GROUPS))

    ldt = jnp.broadcast_to(log_dt[..., None], a_re.shape)
    vec = pl.BlockSpec((2, None, 1, n), lambda z: (0, z, 0, 0))
    mat = pl.BlockSpec((2, None, LANES, n), lambda z: (0, z, 0, 0))
    bm, wh, at_re, at_im, e = pl.pallas_call(
        _ssm_ops_kernel,
        out_shape=(jax.ShapeDtypeStruct((SLABS, t * LANES, 4 * n), BF16),
                   jax.ShapeDtypeStruct((SLABS, 4 * n, t * LANES), BF16),
                   jax.ShapeDtypeStruct((SLABS, 2, 1, n), F32),
                   jax.ShapeDtypeStruct((SLABS, 2, 1, n), F32),
                   jax.ShapeDtypeStruct((SLABS, 2, t, LANES, LANES), F32)),
        grid=(SLABS,),
        in_specs=[vec, vec, vec, mat, mat, mat, mat],
        out_specs=(pl.BlockSpec((None, t * LANES, 4 * n), lambda z: (z, 0, 0)),
                   pl.BlockSpec((None, 4 * n, t * LANES), lambda z: (z, 0, 0)),
                   pl.BlockSpec((None, 2, 1, n), lambda z: (z, 0, 0, 0)),
                   pl.BlockSpec((None, 2, 1, n), lambda z: (z, 0, 0, 0)),
                   pl.BlockSpec((None, 2, t, LANES, LANES), lambda z: (z, 0, 0, 0, 0))),
        compiler_params=_params(1), name="ssm_ops",
    )(per_state(a_re), per_state(a_im), per_state(ldt),
      per_channel(b_re.transpose(0, 1, 3, 2)), per_channel(b_im.transpose(0, 1, 3, 2)),
      per_channel(c_re), per_channel(c_im))
    m = pl.pallas_call(
        _ssm_toeplitz_kernel,
        out_shape=jax.ShapeDtypeStruct((SLABS, t * LANES, t * LANES), BF16),
        grid=(SLABS, t),
        in_specs=[pl.BlockSpec((None, 2, t, LANES, LANES), lambda z, s: (z, 0, 0, 0, 0)),
                  pl.BlockSpec((None, 1, LANES), lambda z, s: (z, 0, 0))],
        out_specs=pl.BlockSpec((None, LANES, t * LANES), lambda z, s: (z, s, 0)),
        compiler_params=_params(2), name="ssm_toeplitz",
    )(e, d_skip.astype(F32).reshape(SLABS, 1, LANES))
    return m, bm, wh, at_re.reshape(SLABS * 2, 1, n), at_im.reshape(SLABS * 2, 1, n)


def _ssm_contrib_kernel(u_ref, bm_ref, o_ref):
    o_ref[...] = jnp.dot(u_ref[...], bm_ref[...], preferred_element_type=F32)


def _ssm_contrib(u, bm):
    slabs, r, k = u.shape
    n = bm.shape[2]
    tr = _tile(r, 576)
    return pl.pallas_call(
        _ssm_contrib_kernel,
        out_shape=jax.ShapeDtypeStruct((slabs, r, n), F32),
        grid=(slabs, r // tr),
        in_specs=[pl.BlockSpec((None, tr, k), lambda s, i: (s, i, 0)),
                  pl.BlockSpec((None, k, n), lambda s, i: (s, 0, 0))],
        out_specs=pl.BlockSpec((None, tr, n), lambda s, i: (s, i, 0)),
        compiler_params=_params(2), name="ssm_contrib",
    )(u, bm)


def _ssm_scan_kernel(ar_ref, ai_ref, cr_ref, ci_ref, hr_ref, hi_ref, *, ncc, nct, rb):
    d = pl.program_id(1)
    lt = ar_ref.shape[-1]
    ncx = nct - ncc
    ar = jnp.broadcast_to(ar_ref[...], (rb, lt))
    ai = jnp.broadcast_to(ai_ref[...], (rb, lt))
    zero = jnp.zeros((rb, lt), F32)

    def advance(i_src, i_dst, hr, hi):
        r0 = pl.multiple_of(i_src * rb, rb)
        r1 = pl.multiple_of(i_dst * rb, rb)
        nr = ar * hr - ai * hi + cr_ref[pl.ds(r0, rb), :]
        ni = ar * hi + ai * hr + ci_ref[pl.ds(r0, rb), :]
        hr_ref[pl.ds(r1, rb), :] = nr
        hi_ref[pl.ds(r1, rb), :] = ni
        return nr, ni

    def start(i):
        hr_ref[pl.ds(i * rb, rb), :] = zero
        hi_ref[pl.ds(i * rb, rb), :] = zero
        return zero, zero

    @pl.when(d == 0)
    def _():
        c = start(ncx)
        c = lax.fori_loop(0, ncc - 1, lambda k, c: advance(ncx + k, ncx + k + 1, *c), c)
        c = advance(nct - 1, 0, *c)
        lax.fori_loop(0, ncx - 1, lambda k, c: advance(k, k + 1, *c), c)

    @pl.when(d == 1)
    def _():
        c = start(nct - 1)
        c = lax.fori_loop(0, ncc - 1, lambda k, c: advance(nct - 1 - k, nct - 2 - k, *c), c)
        c = advance(ncx, ncx - 1, *c)
        lax.fori_loop(0, ncx - 1, lambda k, c: advance(ncx - 1 - k, ncx - 2 - k, *c), c)


def _ssm_scan(contrib, at_re, at_im, ncc, nct, rb):
    slabs, r, n = contrib.shape
    lt = LANES
    q = SLAB_STATE // lt
    re_spec = pl.BlockSpec((None, r, lt), lambda s, d, j: (s, 0, d * 2 * q + j))
    im_spec = pl.BlockSpec((None, r, lt), lambda s, d, j: (s, 0, d * 2 * q + q + j))
    a_spec = pl.BlockSpec((None, 1, lt), lambda s, d, j: (s * 2 + d, 0, j))
    h_spec = pl.BlockSpec((None, r, lt), lambda s, d, j: (s, 0, d * q + j))
    out = jax.ShapeDtypeStruct((slabs, r, n // 2), F32)
    return pl.pallas_call(
        functools.partial(_ssm_scan_kernel, ncc=ncc, nct=nct, rb=rb),
        out_shape=(out, out),
        grid=(slabs, 2, q),
        in_specs=[a_spec, a_spec, re_spec, im_spec],
        out_specs=(h_spec, h_spec),
        compiler_params=_params(3), name="ssm_scan",
    )(at_re, at_im, contrib, contrib)


def _ssm_read_kernel(u_ref, hr_ref, hi_ref, m_ref, wh_ref, o_ref):
    q = SLAB_STATE
    acc = jnp.dot(u_ref[...], m_ref[...], preferred_element_type=F32)
    for d in range(2):
        acc += jnp.dot(hr_ref[:, d * q:(d + 1) * q].astype(BF16), wh_ref[2 * d * q:(2 * d + 1) * q, :],
                       preferred_element_type=F32)
        acc += jnp.dot(hi_ref[:, d * q:(d + 1) * q].astype(BF16), wh_ref[(2 * d + 1) * q:(2 * d + 2) * q, :],
                       preferred_element_type=F32)
    o_ref[...] = acc.astype(BF16)


def _ssm_read(u, h_re, h_im, m, wh):
    slabs, r, k = u.shape
    ns = h_re.shape[2]
    nw = wh.shape[1]
    tr = _tile(r, 576)
    tn = k // 2
    return pl.pallas_call(
        _ssm_read_kernel,
        out_shape=jax.ShapeDtypeStruct((slabs, r, k), BF16),
        grid=(slabs, 2, r // tr),
        in_specs=[pl.BlockSpec((None, tr, k), lambda s, j, i: (s, i, 0)),
                  pl.BlockSpec((None, tr, ns), lambda s, j, i: (s, i, 0)),
                  pl.BlockSpec((None, tr, ns), lambda s, j, i: (s, i, 0)),
                  pl.BlockSpec((None, k, tn), lambda s, j, i: (s, 0, j)),
                  pl.BlockSpec((None, nw, tn), lambda s, j, i: (s, 0, j))],
        out_specs=pl.BlockSpec((None, tr, tn), lambda s, j, i: (s, i, j)),
        compiler_params=_params(3), name="ssm_read",
    )(u, h_re, h_im, m, wh)


MLP_FF_CHUNK = 1024


def _mlp_tail(x, g_ref, sh_ref, sc_ref, gate_ref, w1_ref, w2_ref, o_ref):
    fc = MLP_FF_CHUNK
    h = _norm_mod(x, g_ref[...], sh_ref[...], sc_ref[...]).reshape(-1, D_MODEL).astype(BF16)
    acc = jnp.zeros(h.shape, F32)
    for k in range(D_FF // fc):
        a = jnp.dot(h, w1_ref[:, k * fc:(k + 1) * fc], preferred_element_type=F32)
        a = jnp.square(jnp.maximum(a, 0.0)).astype(BF16)
        acc += jnp.dot(a, w2_ref[k * fc:(k + 1) * fc, :], preferred_element_type=F32)
    o_ref[...] = x + gate_ref[...] * acc.reshape(x.shape)


def _even_tail_kernel(ya_ref, y_ref, x_ref, gate1_ref, wg_ref, bg_ref, wa_ref, wb_ref,
                      g_ref, sh_ref, sc_ref, gate2_ref, w1_ref, w2_ref, o_ref, *ys_scr):
    nb, t_len, _ = x_ref.shape
    for s in range(SLABS):
        for t in range(t_len):
            ys_scr[s][pl.ds(t, nb, stride=t_len), :] = y_ref[s, :, t * LANES:(t + 1) * LANES].astype(F32)
    y = _gelu_tanh(jnp.concatenate([r[...] for r in ys_scr], axis=1))
    glu = y * _sigmoid(jnp.dot(y.astype(BF16), wg_ref[...], preferred_element_type=F32) + bg_ref[...])
    acc = jnp.dot(ya_ref[...].reshape(nb * t_len, D_FOURIER), wa_ref[...], preferred_element_type=F32)
    acc += jnp.dot(glu.astype(BF16), wb_ref[...], preferred_element_type=F32)
    x1 = x_ref[...] + gate1_ref[...] * acc.reshape(nb, t_len, D_MODEL)
    _mlp_tail(x1, g_ref, sh_ref, sc_ref, gate2_ref, w1_ref, w2_ref, o_ref)


def _odd_tail_kernel(a_ref, x_ref, gate1_ref, wo_ref, g_ref, sh_ref, sc_ref, gate2_ref, w1_ref, w2_ref, o_ref):
    a = a_ref[...].reshape(x_ref.shape)
    x1 = x_ref[...] + gate1_ref[...] * jnp.dot(a, wo_ref[...], preferred_element_type=F32)
    _mlp_tail(x1, g_ref, sh_ref, sc_ref, gate2_ref, w1_ref, w2_ref, o_ref)


def _mlp_specs(rows):
    return [_const_spec((1, D_MODEL)), rows.mod(3), rows.mod(4), rows.mod(5),
            _const_spec((D_MODEL, D_FF)), _const_spec((D_FF, D_MODEL))]


def _even_tail(ya, y, first_chunk, x, rows, mod, w_glu, b_glu, w_out, norm_g, w1, w2):
    return pl.pallas_call(
        _even_tail_kernel, out_shape=jax.ShapeDtypeStruct(x.shape, F32), grid=rows.grid,
        in_specs=[rows.rows(D_FOURIER), rows.chunk(first_chunk), rows.rows(D_MODEL), rows.mod(2),
                  _const_spec((D_SSM, D_SSM)), _const_spec((1, D_SSM)),
                  _const_spec((D_FOURIER, D_MODEL)), _const_spec((D_SSM, D_MODEL))] + _mlp_specs(rows),
        out_specs=rows.rows(D_MODEL),
        scratch_shapes=[pltpu.VMEM((rows.tm, LANES), F32)] * SLABS,
        compiler_params=_params(1), name="even_tail",
    )(ya, y, x, mod, w_glu, b_glu, w_out[:D_FOURIER], w_out[D_FOURIER:], norm_g, mod, mod, mod, w1, w2)


def _odd_tail(a, x, rows, mod, w_out, norm_g, w1, w2):
    return pl.pallas_call(
        _odd_tail_kernel, out_shape=jax.ShapeDtypeStruct(x.shape, F32), grid=rows.grid,
        in_specs=[rows.rows3(a.shape[1], D_MODEL), rows.rows(D_MODEL), rows.mod(2),
                  _const_spec((D_MODEL, D_MODEL))] + _mlp_specs(rows),
        out_specs=rows.rows(D_MODEL),
        compiler_params=_params(2), name="odd_tail",
    )(a, x, mod, w_out, norm_g, mod, mod, mod, w1, w2)


KEY_CHUNK = 256
V_ROWS = DA_V_DIM + 16
Q_SCALE = DA_HEAD_DIM ** -0.5 * math.log2(math.e)


def _qkv_kernel(*refs, rope):
    if rope:
        (x_ref, g_ref, sh_ref, sc_ref, w_ref, gqk_ref, ones_ref,
         cos_ref, sa_ref, sb_ref, qt_ref, k_ref, vt_ref) = refs
    else:
        x_ref, g_ref, sh_ref, sc_ref, w_ref, gqk_ref, ones_ref, qt_ref, k_ref, vt_ref = refs
    h = _norm_mod(x_ref[...], g_ref[...], sh_ref[...], sc_ref[...]).astype(BF16)
    y = jnp.dot(h, w_ref[...], preferred_element_type=F32)
    tm = y.shape[0]
    for s in range(2 * DA_HEADS):
        blk = y[:, s * LANES:(s + 1) * LANES]
        ssq = jnp.dot((blk * blk).astype(BF16), ones_ref[...], preferred_element_type=F32)
        nrm = blk * gqk_ref[s // DA_HEADS]
        if rope:
            nrm = (nrm * cos_ref[...] + pltpu.roll(nrm, LANES - 16, 1) * sa_ref[...]
                   + pltpu.roll(nrm, 16, 1) * sb_ref[...])
        nrm = nrm * lax.rsqrt(ssq * (1.0 / DA_HEAD_DIM) + EPS)
        if s < DA_HEADS:
            qt_ref[s] = nrm.T.astype(BF16)
        else:
            hd = s - DA_HEADS
            k_ref[:, hd * LANES:(hd + 1) * LANES] = nrm.astype(BF16)
    ones = jnp.ones((V_ROWS - DA_V_DIM, tm), BF16)
    for hd in range(DA_HEADS):
        v = y[:, (2 * DA_HEADS + hd) * LANES:(2 * DA_HEADS + hd + 1) * LANES]
        vt_ref[hd, :DA_V_DIM, :] = v.T.astype(BF16)
        vt_ref[hd, DA_V_DIM:, :] = ones


def _qkv(x, rows, norm_g, mod, w_in, gqk, ones_bd, rope_tabs):
    nb, rpb, tm = rows.nb, rows.rpb, rows.tm
    assert tm % KEY_CHUNK == 0
    rope = rope_tabs is not None
    in_specs = [rows.rows(D_MODEL), _const_spec((1, D_MODEL)), rows.mod(0), rows.mod(1),
                _const_spec((D_MODEL, 3 * D_MODEL)), _const_spec((2, 1, LANES)),
                _const_spec((LANES, LANES))]
    args = [x, norm_g, mod, mod, w_in, gqk, ones_bd]
    if rope:
        tab = pl.BlockSpec((tm, LANES), lambda b, i: (i, 0))
        in_specs += [tab, tab, tab]
        args += list(rope_tabs)
    out_shape = (jax.ShapeDtypeStruct((nb, DA_HEADS, LANES, rpb), BF16),
                 jax.ShapeDtypeStruct((nb, rpb, D_MODEL), BF16),
                 jax.ShapeDtypeStruct((nb, DA_HEADS, V_ROWS, rpb), BF16))
    out_specs = (pl.BlockSpec((None, DA_HEADS, LANES, tm), lambda b, i: (b, 0, 0, i)),
                 pl.BlockSpec((None, tm, D_MODEL), lambda b, i: (b, i, 0)),
                 pl.BlockSpec((None, DA_HEADS, V_ROWS, tm), lambda b, i: (b, 0, 0, i)))
    return pl.pallas_call(
        functools.partial(_qkv_kernel, rope=rope),
        out_shape=out_shape, grid=rows.grid, in_specs=in_specs, out_specs=out_specs,
        compiler_params=_params(2), name="qkv",
    )(*args)


def _rope_tables(length):
    pos = jnp.arange(length, dtype=jnp.int32)
    row = (pos // GRID_W).astype(F32)
    col = (pos % GRID_W).astype(F32)
    half = DA_HEAD_DIM // 2
    inv_freq = ROPE_THETA ** (-jnp.arange(0, half, 2, dtype=F32) / half)
    lane = jnp.arange(LANES)
    f = lane % 16
    is_col = (lane % DA_HEAD_DIM) >= half
    second = (lane % half) >= 16
    ang = jnp.where(is_col[None, :], col[:, None], row[:, None]) * inv_freq[f][None, :]
    cos, sin = jnp.cos(ang), jnp.sin(ang)
    sin_a = jnp.where(second[None, :], 0.0, -sin)
    sin_b = jnp.where(second[None, :], sin, 0.0)
    return cos, sin_a, sin_b


def _attn_kernel(*refs, n_kv, lam_init, tq):
    lp_ref, gh_ref, qt_ref = refs[:3]
    kv_refs = refs[3:3 + 2 * n_kv]
    o_ref = refs[3 + 2 * n_kv]
    bufs = refs[4 + 2 * n_kv:]
    lp = lp_ref[...]
    lam = (jnp.exp(jnp.sum(lp[0:1] * lp[1:2], axis=-1, keepdims=True))
           - jnp.exp(jnp.sum(lp[2:3] * lp[3:4], axis=-1, keepdims=True)) + lam_init)
    nq = qt_ref.shape[1] // tq
    kc = KEY_CHUNK

    key_chunks = []
    for i in range(n_kv):
        for c in range(kv_refs[2 * i].shape[0] // kc):
            key_chunks.append((kv_refs[2 * i], kv_refs[2 * i + 1], c))

    def masked_q(j):
        qt = qt_ref[:, j * tq:(j + 1) * tq]
        row = lax.broadcasted_iota(jnp.int32, qt.shape, 0)
        zero = jnp.zeros_like(qt)
        return jnp.concatenate([jnp.where(row < DA_HEAD_DIM, qt, zero),
                                jnp.where(row >= DA_HEAD_DIM, qt, zero)], axis=1)

    def score_chunk(n, qst, s_scr, m):
        k_ref, _, c = key_chunks[n]
        s = jnp.dot(k_ref[c * kc:(c + 1) * kc, :], qst, preferred_element_type=F32)
        s_scr[n * kc:(n + 1) * kc, :] = s
        return jnp.maximum(m, s.reshape(kc // 8, 8, 2 * tq).max(axis=0))

    def value_chunk(n, s_scr, m, acc):
        _, vt_ref, c = key_chunks[n]
        p = jnp.exp2((s_scr[n * kc:(n + 1) * kc, :] - m).astype(BF16))
        return acc + jnp.dot(vt_ref[:, c * kc:(c + 1) * kc], p, preferred_element_type=F32)

    def finish(j, acc):
        o2 = acc[:DA_V_DIM] / acc[DA_V_DIM:DA_V_DIM + 1]
        o = o2[:, :tq] - lam * o2[:, tq:]
        o = o * lax.rsqrt(jnp.mean(o * o, axis=0, keepdims=True) + EPS) * gh_ref[...]
        o_ref[j * tq:(j + 1) * tq, :] = (o * (1.0 - lam_init)).T.astype(BF16)

    neg = jnp.full((8, 2 * tq), -jnp.inf, F32)
    qst = masked_q(0)
    m8 = neg
    for n in range(len(key_chunks)):
        m8 = score_chunk(n, qst, bufs[0], m8)
    for j in range(nq):
        m = m8.max(axis=0, keepdims=True)
        acc = jnp.zeros((V_ROWS, 2 * tq), F32)
        if j + 1 < nq:
            qst = masked_q(j + 1)
            m8 = neg
        for n in range(len(key_chunks)):
            if j + 1 < nq:
                m8 = score_chunk(n, qst, bufs[(j + 1) % 2], m8)
            acc = value_chunk(n, bufs[j % 2], m, acc)
        finish(j, acc)


def _attention(qt, kv_list, lam_p, g_head, lam_init, tq):
    nb, _, _, lq = qt.shape
    in_specs = [_const_spec((4, DA_HEAD_DIM)), _const_spec((DA_V_DIM, 1)),
                pl.BlockSpec((None, None, LANES, lq), lambda b, h: (b, h, 0, 0))]
    args = [lam_p, g_head, qt]
    total = 0
    for k, vt in kv_list:
        lk = k.shape[1]
        total += lk
        in_specs.append(pl.BlockSpec((None, lk, LANES), lambda b, h: (b, 0, h)))
        in_specs.append(pl.BlockSpec((None, None, V_ROWS, lk), lambda b, h: (b, h, 0, 0)))
        args += [k, vt]
    n_bufs = min(2, lq // tq)
    return pl.pallas_call(
        functools.partial(_attn_kernel, n_kv=len(kv_list), lam_init=lam_init, tq=tq),
        out_shape=jax.ShapeDtypeStruct((nb, lq, D_MODEL), BF16),
        grid=(nb, DA_HEADS),
        in_specs=in_specs,
        out_specs=pl.BlockSpec((None, lq, LANES), lambda b, h: (b, 0, h)),
        scratch_shapes=[pltpu.VMEM((total, 2 * tq), F32)] * n_bufs,
        compiler_params=_params(2), name="diff_attn",
    )(*args)


def kernel(x, c, ctx, c_ctx, norm1_g, norm2_g, ada_w, ada_b, mlp_w1, mlp_w2, ev_w_in, ev_w_out, ssm_a_re, ssm_a_im, ssm_log_dt, ssm_b_re, ssm_b_im, ssm_c_re, ssm_c_im, ssm_d, ssm_w_glu, ssm_b_glu, od_w_in, od_w_out, od_q_norm, od_k_norm, od_lambda, od_head_norm):
    nb, length, d = x.shape
    clen = ctx.shape[1]
    assert d == D_MODEL and length % GRID_W == 0
    assert length % SSM_CHUNK == 0 and clen % SSM_CHUNK == 0

    xs = x.reshape(nb * length, d)
    cs = ctx.reshape(nb * clen, d)
    rows_x = _Rows(nb, length, _tile(length, 512), None)
    rows_c = _Rows(1, nb * clen, _tile(nb * clen, 512), nb)
    rows_cb = _Rows(nb, clen, clen, nb)
    chunks_x = _ChunkRows(nb, length // SSM_CHUNK, None)
    chunks_c = _ChunkRows(nb, clen // SSM_CHUNK, nb)

    n_cond = nb + 1
    pad = (-n_cond) % 8
    cond = jnp.concatenate([c, c_ctx[None, :], jnp.zeros((pad, d), F32)], axis=0)
    mods = _adaln(cond, ada_w, ada_b)[:, :n_cond].reshape(DEPTH, n_cond, 1, 6 * d)

    kf = lax.iota(jnp.int32, FOURIER_CH)
    ang = ((kf[:, None] * kf[None, :]) % FOURIER_CH).astype(F32) * (2.0 * math.pi / FOURIER_CH)
    cs_ch = jnp.concatenate([jnp.cos(ang), jnp.sin(ang)], axis=1).astype(BF16)
    cl_x, sl_x = (m.astype(BF16) for m in _dft_mats(length))
    cl_c, sl_c = (m.astype(BF16) for m in _dft_mats(clen))
    rope_tabs = _rope_tables(length)
    lane = jnp.arange(LANES)
    ones_bd = (lane[:, None] // DA_HEAD_DIM == lane[None, :] // DA_HEAD_DIM).astype(BF16)

    ncc = clen // SSM_CHUNK
    ncx = length // SSM_CHUNK
    nct = ncx + ncc

    for i in range(DEPTH):
        last = i == DEPTH - 1
        j = i // 2
        mod = mods[i]
        n1 = norm1_g[i].reshape(1, d)
        n2 = norm2_g[i].reshape(1, d)
        if i % 2 == 0:
            w_in = ev_w_in[j].astype(BF16)
            u_shape = (SLABS, nct, nb, SSM_CHUNK * LANES)
            x3 = xs.reshape(nb, length, d)
            c3 = cs.reshape(nb, clen, d)
            vc_x, vs_x, u = _even_in(x3, chunks_x, n1, mod, w_in, cs_ch, u_shape, 0)
            vc_c, vs_c, u = _even_in(c3, chunks_c, n1, mod, w_in, cs_ch, u_shape, ncx, u)
            ya_x = _fourier(vc_x, vs_x, cl_x, sl_x)
            m, bm, wh, at_re, at_im = _ssm_ops(ssm_a_re[j], ssm_a_im[j], ssm_log_dt[j], ssm_b_re[j],
                                                ssm_b_im[j], ssm_c_re[j], ssm_c_im[j], ssm_d[j])
            u = u.reshape(SLABS, nct * nb, SSM_CHUNK * LANES)
            contrib = _ssm_contrib(u, bm)
            h_re, h_im = _ssm_scan(contrib, at_re, at_im, ncc, nct, nb)
            y = _ssm_read(u, h_re, h_im, m, wh).reshape(u_shape)
            w_glu = ssm_w_glu[j].astype(BF16)
            b_glu = ssm_b_glu[j].reshape(1, D_SSM)
            w_out = ev_w_out[j].astype(BF16)
            w1 = mlp_w1[i].astype(BF16)
            w2 = mlp_w2[i].astype(BF16)
            xs = _even_tail(ya_x, y, 0, x3, chunks_x, mod, w_glu, b_glu, w_out, n2, w1, w2).reshape(nb * length, d)
            if not last:
                ya_c = _fourier(vc_c, vs_c, cl_c, sl_c)
                cs = _even_tail(ya_c, y, ncx, c3, chunks_c, mod, w_glu, b_glu, w_out, n2, w1, w2).reshape(nb * clen, d)
        else:
            lam_init = 0.8 - 0.6 * math.exp(-0.3 * i)
            w_in = od_w_in[j].astype(BF16)
            gqk = jnp.stack([jnp.tile(od_q_norm[j], 2) * Q_SCALE,
                             jnp.tile(od_k_norm[j], 2)]).reshape(2, 1, LANES)
            qt_x, k_x, vt_x = _qkv(xs, rows_x, n1, mod, w_in, gqk, ones_bd, rope_tabs)
            qt_c, k_c, vt_c = _qkv(cs, rows_cb, n1, mod, w_in, gqk, ones_bd, None)
            g_head = od_head_norm[j].reshape(DA_V_DIM, 1)
            w_out = od_w_out[j].astype(BF16)
            o_x = _attention(qt_x, [(k_x, vt_x), (k_c, vt_c)], od_lambda[j], g_head, lam_init,
                             _tile(length, 512, LANES))
            w1 = mlp_w1[i].astype(BF16)
            w2 = mlp_w2[i].astype(BF16)
            xs = _odd_tail(o_x, xs, rows_x, mod, w_out, n2, w1, w2)
            if not last:
                o_c = _attention(qt_c, [(k_c, vt_c)], od_lambda[j], g_head, lam_init, clen)
                cs = _odd_tail(o_c, cs, rows_c, mod, w_out, n2, w1, w2)
    return xs.reshape(nb, length, d)
```

```python
import functools
import math

import jax
import jax.numpy as jnp
from jax import lax
from jax.experimental import pallas as pl
from jax.experimental.pallas import tpu as pltpu

F32 = jnp.float32
BF16 = jnp.bfloat16

D_MODEL = 1024
DEPTH = 4
GRID_W = 64
EPS = 1e-6
D_FF = 4 * D_MODEL
D_FOURIER = D_MODEL // 2
FOURIER_GROUPS = 4
FOURIER_CH = D_FOURIER // FOURIER_GROUPS
D_SSM = D_MODEL - D_FOURIER
SSM_GROUP = 16
SSM_GROUPS = D_SSM // SSM_GROUP
SSM_STATE = 64
DA_HEAD_DIM = 64
DA_V_DIM = 2 * DA_HEAD_DIM
DA_HEADS = D_MODEL // DA_V_DIM
ROPE_THETA = 10000.0

LANES = 128
SSM_CHUNK = 16
SLABS = D_SSM // LANES
SLAB_GROUPS = LANES // SSM_GROUP
SLAB_STATE = SLAB_GROUPS * SSM_STATE
VMEM_LIMIT = 56 * 1024 * 1024


def _params(n_axes, flags=None):
    return pltpu.CompilerParams(dimension_semantics=("arbitrary",) * n_axes,
                                vmem_limit_bytes=VMEM_LIMIT, flags=flags)


def _tile(n, target, mult=8):
    t = min(n, target)
    while t > 1 and (n % t or t % mult):
        t -= 1
    return t if n % t == 0 else n


def _const_spec(shape):
    nd = len(shape)
    return pl.BlockSpec(shape, lambda *_: (0,) * nd, pipeline_mode=pl.Buffered(1))


def _sigmoid(x):
    return 1.0 / (1.0 + jnp.exp(-x))


def _gelu_tanh(x):
    c = math.sqrt(2.0 / math.pi)
    return x * (0.5 * (1.0 + jnp.tanh(c * (x + 0.044715 * (x * x * x)))))


def _norm_mod(x, g, shift, scale):
    ms = jnp.mean(x * x, axis=-1, keepdims=True)
    y = x * lax.rsqrt(ms + EPS) * g
    return y * (1.0 + scale) + shift


def _adaln_kernel(c_ref, w_ref, b_ref, o_ref):
    c = c_ref[...]
    s = (c * _sigmoid(c)).astype(BF16)
    o_ref[0] = jnp.dot(s, w_ref[0].astype(BF16), preferred_element_type=F32) + b_ref[0]


def _adaln(cond, ada_w, ada_b):
    depth, d, n = ada_w.shape
    r = cond.shape[0]
    tn = _tile(n, 1536, LANES)
    return pl.pallas_call(
        _adaln_kernel,
        out_shape=jax.ShapeDtypeStruct((depth, r, n), F32),
        grid=(depth, n // tn),
        in_specs=[pl.BlockSpec((r, d), lambda l, j: (0, 0)),
                  pl.BlockSpec((1, d, tn), lambda l, j: (l, 0, j)),
                  pl.BlockSpec((1, 1, tn), lambda l, j: (l, 0, j))],
        out_specs=pl.BlockSpec((1, r, tn), lambda l, j: (l, 0, j)),
        compiler_params=_params(2), name="adaln",
    )(cond, ada_w, ada_b.reshape(depth, 1, n))


class _Rows:
    def __init__(self, nb, rpb, tm, mod_row):
        self.nb, self.rpb, self.tm = nb, rpb, tm
        self.nt = rpb // tm
        self.grid = (nb, self.nt)
        self.mod_row = mod_row

    def rows(self, cols):
        nt = self.nt
        return pl.BlockSpec((self.tm, cols), lambda b, i: (b * nt + i, 0))

    def rows3(self, per, cols):
        nt, tm = self.nt, self.tm
        if tm <= per:
            k = per // tm
            return pl.BlockSpec((None, tm, cols), lambda b, i: ((b * nt + i) // k, (b * nt + i) % k, 0))
        return pl.BlockSpec((tm // per, per, cols), lambda b, i: (b * nt + i, 0, 0))

    def mod(self, k):
        if self.mod_row is None:
            return pl.BlockSpec((None, 1, D_MODEL), lambda b, i: (b, 0, k))
        r = self.mod_row
        return pl.BlockSpec((None, 1, D_MODEL), lambda b, i: (r, 0, k))


class _ChunkRows:
    def __init__(self, nb, n_chunks, mod_row):
        self.nb, self.grid, self.mod_row = nb, (n_chunks,), mod_row
        self.tm = nb * SSM_CHUNK

    def rows(self, cols):
        return pl.BlockSpec((self.nb, SSM_CHUNK, cols), lambda c: (0, c, 0))

    def mod(self, k):
        if self.mod_row is None:
            return pl.BlockSpec((self.nb, 1, D_MODEL), lambda c: (0, 0, k))
        r = self.mod_row
        return pl.BlockSpec((1, 1, D_MODEL), lambda c: (r, 0, k))

    def chunk(self, first_chunk):
        return pl.BlockSpec((SLABS, None, self.nb, SSM_CHUNK * LANES), lambda c: (0, first_chunk + c, 0, 0))


def _even_in_kernel(*refs, aliased):
    if aliased:
        refs = refs[1:]
    x_ref, g_ref, sh_ref, sc_ref, w_ref, cs_ref, vc_ref, vs_ref, u_ref = refs[:9]
    zs_scr = refs[9:]
    nb, t_len, _ = x_ref.shape
    h = _norm_mod(x_ref[...], g_ref[...], sh_ref[...], sc_ref[...])
    z = jnp.dot(h.reshape(nb * t_len, D_MODEL).astype(BF16), w_ref[...], preferred_element_type=F32)
    zf = z[:, :D_FOURIER].astype(BF16)
    vc, vs = [], []
    for gi in range(FOURIER_GROUPS):
        v = jnp.dot(zf[:, gi * FOURIER_CH:(gi + 1) * FOURIER_CH], cs_ref[...], preferred_element_type=F32)
        vc.append(v[:, :FOURIER_CH].astype(BF16))
        vs.append(v[:, FOURIER_CH:].astype(BF16))
    vc_ref[...] = jnp.concatenate(vc, axis=1).reshape(nb, t_len, D_FOURIER)
    vs_ref[...] = jnp.concatenate(vs, axis=1).reshape(nb, t_len, D_FOURIER)
    for s in range(SLABS):
        zs_scr[s][...] = z[:, D_FOURIER + s * LANES:D_FOURIER + (s + 1) * LANES]
        for t in range(t_len):
            u_ref[s, :, t * LANES:(t + 1) * LANES] = zs_scr[s][pl.ds(t, nb, stride=t_len), :].astype(BF16)


def _even_in(x, rows, norm_g, mod, w_in, cs, u_shape, first_chunk, u=None):
    nb, n_tok, _ = x.shape
    outs = (jax.ShapeDtypeStruct((nb, n_tok, D_FOURIER), BF16), jax.ShapeDtypeStruct((nb, n_tok, D_FOURIER), BF16),
            jax.ShapeDtypeStruct(u_shape, BF16))
    in_specs = [rows.rows(D_MODEL), _const_spec((1, D_MODEL)), rows.mod(0), rows.mod(1),
                _const_spec((D_MODEL, D_MODEL)), _const_spec((FOURIER_CH, 2 * FOURIER_CH))]
    args = [x, norm_g, mod, mod, w_in, cs]
    aliases = {}
    if u is not None:
        in_specs = [pl.BlockSpec(memory_space=pl.ANY)] + in_specs
        args = [u] + args
        aliases = {0: 2}
    return pl.pallas_call(
        functools.partial(_even_in_kernel, aliased=u is not None), out_shape=outs, grid=rows.grid,
        in_specs=in_specs,
        out_specs=(rows.rows(D_FOURIER), rows.rows(D_FOURIER), rows.chunk(first_chunk)),
        scratch_shapes=[pltpu.VMEM((rows.tm, LANES), F32)] * SLABS,
        input_output_aliases=aliases,
        compiler_params=_params(1), name="even_in",
    )(*args)


def _fourier_kernel(cl_ref, sl_ref, vc_ref, vs_ref, o_ref, *, scale):
    a = jnp.dot(cl_ref[...], vc_ref[0], preferred_element_type=F32)
    a = a - jnp.dot(sl_ref[...], vs_ref[0], preferred_element_type=F32)
    o_ref[0] = (a * scale).astype(BF16)


def _fourier(vc3, vs3, cl, sl):
    nb, length, _ = vc3.shape
    scale = 1.0 / math.sqrt(length * FOURIER_CH)
    blk = pl.BlockSpec((1, length, D_FOURIER), lambda b: (b, 0, 0))
    return pl.pallas_call(
        functools.partial(_fourier_kernel, scale=scale),
        out_shape=jax.ShapeDtypeStruct((nb, length, D_FOURIER), BF16),
        grid=(nb,),
        in_specs=[_const_spec((length, length)), _const_spec((length, length)), blk, blk],
        out_specs=blk,
        compiler_params=_params(1), name="fourier",
    )(cl, sl, vc3, vs3)


def _dft_mats(n):
    k = lax.iota(jnp.int32, n)
    ang = ((k[:, None] * k[None, :]) % n).astype(F32) * (2.0 * math.pi / n)
    return jnp.cos(ang), jnp.sin(ang)


def _ssm_ops_kernel(ar_ref, ai_ref, ldt_ref, btr_ref, bti_ref, ctr_ref, cti_ref,
                    bm_ref, wh_ref, atr_ref, ati_ref, e_ref):
    t = SSM_CHUNK
    n = SLAB_STATE
    rows = lax.broadcasted_iota(jnp.int32, (LANES, n), 0)
    cols = lax.broadcasted_iota(jnp.int32, (LANES, n), 1)
    same_group = (rows // SSM_GROUP) == (cols // SSM_STATE)
    lag = lax.broadcasted_iota(jnp.int32, (t + 8, n), 0).astype(F32)
    nt_dims = (((1,), (1,)), ((), ()))
    for d in range(2):
        ar, ai = ar_ref[d], ai_ref[d]
        dt = jnp.exp(ldt_ref[d])
        xr, xi = ar * dt, ai * dt
        mag = jnp.exp(lag * xr)
        pr = mag * jnp.cos(lag * xi)
        pi = mag * jnp.sin(lag * xi)
        abr, abi = pr[1:2], pi[1:2]
        den = ar * ar + ai * ai
        fr = ((abr - 1.0) * ar + abi * ai) / den
        fi = (abi * ar - (abr - 1.0) * ai) / den
        br = jnp.where(same_group, btr_ref[d], 0.0)
        bi = jnp.where(same_group, bti_ref[d], 0.0)
        bbr = fr * br - fi * bi
        bbi = fr * bi + fi * br
        cr = jnp.where(same_group, ctr_ref[d], 0.0)
        ci = jnp.where(same_group, cti_ref[d], 0.0)
        for k in range(t):
            pkr, pki = pr[k:k + 1], pi[k:k + 1]
            x_r = bbr * pkr - bbi * pki
            x_i = bbr * pki + bbi * pkr
            s = t - 1 - k if d == 0 else k
            bm_ref[s * LANES:(s + 1) * LANES, 2 * d * n:(2 * d + 1) * n] = x_r.astype(BF16)
            bm_ref[s * LANES:(s + 1) * LANES, (2 * d + 1) * n:(2 * d + 2) * n] = x_i.astype(BF16)
            e_ref[d, k] = (lax.dot_general(x_r, cr, nt_dims, precision=lax.Precision.HIGHEST,
                                           preferred_element_type=F32)
                           - lax.dot_general(x_i, ci, nt_dims, precision=lax.Precision.HIGHEST,
                                             preferred_element_type=F32))
        for k in range(1, t + 1):
            pkr, pki = pr[k:k + 1], pi[k:k + 1]
            w_r = cr * pkr - ci * pki
            w_i = cr * pki + ci * pkr
            tt = k - 1 if d == 0 else t - k
            wh_ref[2 * d * n:(2 * d + 1) * n, tt * LANES:(tt + 1) * LANES] = w_r.T.astype(BF16)
            wh_ref[(2 * d + 1) * n:(2 * d + 2) * n, tt * LANES:(tt + 1) * LANES] = (-w_i).T.astype(BF16)
        atr_ref[d] = pr[t:t + 1]
        ati_ref[d] = pi[t:t + 1]


def _ssm_toeplitz_kernel(e_ref, dsk_ref, m_ref):
    t = SSM_CHUNK
    s = pl.program_id(1)
    r = lax.broadcasted_iota(jnp.int32, (LANES, LANES), 0)
    c = lax.broadcasted_iota(jnp.int32, (LANES, LANES), 1)
    skip = jnp.where(r == c, jnp.broadcast_to(dsk_ref[...], (LANES, LANES)), 0.0)
    for tt in range(t):
        blk = ((tt >= s).astype(F32) * e_ref[0, jnp.maximum(tt - s, 0)]
               + (s >= tt).astype(F32) * e_ref[1, jnp.maximum(s - tt, 0)]
               + (s == tt).astype(F32) * skip)
        m_ref[:, tt * LANES:(tt + 1) * LANES] = blk.astype(BF16)


def _ssm_ops(a_re, a_im, log_dt, b_re, b_im, c_re, c_im, d_skip):
    t = SSM_CHUNK
    n = SLAB_STATE

    def per_state(x):
        return x.astype(F32).reshape(2, SLABS, 1, n)

    def per_channel(x):
        x = x.astype(F32).reshape(2, SLABS, LANES, SSM_STATE)
        return jnp.tile(x, (1, 1, 1, SLAB_GROUPS))

    ldt = jnp.broadcast_to(log_dt[..., None], a_re.shape)
    vec = pl.BlockSpec((2, None, 1, n), lambda z: (0, z, 0, 0))
    mat = pl.BlockSpec((2, None, LANES, n), lambda z: (0, z, 0, 0))
    bm, wh, at_re, at_im, e = pl.pallas_call(
        _ssm_ops_kernel,
        out_shape=(jax.ShapeDtypeStruct((SLABS, t * LANES, 4 * n), BF16),
                   jax.ShapeDtypeStruct((SLABS, 4 * n, t * LANES), BF16),
                   jax.ShapeDtypeStruct((SLABS, 2, 1, n), F32),
                   jax.ShapeDtypeStruct((SLABS, 2, 1, n), F32),
                   jax.ShapeDtypeStruct((SLABS, 2, t, LANES, LANES), F32)),
        grid=(SLABS,),
        in_specs=[vec, vec, vec, mat, mat, mat, mat],
        out_specs=(pl.BlockSpec((None, t * LANES, 4 * n), lambda z: (z, 0, 0)),
                   pl.BlockSpec((None, 4 * n, t * LANES), lambda z: (z, 0, 0)),
                   pl.BlockSpec((None, 2, 1, n), lambda z: (z, 0, 0, 0)),
                   pl.BlockSpec((None, 2, 1, n), lambda z: (z, 0, 0, 0)),
                   pl.BlockSpec((None, 2, t, LANES, LANES), lambda z: (z, 0, 0, 0, 0))),
        compiler_params=_params(1), name="ssm_ops",
    )(per_state(a_re), per_state(a_im), per_state(ldt),
      per_channel(b_re.transpose(0, 1, 3, 2)), per_channel(b_im.transpose(0, 1, 3, 2)),
      per_channel(c_re), per_channel(c_im))
    m = pl.pallas_call(
        _ssm_toeplitz_kernel,
        out_shape=jax.ShapeDtypeStruct((SLABS, t * LANES, t * LANES), BF16),
        grid=(SLABS, t),
        in_specs=[pl.BlockSpec((None, 2, t, LANES, LANES), lambda z, s: (z, 0, 0, 0, 0)),
                  pl.BlockSpec((None, 1, LANES), lambda z, s: (z, 0, 0))],
        out_specs=pl.BlockSpec((None, LANES, t * LANES), lambda z, s: (z, s, 0)),
        compiler_params=_params(2), name="ssm_toeplitz",
    )(e, d_skip.astype(F32).reshape(SLABS, 1, LANES))
    return m, bm, wh, at_re.reshape(SLABS * 2, 1, n), at_im.reshape(SLABS * 2, 1, n)


def _ssm_contrib_kernel(u_ref, bm_ref, o_ref):
    o_ref[...] = jnp.dot(u_ref[...], bm_ref[...], preferred_element_type=F32)


def _ssm_contrib(u, bm):
    slabs, r, k = u.shape
    n = bm.shape[2]
    tr = _tile(r, 576)
    return pl.pallas_call(
        _ssm_contrib_kernel,
        out_shape=jax.ShapeDtypeStruct((slabs, r, n), F32),
        grid=(slabs, r // tr),
        in_specs=[pl.BlockSpec((None, tr, k), lambda s, i: (s, i, 0)),
                  pl.BlockSpec((None, k, n), lambda s, i: (s, 0, 0))],
        out_specs=pl.BlockSpec((None, tr, n), lambda s, i: (s, i, 0)),
        compiler_params=_params(2), name="ssm_contrib",
    )(u, bm)


def _ssm_scan_kernel(ar_ref, ai_ref, cr_ref, ci_ref, hr_ref, hi_ref, *, ncc, nct, rb):
    d = pl.program_id(1)
    lt = ar_ref.shape[-1]
    ncx = nct - ncc
    ar = jnp.broadcast_to(ar_ref[...], (rb, lt))
    ai = jnp.broadcast_to(ai_ref[...], (rb, lt))
    zero = jnp.zeros((rb, lt), F32)

    def advance(i_src, i_dst, hr, hi):
        r0 = pl.multiple_of(i_src * rb, rb)
        r1 = pl.multiple_of(i_dst * rb, rb)
        nr = ar * hr - ai * hi + cr_ref[pl.ds(r0, rb), :]
        ni = ar * hi + ai * hr + ci_ref[pl.ds(r0, rb), :]
        hr_ref[pl.ds(r1, rb), :] = nr
        hi_ref[pl.ds(r1, rb), :] = ni
        return nr, ni

    def start(i):
        hr_ref[pl.ds(i * rb, rb), :] = zero
        hi_ref[pl.ds(i * rb, rb), :] = zero
        return zero, zero

    @pl.when(d == 0)
    def _():
        c = start(ncx)
        c = lax.fori_loop(0, ncc - 1, lambda k, c: advance(ncx + k, ncx + k + 1, *c), c)
        c = advance(nct - 1, 0, *c)
        lax.fori_loop(0, ncx - 1, lambda k, c: advance(k, k + 1, *c), c)

    @pl.when(d == 1)
    def _():
        c = start(nct - 1)
        c = lax.fori_loop(0, ncc - 1, lambda k, c: advance(nct - 1 - k, nct - 2 - k, *c), c)
        c = advance(ncx, ncx - 1, *c)
        lax.fori_loop(0, ncx - 1, lambda k, c: advance(ncx - 1 - k, ncx - 2 - k, *c), c)


def _ssm_scan(contrib, at_re, at_im, ncc, nct, rb):
    slabs, r, n = contrib.shape
    lt = LANES
    q = SLAB_STATE // lt
    re_spec = pl.BlockSpec((None, r, lt), lambda s, d, j: (s, 0, d * 2 * q + j))
    im_spec = pl.BlockSpec((None, r, lt), lambda s, d, j: (s, 0, d * 2 * q + q + j))
    a_spec = pl.BlockSpec((None, 1, lt), lambda s, d, j: (s * 2 + d, 0, j))
    h_spec = pl.BlockSpec((None, r, lt), lambda s, d, j: (s, 0, d * q + j))
    out = jax.ShapeDtypeStruct((slabs, r, n // 2), F32)
    return pl.pallas_call(
        functools.partial(_ssm_scan_kernel, ncc=ncc, nct=nct, rb=rb),
        out_shape=(out, out),
        grid=(slabs, 2, q),
        in_specs=[a_spec, a_spec, re_spec, im_spec],
        out_specs=(h_spec, h_spec),
        compiler_params=_params(3), name="ssm_scan",
    )(at_re, at_im, contrib, contrib)


def _ssm_read_kernel(u_ref, hr_ref, hi_ref, m_ref, wh_ref, o_ref):
    q = SLAB_STATE
    acc = jnp.dot(u_ref[...], m_ref[...], preferred_element_type=F32)
    for d in range(2):
        acc += jnp.dot(hr_ref[:, d * q:(d + 1) * q].astype(BF16), wh_ref[2 * d * q:(2 * d + 1) * q, :],
                       preferred_element_type=F32)
        acc += jnp.dot(hi_ref[:, d * q:(d + 1) * q].astype(BF16), wh_ref[(2 * d + 1) * q:(2 * d + 2) * q, :],
                       preferred_element_type=F32)
    o_ref[...] = acc.astype(BF16)


def _ssm_read(u, h_re, h_im, m, wh):
    slabs, r, k = u.shape
    ns = h_re.shape[2]
    nw = wh.shape[1]
    tr = _tile(r, 576)
    tn = k // 2
    return pl.pallas_call(
        _ssm_read_kernel,
        out_shape=jax.ShapeDtypeStruct((slabs, r, k), BF16),
        grid=(slabs, 2, r // tr),
        in_specs=[pl.BlockSpec((None, tr, k), lambda s, j, i: (s, i, 0)),
                  pl.BlockSpec((None, tr, ns), lambda s, j, i: (s, i, 0)),
                  pl.BlockSpec((None, tr, ns), lambda s, j, i: (s, i, 0)),
                  pl.BlockSpec((None, k, tn), lambda s, j, i: (s, 0, j)),
                  pl.BlockSpec((None, nw, tn), lambda s, j, i: (s, 0, j))],
        out_specs=pl.BlockSpec((None, tr, tn), lambda s, j, i: (s, i, j)),
        compiler_params=_params(3), name="ssm_read",
    )(u, h_re, h_im, m, wh)


MLP_FF_CHUNK = 1024


def _mlp_tail(x, g_ref, sh_ref, sc_ref, gate_ref, w1_ref, w2_ref, o_ref):
    fc = MLP_FF_CHUNK
    h = _norm_mod(x, g_ref[...], sh_ref[...], sc_ref[...]).reshape(-1, D_MODEL).astype(BF16)
    acc = jnp.zeros(h.shape, F32)
    for k in range(D_FF // fc):
        a = jnp.dot(h, w1_ref[:, k * fc:(k + 1) * fc], preferred_element_type=F32)
        a = jnp.square(jnp.maximum(a, 0.0)).astype(BF16)
        acc += jnp.dot(a, w2_ref[k * fc:(k + 1) * fc, :], preferred_element_type=F32)
    o_ref[...] = x + gate_ref[...] * acc.reshape(x.shape)


def _even_tail_kernel(ya_ref, y_ref, x_ref, gate1_ref, wg_ref, bg_ref, wa_ref, wb_ref,
                      g_ref, sh_ref, sc_ref, gate2_ref, w1_ref, w2_ref, o_ref, *ys_scr):
    nb, t_len, _ = x_ref.shape
    for s in range(SLABS):
        for t in range(t_len):
            ys_scr[s][pl.ds(t, nb, stride=t_len), :] = y_ref[s, :, t * LANES:(t + 1) * LANES].astype(F32)
    y = _gelu_tanh(jnp.concatenate([r[...] for r in ys_scr], axis=1))
    glu = y * _sigmoid(jnp.dot(y.astype(BF16), wg_ref[...], preferred_element_type=F32) + bg_ref[...])
    acc = jnp.dot(ya_ref[...].reshape(nb * t_len, D_FOURIER), wa_ref[...], preferred_element_type=F32)
    acc += jnp.dot(glu.astype(BF16), wb_ref[...], preferred_element_type=F32)
    x1 = x_ref[...] + gate1_ref[...] * acc.reshape(nb, t_len, D_MODEL)
    _mlp_tail(x1, g_ref, sh_ref, sc_ref, gate2_ref, w1_ref, w2_ref, o_ref)


def _odd_tail_kernel(a_ref, x_ref, gate1_ref, wo_ref, g_ref, sh_ref, sc_ref, gate2_ref, w1_ref, w2_ref, o_ref):
    a = a_ref[...].reshape(x_ref.shape)
    x1 = x_ref[...] + gate1_ref[...] * jnp.dot(a, wo_ref[...], preferred_element_type=F32)
    _mlp_tail(x1, g_ref, sh_ref, sc_ref, gate2_ref, w1_ref, w2_ref, o_ref)


def _mlp_specs(rows):
    return [_const_spec((1, D_MODEL)), rows.mod(3), rows.mod(4), rows.mod(5),
            _const_spec((D_MODEL, D_FF)), _const_spec((D_FF, D_MODEL))]


def _even_tail(ya, y, first_chunk, x, rows, mod, w_glu, b_glu, w_out, norm_g, w1, w2):
    return pl.pallas_call(
        _even_tail_kernel, out_shape=jax.ShapeDtypeStruct(x.shape, F32), grid=rows.grid,
        in_specs=[rows.rows(D_FOURIER), rows.chunk(first_chunk), rows.rows(D_MODEL), rows.mod(2),
                  _const_spec((D_SSM, D_SSM)), _const_spec((1, D_SSM)),
                  _const_spec((D_FOURIER, D_MODEL)), _const_spec((D_SSM, D_MODEL))] + _mlp_specs(rows),
        out_specs=rows.rows(D_MODEL),
        scratch_shapes=[pltpu.VMEM((rows.tm, LANES), F32)] * SLABS,
        compiler_params=_params(1), name="even_tail",
    )(ya, y, x, mod, w_glu, b_glu, w_out[:D_FOURIER], w_out[D_FOURIER:], norm_g, mod, mod, mod, w1, w2)


def _odd_tail(a, x, rows, mod, w_out, norm_g, w1, w2):
    return pl.pallas_call(
        _odd_tail_kernel, out_shape=jax.ShapeDtypeStruct(x.shape, F32), grid=rows.grid,
        in_specs=[rows.rows3(a.shape[1], D_MODEL), rows.rows(D_MODEL), rows.mod(2),
                  _const_spec((D_MODEL, D_MODEL))] + _mlp_specs(rows),
        out_specs=rows.rows(D_MODEL),
        compiler_params=_params(2), name="odd_tail",
    )(a, x, mod, w_out, norm_g, mod, mod, mod, w1, w2)


KEY_CHUNK = 256
V_ROWS = DA_V_DIM + 16
Q_SCALE = DA_HEAD_DIM ** -0.5 * math.log2(math.e)


def _qkv_kernel(*refs, rope):
    if rope:
        (x_ref, g_ref, sh_ref, sc_ref, w_ref, gqk_ref, ones_ref,
         cos_ref, sa_ref, sb_ref, qt_ref, k_ref, vt_ref) = refs
    else:
        x_ref, g_ref, sh_ref, sc_ref, w_ref, gqk_ref, ones_ref, qt_ref, k_ref, vt_ref = refs
    h = _norm_mod(x_ref[...], g_ref[...], sh_ref[...], sc_ref[...]).astype(BF16)
    y = jnp.dot(h, w_ref[...], preferred_element_type=F32)
    tm = y.shape[0]
    for s in range(2 * DA_HEADS):
        blk = y[:, s * LANES:(s + 1) * LANES]
        ssq = jnp.dot((blk * blk).astype(BF16), ones_ref[...], preferred_element_type=F32)
        nrm = blk * gqk_ref[s // DA_HEADS]
        if rope:
            nrm = (nrm * cos_ref[...] + pltpu.roll(nrm, LANES - 16, 1) * sa_ref[...]
                   + pltpu.roll(nrm, 16, 1) * sb_ref[...])
        nrm = nrm * lax.rsqrt(ssq * (1.0 / DA_HEAD_DIM) + EPS)
        if s < DA_HEADS:
            qt_ref[s] = nrm.T.astype(BF16)
        else:
            hd = s - DA_HEADS
            k_ref[:, hd * LANES:(hd + 1) * LANES] = nrm.astype(BF16)
    ones = jnp.ones((V_ROWS - DA_V_DIM, tm), BF16)
    for hd in range(DA_HEADS):
        v = y[:, (2 * DA_HEADS + hd) * LANES:(2 * DA_HEADS + hd + 1) * LANES]
        vt_ref[hd, :DA_V_DIM, :] = v.T.astype(BF16)
        vt_ref[hd, DA_V_DIM:, :] = ones


def _qkv(x, rows, norm_g, mod, w_in, gqk, ones_bd, rope_tabs):
    nb, rpb, tm = rows.nb, rows.rpb, rows.tm
    assert tm % KEY_CHUNK == 0
    rope = rope_tabs is not None
    in_specs = [rows.rows(D_MODEL), _const_spec((1, D_MODEL)), rows.mod(0), rows.mod(1),
                _const_spec((D_MODEL, 3 * D_MODEL)), _const_spec((2, 1, LANES)),
                _const_spec((LANES, LANES))]
    args = [x, norm_g, mod, mod, w_in, gqk, ones_bd]
    if rope:
        tab = pl.BlockSpec((tm, LANES), lambda b, i: (i, 0))
        in_specs += [tab, tab, tab]
        args += list(rope_tabs)
    out_shape = (jax.ShapeDtypeStruct((nb, DA_HEADS, LANES, rpb), BF16),
                 jax.ShapeDtypeStruct((nb, rpb, D_MODEL), BF16),
                 jax.ShapeDtypeStruct((nb, DA_HEADS, V_ROWS, rpb), BF16))
    out_specs = (pl.BlockSpec((None, DA_HEADS, LANES, tm), lambda b, i: (b, 0, 0, i)),
                 pl.BlockSpec((None, tm, D_MODEL), lambda b, i: (b, i, 0)),
                 pl.BlockSpec((None, DA_HEADS, V_ROWS, tm), lambda b, i: (b, 0, 0, i)))
    return pl.pallas_call(
        functools.partial(_qkv_kernel, rope=rope),
        out_shape=out_shape, grid=rows.grid, in_specs=in_specs, out_specs=out_specs,
        compiler_params=_params(2), name="qkv",
    )(*args)


def _rope_tables(length):
    pos = jnp.arange(length, dtype=jnp.int32)
    row = (pos // GRID_W).astype(F32)
    col = (pos % GRID_W).astype(F32)
    half = DA_HEAD_DIM // 2
    inv_freq = ROPE_THETA ** (-jnp.arange(0, half, 2, dtype=F32) / half)
    lane = jnp.arange(LANES)
    f = lane % 16
    is_col = (lane % DA_HEAD_DIM) >= half
    second = (lane % half) >= 16
    ang = jnp.where(is_col[None, :], col[:, None], row[:, None]) * inv_freq[f][None, :]
    cos, sin = jnp.cos(ang), jnp.sin(ang)
    sin_a = jnp.where(second[None, :], 0.0, -sin)
    sin_b = jnp.where(second[None, :], sin, 0.0)
    return cos, sin_a, sin_b


def _attn_kernel(*refs, n_kv, lam_init, tq):
    lp_ref, gh_ref, qt_ref = refs[:3]
    kv_refs = refs[3:3 + 2 * n_kv]
    o_ref = refs[3 + 2 * n_kv]
    bufs = refs[4 + 2 * n_kv:]
    lp = lp_ref[...]
    lam = (jnp.exp(jnp.sum(lp[0:1] * lp[1:2], axis=-1, keepdims=True))
           - jnp.exp(jnp.sum(lp[2:3] * lp[3:4], axis=-1, keepdims=True)) + lam_init)
    nq = qt_ref.shape[1] // tq
    kc = KEY_CHUNK

    key_chunks = []
    for i in range(n_kv):
        for c in range(kv_refs[2 * i].shape[0] // kc):
            key_chunks.append((kv_refs[2 * i], kv_refs[2 * i + 1], c))

    def masked_q(j):
        qt = qt_ref[:, j * tq:(j + 1) * tq]
        row = lax.broadcasted_iota(jnp.int32, qt.shape, 0)
        zero = jnp.zeros_like(qt)
        return jnp.concatenate([jnp.where(row < DA_HEAD_DIM, qt, zero),
                                jnp.where(row >= DA_HEAD_DIM, qt, zero)], axis=1)

    def score_chunk(n, qst, s_scr, m):
        k_ref, _, c = key_chunks[n]
        s = jnp.dot(k_ref[c * kc:(c + 1) * kc, :], qst, preferred_element_type=F32)
        s_scr[n * kc:(n + 1) * kc, :] = s
        return jnp.maximum(m, s.reshape(kc // 8, 8, 2 * tq).max(axis=0))

    def value_chunk(n, s_scr, m, acc):
        _, vt_ref, c = key_chunks[n]
        r = pl.multiple_of(n * kc + jnp.minimum(pl.program_id(0), 0) * kc, kc)
        p = jnp.exp2((s_scr[pl.ds(r, kc), :] - m).astype(BF16))
        return acc + jnp.dot(vt_ref[:, c * kc:(c + 1) * kc], p, preferred_element_type=F32)

    def finish(j, acc):
        o2 = acc[:DA_V_DIM] / acc[DA_V_DIM:DA_V_DIM + 1]
        o = o2[:, :tq] - lam * o2[:, tq:]
        o = o * lax.rsqrt(jnp.mean(o * o, axis=0, keepdims=True) + EPS) * gh_ref[...]
        o_ref[j * tq:(j + 1) * tq, :] = (o * (1.0 - lam_init)).T.astype(BF16)

    neg = jnp.full((8, 2 * tq), -jnp.inf, F32)
    qst = masked_q(0)
    m8 = neg
    for n in range(len(key_chunks)):
        m8 = score_chunk(n, qst, bufs[0], m8)
    for j in range(nq):
        m = m8.max(axis=0, keepdims=True)
        acc = jnp.zeros((V_ROWS, 2 * tq), F32)
        if j + 1 < nq:
            qst = masked_q(j + 1)
            m8 = neg
        for n in range(len(key_chunks)):
            if j + 1 < nq:
                m8 = score_chunk(n, qst, bufs[(j + 1) % 2], m8)
            acc = value_chunk(n, bufs[j % 2], m, acc)
        finish(j, acc)


def _attention(qt, kv_list, lam_p, g_head, lam_init, tq):
    nb, _, _, lq = qt.shape
    in_specs = [_const_spec((4, DA_HEAD_DIM)), _const_spec((DA_V_DIM, 1)),
                pl.BlockSpec((None, None, LANES, lq), lambda b, h: (b, h, 0, 0))]
    args = [lam_p, g_head, qt]
    total = 0
    for k, vt in kv_list:
        lk = k.shape[1]
        total += lk
        in_specs.append(pl.BlockSpec((None, lk, LANES), lambda b, h: (b, 0, h)))
        in_specs.append(pl.BlockSpec((None, None, V_ROWS, lk), lambda b, h: (b, h, 0, 0)))
        args += [k, vt]
    n_bufs = min(2, lq // tq)
    return pl.pallas_call(
        functools.partial(_attn_kernel, n_kv=len(kv_list), lam_init=lam_init, tq=tq),
        out_shape=jax.ShapeDtypeStruct((nb, lq, D_MODEL), BF16),
        grid=(nb, DA_HEADS),
        in_specs=in_specs,
        out_specs=pl.BlockSpec((None, lq, LANES), lambda b, h: (b, 0, h)),
        scratch_shapes=[pltpu.VMEM((total, 2 * tq), F32)] * n_bufs,
        compiler_params=_params(2), name="diff_attn",
    )(*args)


def kernel(x, c, ctx, c_ctx, norm1_g, norm2_g, ada_w, ada_b, mlp_w1, mlp_w2, ev_w_in, ev_w_out, ssm_a_re, ssm_a_im, ssm_log_dt, ssm_b_re, ssm_b_im, ssm_c_re, ssm_c_im, ssm_d, ssm_w_glu, ssm_b_glu, od_w_in, od_w_out, od_q_norm, od_k_norm, od_lambda, od_head_norm):
    nb, length, d = x.shape
    clen = ctx.shape[1]
    assert d == D_MODEL and length % GRID_W == 0
    assert length % SSM_CHUNK == 0 and clen % SSM_CHUNK == 0

    xs = x.reshape(nb * length, d)
    cs = ctx.reshape(nb * clen, d)
    rows_x = _Rows(nb, length, _tile(length, 512), None)
    rows_c = _Rows(1, nb * clen, _tile(nb * clen, 512), nb)
    rows_cb = _Rows(nb, clen, clen, nb)
    chunks_x = _ChunkRows(nb, length // SSM_CHUNK, None)
    chunks_c = _ChunkRows(nb, clen // SSM_CHUNK, nb)

    n_cond = nb + 1
    pad = (-n_cond) % 8
    cond = jnp.concatenate([c, c_ctx[None, :], jnp.zeros((pad, d), F32)], axis=0)
    mods = _adaln(cond, ada_w, ada_b)[:, :n_cond].reshape(DEPTH, n_cond, 1, 6 * d)

    kf = lax.iota(jnp.int32, FOURIER_CH)
    ang = ((kf[:, None] * kf[None, :]) % FOURIER_CH).astype(F32) * (2.0 * math.pi / FOURIER_CH)
    cs_ch = jnp.concatenate([jnp.cos(ang), jnp.sin(ang)], axis=1).astype(BF16)
    cl_x, sl_x = (m.astype(BF16) for m in _dft_mats(length))
    cl_c, sl_c = (m.astype(BF16) for m in _dft_mats(clen))
    rope_tabs = _rope_tables(length)
    lane = jnp.arange(LANES)
    ones_bd = (lane[:, None] // DA_HEAD_DIM == lane[None, :] // DA_HEAD_DIM).astype(BF16)

    ncc = clen // SSM_CHUNK
    ncx = length // SSM_CHUNK
    nct = ncx + ncc

    for i in range(DEPTH):
        last = i == DEPTH - 1
        j = i // 2
        mod = mods[i]
        n1 = norm1_g[i].reshape(1, d)
        n2 = norm2_g[i].reshape(1, d)
        if i % 2 == 0:
            w_in = ev_w_in[j].astype(BF16)
            u_shape = (SLABS, nct, nb, SSM_CHUNK * LANES)
            x3 = xs.reshape(nb, length, d)
            c3 = cs.reshape(nb, clen, d)
            vc_x, vs_x, u = _even_in(x3, chunks_x, n1, mod, w_in, cs_ch, u_shape, 0)
            vc_c, vs_c, u = _even_in(c3, chunks_c, n1, mod, w_in, cs_ch, u_shape, ncx, u)
            ya_x = _fourier(vc_x, vs_x, cl_x, sl_x)
            m, bm, wh, at_re, at_im = _ssm_ops(ssm_a_re[j], ssm_a_im[j], ssm_log_dt[j], ssm_b_re[j],
                                                ssm_b_im[j], ssm_c_re[j], ssm_c_im[j], ssm_d[j])
            u = u.reshape(SLABS, nct * nb, SSM_CHUNK * LANES)
            contrib = _ssm_contrib(u, bm)
            h_re, h_im = _ssm_scan(contrib, at_re, at_im, ncc, nct, nb)
            y = _ssm_read(u, h_re, h_im, m, wh).reshape(u_shape)
            w_glu = ssm_w_glu[j].astype(BF16)
            b_glu = ssm_b_glu[j].reshape(1, D_SSM)
            w_out = ev_w_out[j].astype(BF16)
            w1 = mlp_w1[i].astype(BF16)
            w2 = mlp_w2[i].astype(BF16)
            xs = _even_tail(ya_x, y, 0, x3, chunks_x, mod, w_glu, b_glu, w_out, n2, w1, w2).reshape(nb * length, d)
            if not last:
                ya_c = _fourier(vc_c, vs_c, cl_c, sl_c)
                cs = _even_tail(ya_c, y, ncx, c3, chunks_c, mod, w_glu, b_glu, w_out, n2, w1, w2).reshape(nb * clen, d)
        else:
            lam_init = 0.8 - 0.6 * math.exp(-0.3 * i)
            w_in = od_w_in[j].astype(BF16)
            gqk = jnp.stack([jnp.tile(od_q_norm[j], 2) * Q_SCALE,
                             jnp.tile(od_k_norm[j], 2)]).reshape(2, 1, LANES)
            qt_x, k_x, vt_x = _qkv(xs, rows_x, n1, mod, w_in, gqk, ones_bd, rope_tabs)
            qt_c, k_c, vt_c = _qkv(cs, rows_cb, n1, mod, w_in, gqk, ones_bd, None)
            g_head = od_head_norm[j].reshape(DA_V_DIM, 1)
            w_out = od_w_out[j].astype(BF16)
            o_x = _attention(qt_x, [(k_x, vt_x), (k_c, vt_c)], od_lambda[j], g_head, lam_init,
                             _tile(length, 512, LANES))
            w1 = mlp_w1[i].astype(BF16)
            w2 = mlp_w2[i].astype(BF16)
            xs = _odd_tail(o_x, xs, rows_x, mod, w_out, n2, w1, w2)
            if not last:
                o_c = _attention(qt_c, [(k_c, vt_c)], od_lambda[j], g_head, lam_init, clen)
                cs = _odd_tail(o_c, cs, rows_c, mod, w_out, n2, w1, w2)
    return xs.reshape(nb, length, d)
```

```python
import functools
import math

import jax
import jax.numpy as jnp
from jax import lax
from jax.experimental import pallas as pl
from jax.experimental.pallas import tpu as pltpu

F32 = jnp.float32
BF16 = jnp.bfloat16

D_MODEL = 1024
DEPTH = 4
GRID_W = 64
EPS = 1e-6
D_FF = 4 * D_MODEL
D_FOURIER = D_MODEL // 2
FOURIER_GROUPS = 4
FOURIER_CH = D_FOURIER // FOURIER_GROUPS
D_SSM = D_MODEL - D_FOURIER
SSM_GROUP = 16
SSM_GROUPS = D_SSM // SSM_GROUP
SSM_STATE = 64
DA_HEAD_DIM = 64
DA_V_DIM = 2 * DA_HEAD_DIM
DA_HEADS = D_MODEL // DA_V_DIM
ROPE_THETA = 10000.0

LANES = 128
SSM_CHUNK = 16
SLABS = D_SSM // LANES
SLAB_GROUPS = LANES // SSM_GROUP
SLAB_STATE = SLAB_GROUPS * SSM_STATE
VMEM_LIMIT = 56 * 1024 * 1024


def _params(n_axes, flags=None):
    return pltpu.CompilerParams(dimension_semantics=("arbitrary",) * n_axes,
                                vmem_limit_bytes=VMEM_LIMIT, flags=flags)


def _tile(n, target, mult=8):
    t = min(n, target)
    while t > 1 and (n % t or t % mult):
        t -= 1
    return t if n % t == 0 else n


def _const_spec(shape):
    nd = len(shape)
    return pl.BlockSpec(shape, lambda *_: (0,) * nd, pipeline_mode=pl.Buffered(1))


def _sigmoid(x):
    return 1.0 / (1.0 + jnp.exp(-x))


def _gelu_tanh(x):
    c = math.sqrt(2.0 / math.pi)
    return x * (0.5 * (1.0 + jnp.tanh(c * (x + 0.044715 * (x * x * x)))))


def _norm_mod(x, g, shift, scale):
    ms = jnp.mean(x * x, axis=-1, keepdims=True)
    y = x * lax.rsqrt(ms + EPS) * g
    return y * (1.0 + scale) + shift


def _adaln_kernel(c_ref, w_ref, b_ref, o_ref):
    c = c_ref[...]
    s = (c * _sigmoid(c)).astype(BF16)
    o_ref[0] = jnp.dot(s, w_ref[0].astype(BF16), preferred_element_type=F32) + b_ref[0]


def _adaln(cond, ada_w, ada_b):
    depth, d, n = ada_w.shape
    r = cond.shape[0]
    tn = _tile(n, 1536, LANES)
    return pl.pallas_call(
        _adaln_kernel,
        out_shape=jax.ShapeDtypeStruct((depth, r, n), F32),
        grid=(depth, n // tn),
        in_specs=[pl.BlockSpec((r, d), lambda l, j: (0, 0)),
                  pl.BlockSpec((1, d, tn), lambda l, j: (l, 0, j)),
                  pl.BlockSpec((1, 1, tn), lambda l, j: (l, 0, j))],
        out_specs=pl.BlockSpec((1, r, tn), lambda l, j: (l, 0, j)),
        compiler_params=_params(2), name="adaln",
    )(cond, ada_w, ada_b.reshape(depth, 1, n))


class _Rows:
    def __init__(self, nb, rpb, tm, mod_row):
        self.nb, self.rpb, self.tm = nb, rpb, tm
        self.nt = rpb // tm
        self.grid = (nb, self.nt)
        self.mod_row = mod_row

    def rows(self, cols):
        nt = self.nt
        return pl.BlockSpec((self.tm, cols), lambda b, i: (b * nt + i, 0))

    def rows3(self, per, cols):
        nt, tm = self.nt, self.tm
        if tm <= per:
            k = per // tm
            return pl.BlockSpec((None, tm, cols), lambda b, i: ((b * nt + i) // k, (b * nt + i) % k, 0))
        return pl.BlockSpec((tm // per, per, cols), lambda b, i: (b * nt + i, 0, 0))

    def mod(self, k):
        if self.mod_row is None:
            return pl.BlockSpec((None, 1, D_MODEL), lambda b, i: (b, 0, k))
        r = self.mod_row
        return pl.BlockSpec((None, 1, D_MODEL), lambda b, i: (r, 0, k))


class _ChunkRows:
    def __init__(self, nb, n_chunks, mod_row):
        self.nb, self.grid, self.mod_row = nb, (n_chunks,), mod_row
        self.tm = nb * SSM_CHUNK

    def rows(self, cols):
        return pl.BlockSpec((self.nb, SSM_CHUNK, cols), lambda c: (0, c, 0))

    def mod(self, k):
        if self.mod_row is None:
            return pl.BlockSpec((self.nb, 1, D_MODEL), lambda c: (0, 0, k))
        r = self.mod_row
        return pl.BlockSpec((1, 1, D_MODEL), lambda c: (r, 0, k))

    def chunk(self, first_chunk):
        return pl.BlockSpec((SLABS, None, self.nb, SSM_CHUNK * LANES), lambda c: (0, first_chunk + c, 0, 0))


def _even_in_kernel(*refs, aliased):
    if aliased:
        refs = refs[1:]
    x_ref, g_ref, sh_ref, sc_ref, w_ref, cs_ref, vc_ref, vs_ref, u_ref = refs[:9]
    zs_scr = refs[9:]
    nb, t_len, _ = x_ref.shape
    h = _norm_mod(x_ref[...], g_ref[...], sh_ref[...], sc_ref[...])
    z = jnp.dot(h.reshape(nb * t_len, D_MODEL).astype(BF16), w_ref[...], preferred_element_type=F32)
    zf = z[:, :D_FOURIER].astype(BF16)
    vc, vs = [], []
    for gi in range(FOURIER_GROUPS):
        v = jnp.dot(zf[:, gi * FOURIER_CH:(gi + 1) * FOURIER_CH], cs_ref[...], preferred_element_type=F32)
        vc.append(v[:, :FOURIER_CH].astype(BF16))
        vs.append(v[:, FOURIER_CH:].astype(BF16))
    vc_ref[...] = jnp.concatenate(vc, axis=1).reshape(nb, t_len, D_FOURIER)
    vs_ref[...] = jnp.concatenate(vs, axis=1).reshape(nb, t_len, D_FOURIER)
    for s in range(SLABS):
        zs_scr[s][...] = z[:, D_FOURIER + s * LANES:D_FOURIER + (s + 1) * LANES]
        for t in range(t_len):
            u_ref[s, :, t * LANES:(t + 1) * LANES] = zs_scr[s][pl.ds(t, nb, stride=t_len), :].astype(BF16)


def _even_in(x, rows, norm_g, mod, w_in, cs, u_shape, first_chunk, u=None):
    nb, n_tok, _ = x.shape
    outs = (jax.ShapeDtypeStruct((nb, n_tok, D_FOURIER), BF16), jax.ShapeDtypeStruct((nb, n_tok, D_FOURIER), BF16),
            jax.ShapeDtypeStruct(u_shape, BF16))
    in_specs = [rows.rows(D_MODEL), _const_spec((1, D_MODEL)), rows.mod(0), rows.mod(1),
                _const_spec((D_MODEL, D_MODEL)), _const_spec((FOURIER_CH, 2 * FOURIER_CH))]
    args = [x, norm_g, mod, mod, w_in, cs]
    aliases = {}
    if u is not None:
        in_specs = [pl.BlockSpec(memory_space=pl.ANY)] + in_specs
        args = [u] + args
        aliases = {0: 2}
    return pl.pallas_call(
        functools.partial(_even_in_kernel, aliased=u is not None), out_shape=outs, grid=rows.grid,
        in_specs=in_specs,
        out_specs=(rows.rows(D_FOURIER), rows.rows(D_FOURIER), rows.chunk(first_chunk)),
        scratch_shapes=[pltpu.VMEM((rows.tm, LANES), F32)] * SLABS,
        input_output_aliases=aliases,
        compiler_params=_params(1), name="even_in",
    )(*args)


DFT_PAD = 16


def _fourier_kernel(cl_ref, sl_ref, rev_ref, vc_ref, vs_ref, o_ref, *, scale):
    half = rev_ref.shape[0]
    a = jnp.dot(cl_ref[...], vc_ref[0], preferred_element_type=F32)
    b = jnp.dot(sl_ref[...], vs_ref[0], preferred_element_type=F32)
    o_ref[0, :half, :] = ((a[:half] - b[:half]) * scale).astype(BF16)
    mirrored = ((a + b) * scale).astype(BF16)
    o_ref[0, half:, :] = jnp.dot(rev_ref[...], mirrored, preferred_element_type=F32).astype(BF16)


def _fourier(vc3, vs3, mats):
    cl, sl, rev = mats
    nb, length, _ = vc3.shape
    scale = 1.0 / math.sqrt(length * FOURIER_CH)
    blk = pl.BlockSpec((1, length, D_FOURIER), lambda b: (b, 0, 0))
    return pl.pallas_call(
        functools.partial(_fourier_kernel, scale=scale),
        out_shape=jax.ShapeDtypeStruct((nb, length, D_FOURIER), BF16),
        grid=(nb,),
        in_specs=[_const_spec(cl.shape), _const_spec(sl.shape), _const_spec(rev.shape), blk, blk],
        out_specs=blk,
        compiler_params=_params(1), name="fourier",
    )(cl, sl, rev, vc3, vs3)


def _dft_mats(n):
    half = n // 2
    k = lax.iota(jnp.int32, half + DFT_PAD)
    t = lax.iota(jnp.int32, n)
    ang = ((k[:, None] * t[None, :]) % n).astype(F32) * (2.0 * math.pi / n)
    cl = jnp.where((k <= half)[:, None], jnp.cos(ang), 0.0)
    sl = jnp.where((k < half)[:, None], jnp.sin(ang), 0.0)
    j = lax.iota(jnp.int32, half)
    rev = k[None, :] == (half - j)[:, None]
    return cl.astype(BF16), sl.astype(BF16), rev.astype(BF16)


def _ssm_ops_kernel(ar_ref, ai_ref, ldt_ref, btr_ref, bti_ref, ctr_ref, cti_ref,
                    bm_ref, wh_ref, atr_ref, ati_ref, e_ref):
    t = SSM_CHUNK
    n = SLAB_STATE
    rows = lax.broadcasted_iota(jnp.int32, (LANES, n), 0)
    cols = lax.broadcasted_iota(jnp.int32, (LANES, n), 1)
    same_group = (rows // SSM_GROUP) == (cols // SSM_STATE)
    lag = lax.broadcasted_iota(jnp.int32, (t + 8, n), 0).astype(F32)
    nt_dims = (((1,), (1,)), ((), ()))
    for d in range(2):
        ar, ai = ar_ref[d], ai_ref[d]
        dt = jnp.exp(ldt_ref[d])
        xr, xi = ar * dt, ai * dt
        mag = jnp.exp(lag * xr)
        pr = mag * jnp.cos(lag * xi)
        pi = mag * jnp.sin(lag * xi)
        abr, abi = pr[1:2], pi[1:2]
        den = ar * ar + ai * ai
        fr = ((abr - 1.0) * ar + abi * ai) / den
        fi = (abi * ar - (abr - 1.0) * ai) / den
        br = jnp.where(same_group, btr_ref[d], 0.0)
        bi = jnp.where(same_group, bti_ref[d], 0.0)
        bbr = fr * br - fi * bi
        bbi = fr * bi + fi * br
        cr = jnp.where(same_group, ctr_ref[d], 0.0)
        ci = jnp.where(same_group, cti_ref[d], 0.0)
        for k in range(t):
            pkr, pki = pr[k:k + 1], pi[k:k + 1]
            x_r = bbr * pkr - bbi * pki
            x_i = bbr * pki + bbi * pkr
            s = t - 1 - k if d == 0 else k
            bm_ref[s * LANES:(s + 1) * LANES, 2 * d * n:(2 * d + 1) * n] = x_r.astype(BF16)
            bm_ref[s * LANES:(s + 1) * LANES, (2 * d + 1) * n:(2 * d + 2) * n] = x_i.astype(BF16)
            e_ref[d, k] = (lax.dot_general(x_r, cr, nt_dims, precision=lax.Precision.HIGHEST,
                                           preferred_element_type=F32)
                           - lax.dot_general(x_i, ci, nt_dims, precision=lax.Precision.HIGHEST,
                                             preferred_element_type=F32))
        for k in range(1, t + 1):
            pkr, pki = pr[k:k + 1], pi[k:k + 1]
            w_r = cr * pkr - ci * pki
            w_i = cr * pki + ci * pkr
            tt = k - 1 if d == 0 else t - k
            wh_ref[2 * d * n:(2 * d + 1) * n, tt * LANES:(tt + 1) * LANES] = w_r.T.astype(BF16)
            wh_ref[(2 * d + 1) * n:(2 * d + 2) * n, tt * LANES:(tt + 1) * LANES] = (-w_i).T.astype(BF16)
        atr_ref[d] = pr[t:t + 1]
        ati_ref[d] = pi[t:t + 1]


def _ssm_toeplitz_kernel(e_ref, dsk_ref, m_ref):
    t = SSM_CHUNK
    s = pl.program_id(1)
    r = lax.broadcasted_iota(jnp.int32, (LANES, LANES), 0)
    c = lax.broadcasted_iota(jnp.int32, (LANES, LANES), 1)
    skip = jnp.where(r == c, jnp.broadcast_to(dsk_ref[...], (LANES, LANES)), 0.0)
    for tt in range(t):
        blk = ((tt >= s).astype(F32) * e_ref[0, jnp.maximum(tt - s, 0)]
               + (s >= tt).astype(F32) * e_ref[1, jnp.maximum(s - tt, 0)]
               + (s == tt).astype(F32) * skip)
        m_ref[:, tt * LANES:(tt + 1) * LANES] = blk.astype(BF16)


def _ssm_ops(a_re, a_im, log_dt, b_re, b_im, c_re, c_im, d_skip):
    t = SSM_CHUNK
    n = SLAB_STATE

    def per_state(x):
        return x.astype(F32).reshape(2, SLABS, 1, n)

    def per_channel(x):
        x = x.astype(F32).reshape(2, SLABS, LANES, SSM_STATE)
        return jnp.tile(x, (1, 1, 1, SLAB_GROUPS))

    ldt = jnp.broadcast_to(log_dt[..., None], a_re.shape)
    vec = pl.BlockSpec((2, None, 1, n), lambda z: (0, z, 0, 0))
    mat = pl.BlockSpec((2, None, LANES, n), lambda z: (0, z, 0, 0))
    bm, wh, at_re, at_im, e = pl.pallas_call(
        _ssm_ops_kernel,
        out_shape=(jax.ShapeDtypeStruct((SLABS, t * LANES, 4 * n), BF16),
                   jax.ShapeDtypeStruct((SLABS, 4 * n, t * LANES), BF16),
                   jax.ShapeDtypeStruct((SLABS, 2, 1, n), F32),
                   jax.ShapeDtypeStruct((SLABS, 2, 1, n), F32),
                   jax.ShapeDtypeStruct((SLABS, 2, t, LANES, LANES), F32)),
        grid=(SLABS,),
        in_specs=[vec, vec, vec, mat, mat, mat, mat],
        out_specs=(pl.BlockSpec((None, t * LANES, 4 * n), lambda z: (z, 0, 0)),
                   pl.BlockSpec((None, 4 * n, t * LANES), lambda z: (z, 0, 0)),
                   pl.BlockSpec((None, 2, 1, n), lambda z: (z, 0, 0, 0)),
                   pl.BlockSpec((None, 2, 1, n), lambda z: (z, 0, 0, 0)),
                   pl.BlockSpec((None, 2, t, LANES, LANES), lambda z: (z, 0, 0, 0, 0))),
        compiler_params=_params(1), name="ssm_ops",
    )(per_state(a_re), per_state(a_im), per_state(ldt),
      per_channel(b_re.transpose(0, 1, 3, 2)), per_channel(b_im.transpose(0, 1, 3, 2)),
      per_channel(c_re), per_channel(c_im))
    m = pl.pallas_call(
        _ssm_toeplitz_kernel,
        out_shape=jax.ShapeDtypeStruct((SLABS, t * LANES, t * LANES), BF16),
        grid=(SLABS, t),
        in_specs=[pl.BlockSpec((None, 2, t, LANES, LANES), lambda z, s: (z, 0, 0, 0, 0)),
                  pl.BlockSpec((None, 1, LANES), lambda z, s: (z, 0, 0))],
        out_specs=pl.BlockSpec((None, LANES, t * LANES), lambda z, s: (z, s, 0)),
        compiler_params=_params(2), name="ssm_toeplitz",
    )(e, d_skip.astype(F32).reshape(SLABS, 1, LANES))
    return m, bm, wh, at_re.reshape(SLABS * 2, 1, n), at_im.reshape(SLABS * 2, 1, n)


def _ssm_contrib_kernel(u_ref, bm_ref, o_ref):
    o_ref[...] = jnp.dot(u_ref[...], bm_ref[...], preferred_element_type=F32)


def _ssm_contrib(u, bm):
    slabs, r, k = u.shape
    n = bm.shape[2]
    tr = _tile(r, 576)
    return pl.pallas_call(
        _ssm_contrib_kernel,
        out_shape=jax.ShapeDtypeStruct((slabs, r, n), F32),
        grid=(slabs, r // tr),
        in_specs=[pl.BlockSpec((None, tr, k), lambda s, i: (s, i, 0)),
                  pl.BlockSpec((None, k, n), lambda s, i: (s, 0, 0))],
        out_specs=pl.BlockSpec((None, tr, n), lambda s, i: (s, i, 0)),
        compiler_params=_params(2), name="ssm_contrib",
    )(u, bm)


def _ssm_scan_kernel(ar_ref, ai_ref, cr_ref, ci_ref, hr_ref, hi_ref, *, ncc, nct, rb):
    d = pl.program_id(1)
    lt = ar_ref.shape[-1]
    ncx = nct - ncc
    ar = jnp.broadcast_to(ar_ref[...], (rb, lt))
    ai = jnp.broadcast_to(ai_ref[...], (rb, lt))
    zero = jnp.zeros((rb, lt), F32)

    def advance(i_src, i_dst, hr, hi):
        r0 = pl.multiple_of(i_src * rb, rb)
        r1 = pl.multiple_of(i_dst * rb, rb)
        nr = ar * hr - ai * hi + cr_ref[pl.ds(r0, rb), :]
        ni = ar * hi + ai * hr + ci_ref[pl.ds(r0, rb), :]
        hr_ref[pl.ds(r1, rb), :] = nr
        hi_ref[pl.ds(r1, rb), :] = ni
        return nr, ni

    def start(i):
        hr_ref[pl.ds(i * rb, rb), :] = zero
        hi_ref[pl.ds(i * rb, rb), :] = zero
        return zero, zero

    @pl.when(d == 0)
    def _():
        c = start(ncx)
        c = lax.fori_loop(0, ncc - 1, lambda k, c: advance(ncx + k, ncx + k + 1, *c), c)
        c = advance(nct - 1, 0, *c)
        lax.fori_loop(0, ncx - 1, lambda k, c: advance(k, k + 1, *c), c)

    @pl.when(d == 1)
    def _():
        c = start(nct - 1)
        c = lax.fori_loop(0, ncc - 1, lambda k, c: advance(nct - 1 - k, nct - 2 - k, *c), c)
        c = advance(ncx, ncx - 1, *c)
        lax.fori_loop(0, ncx - 1, lambda k, c: advance(ncx - 1 - k, ncx - 2 - k, *c), c)


def _ssm_scan(contrib, at_re, at_im, ncc, nct, rb):
    slabs, r, n = contrib.shape
    lt = 2 * LANES
    q = SLAB_STATE // lt
    re_spec = pl.BlockSpec((None, r, lt), lambda s, d, j: (s, 0, d * 2 * q + j))
    im_spec = pl.BlockSpec((None, r, lt), lambda s, d, j: (s, 0, d * 2 * q + q + j))
    a_spec = pl.BlockSpec((None, 1, lt), lambda s, d, j: (s * 2 + d, 0, j))
    h_spec = pl.BlockSpec((None, r, lt), lambda s, d, j: (s, 0, d * q + j))
    out = jax.ShapeDtypeStruct((slabs, r, n // 2), F32)
    return pl.pallas_call(
        functools.partial(_ssm_scan_kernel, ncc=ncc, nct=nct, rb=rb),
        out_shape=(out, out),
        grid=(slabs, 2, q),
        in_specs=[a_spec, a_spec, re_spec, im_spec],
        out_specs=(h_spec, h_spec),
        compiler_params=_params(3), name="ssm_scan",
    )(at_re, at_im, contrib, contrib)


def _ssm_read_kernel(u_ref, hr_ref, hi_ref, m_ref, wh_ref, o_ref):
    q = SLAB_STATE
    acc = jnp.dot(u_ref[...], m_ref[...], preferred_element_type=F32)
    for d in range(2):
        acc += jnp.dot(hr_ref[:, d * q:(d + 1) * q].astype(BF16), wh_ref[2 * d * q:(2 * d + 1) * q, :],
                       preferred_element_type=F32)
        acc += jnp.dot(hi_ref[:, d * q:(d + 1) * q].astype(BF16), wh_ref[(2 * d + 1) * q:(2 * d + 2) * q, :],
                       preferred_element_type=F32)
    o_ref[...] = acc.astype(BF16)


def _ssm_read(u, h_re, h_im, m, wh):
    slabs, r, k = u.shape
    ns = h_re.shape[2]
    nw = wh.shape[1]
    tr = _tile(r, 576)
    tn = k // 2
    return pl.pallas_call(
        _ssm_read_kernel,
        out_shape=jax.ShapeDtypeStruct((slabs, r, k), BF16),
        grid=(slabs, 2, r // tr),
        in_specs=[pl.BlockSpec((None, tr, k), lambda s, j, i: (s, i, 0)),
                  pl.BlockSpec((None, tr, ns), lambda s, j, i: (s, i, 0)),
                  pl.BlockSpec((None, tr, ns), lambda s, j, i: (s, i, 0)),
                  pl.BlockSpec((None, k, tn), lambda s, j, i: (s, 0, j)),
                  pl.BlockSpec((None, nw, tn), lambda s, j, i: (s, 0, j))],
        out_specs=pl.BlockSpec((None, tr, tn), lambda s, j, i: (s, i, j)),
        compiler_params=_params(3), name="ssm_read",
    )(u, h_re, h_im, m, wh)


MLP_FF_CHUNK = 1024


def _mlp_tail(x, g_ref, sh_ref, sc_ref, gate_ref, w1_ref, w2_ref, o_ref):
    fc = MLP_FF_CHUNK
    h = _norm_mod(x, g_ref[...], sh_ref[...], sc_ref[...]).reshape(-1, D_MODEL).astype(BF16)
    acc = jnp.zeros(h.shape, F32)
    for k in range(D_FF // fc):
        a = jnp.dot(h, w1_ref[:, k * fc:(k + 1) * fc], preferred_element_type=F32)
        a = jnp.square(jnp.maximum(a, 0.0)).astype(BF16)
        acc += jnp.dot(a, w2_ref[k * fc:(k + 1) * fc, :], preferred_element_type=F32)
    o_ref[...] = x + gate_ref[...] * acc.reshape(x.shape)


def _even_tail_kernel(ya_ref, y_ref, x_ref, gate1_ref, wg_ref, bg_ref, wa_ref, wb_ref,
                      g_ref, sh_ref, sc_ref, gate2_ref, w1_ref, w2_ref, o_ref, *ys_scr):
    nb, t_len, _ = x_ref.shape
    for s in range(SLABS):
        for t in range(t_len):
            ys_scr[s][pl.ds(t, nb, stride=t_len), :] = y_ref[s, :, t * LANES:(t + 1) * LANES].astype(F32)
    y = _gelu_tanh(jnp.concatenate([r[...] for r in ys_scr], axis=1))
    glu = y * _sigmoid(jnp.dot(y.astype(BF16), wg_ref[...], preferred_element_type=F32) + bg_ref[...])
    acc = jnp.dot(ya_ref[...].reshape(nb * t_len, D_FOURIER), wa_ref[...], preferred_element_type=F32)
    acc += jnp.dot(glu.astype(BF16), wb_ref[...], preferred_element_type=F32)
    x1 = x_ref[...] + gate1_ref[...] * acc.reshape(nb, t_len, D_MODEL)
    _mlp_tail(x1, g_ref, sh_ref, sc_ref, gate2_ref, w1_ref, w2_ref, o_ref)


def _odd_tail_kernel(a_ref, x_ref, gate1_ref, wo_ref, g_ref, sh_ref, sc_ref, gate2_ref, w1_ref, w2_ref, o_ref):
    a = a_ref[...].reshape(x_ref.shape)
    x1 = x_ref[...] + gate1_ref[...] * jnp.dot(a, wo_ref[...], preferred_element_type=F32)
    _mlp_tail(x1, g_ref, sh_ref, sc_ref, gate2_ref, w1_ref, w2_ref, o_ref)


def _mlp_specs(rows):
    return [_const_spec((1, D_MODEL)), rows.mod(3), rows.mod(4), rows.mod(5),
            _const_spec((D_MODEL, D_FF)), _const_spec((D_FF, D_MODEL))]


def _even_tail(ya, y, first_chunk, x, rows, mod, w_glu, b_glu, w_out, norm_g, w1, w2):
    return pl.pallas_call(
        _even_tail_kernel, out_shape=jax.ShapeDtypeStruct(x.shape, F32), grid=rows.grid,
        in_specs=[rows.rows(D_FOURIER), rows.chunk(first_chunk), rows.rows(D_MODEL), rows.mod(2),
                  _const_spec((D_SSM, D_SSM)), _const_spec((1, D_SSM)),
                  _const_spec((D_FOURIER, D_MODEL)), _const_spec((D_SSM, D_MODEL))] + _mlp_specs(rows),
        out_specs=rows.rows(D_MODEL),
        scratch_shapes=[pltpu.VMEM((rows.tm, LANES), F32)] * SLABS,
        compiler_params=_params(1), name="even_tail",
    )(ya, y, x, mod, w_glu, b_glu, w_out[:D_FOURIER], w_out[D_FOURIER:], norm_g, mod, mod, mod, w1, w2)


def _odd_tail(a, x, rows, mod, w_out, norm_g, w1, w2):
    return pl.pallas_call(
        _odd_tail_kernel, out_shape=jax.ShapeDtypeStruct(x.shape, F32), grid=rows.grid,
        in_specs=[rows.rows3(a.shape[1], D_MODEL), rows.rows(D_MODEL), rows.mod(2),
                  _const_spec((D_MODEL, D_MODEL))] + _mlp_specs(rows),
        out_specs=rows.rows(D_MODEL),
        compiler_params=_params(2), name="odd_tail",
    )(a, x, mod, w_out, norm_g, mod, mod, mod, w1, w2)


KEY_CHUNK = 256
V_ROWS = DA_V_DIM + 16
Q_SCALE = DA_HEAD_DIM ** -0.5 * math.log2(math.e)


def _qkv_kernel(*refs, rope):
    if rope:
        (x_ref, g_ref, sh_ref, sc_ref, w_ref, gqk_ref, ones_ref,
         cos_ref, sa_ref, sb_ref, qt_ref, k_ref, vt_ref) = refs
    else:
        x_ref, g_ref, sh_ref, sc_ref, w_ref, gqk_ref, ones_ref, qt_ref, k_ref, vt_ref = refs
    h = _norm_mod(x_ref[...], g_ref[...], sh_ref[...], sc_ref[...]).astype(BF16)
    y = jnp.dot(h, w_ref[...], preferred_element_type=F32)
    tm = y.shape[0]
    for s in range(2 * DA_HEADS):
        blk = y[:, s * LANES:(s + 1) * LANES]
        ssq = jnp.dot((blk * blk).astype(BF16), ones_ref[...], preferred_element_type=F32)
        nrm = blk * gqk_ref[s // DA_HEADS]
        if rope:
            nrm = (nrm * cos_ref[...] + pltpu.roll(nrm, LANES - 16, 1) * sa_ref[...]
                   + pltpu.roll(nrm, 16, 1) * sb_ref[...])
        nrm = nrm * lax.rsqrt(ssq * (1.0 / DA_HEAD_DIM) + EPS)
        if s < DA_HEADS:
            qt_ref[s] = nrm.T.astype(BF16)
        else:
            hd = s - DA_HEADS
            k_ref[:, hd * LANES:(hd + 1) * LANES] = nrm.astype(BF16)
    ones = jnp.ones((V_ROWS - DA_V_DIM, tm), BF16)
    for hd in range(DA_HEADS):
        v = y[:, (2 * DA_HEADS + hd) * LANES:(2 * DA_HEADS + hd + 1) * LANES]
        vt_ref[hd, :DA_V_DIM, :] = v.T.astype(BF16)
        vt_ref[hd, DA_V_DIM:, :] = ones


def _qkv(x, rows, norm_g, mod, w_in, gqk, ones_bd, rope_tabs):
    nb, rpb, tm = rows.nb, rows.rpb, rows.tm
    assert tm % KEY_CHUNK == 0
    rope = rope_tabs is not None
    in_specs = [rows.rows(D_MODEL), _const_spec((1, D_MODEL)), rows.mod(0), rows.mod(1),
                _const_spec((D_MODEL, 3 * D_MODEL)), _const_spec((2, 1, LANES)),
                _const_spec((LANES, LANES))]
    args = [x, norm_g, mod, mod, w_in, gqk, ones_bd]
    if rope:
        tab = pl.BlockSpec((tm, LANES), lambda b, i: (i, 0))
        in_specs += [tab, tab, tab]
        args += list(rope_tabs)
    out_shape = (jax.ShapeDtypeStruct((nb, DA_HEADS, LANES, rpb), BF16),
                 jax.ShapeDtypeStruct((nb, rpb, D_MODEL), BF16),
                 jax.ShapeDtypeStruct((nb, DA_HEADS, V_ROWS, rpb), BF16))
    out_specs = (pl.BlockSpec((None, DA_HEADS, LANES, tm), lambda b, i: (b, 0, 0, i)),
                 pl.BlockSpec((None, tm, D_MODEL), lambda b, i: (b, i, 0)),
                 pl.BlockSpec((None, DA_HEADS, V_ROWS, tm), lambda b, i: (b, 0, 0, i)))
    return pl.pallas_call(
        functools.partial(_qkv_kernel, rope=rope),
        out_shape=out_shape, grid=rows.grid, in_specs=in_specs, out_specs=out_specs,
        compiler_params=_params(2), name="qkv",
    )(*args)


def _rope_tables(length):
    pos = jnp.arange(length, dtype=jnp.int32)
    row = (pos // GRID_W).astype(F32)
    col = (pos % GRID_W).astype(F32)
    half = DA_HEAD_DIM // 2
    inv_freq = ROPE_THETA ** (-jnp.arange(0, half, 2, dtype=F32) / half)
    lane = jnp.arange(LANES)
    f = lane % 16
    is_col = (lane % DA_HEAD_DIM) >= half
    second = (lane % half) >= 16
    ang = jnp.where(is_col[None, :], col[:, None], row[:, None]) * inv_freq[f][None, :]
    cos, sin = jnp.cos(ang), jnp.sin(ang)
    sin_a = jnp.where(second[None, :], 0.0, -sin)
    sin_b = jnp.where(second[None, :], sin, 0.0)
    return cos, sin_a, sin_b


def _attn_kernel(*refs, n_kv, lam_init, tq):
    lp_ref, gh_ref, qt_ref = refs[:3]
    kv_refs = refs[3:3 + 2 * n_kv]
    o_ref = refs[3 + 2 * n_kv]
    bufs = refs[4 + 2 * n_kv:]
    lp = lp_ref[...]
    lam = (jnp.exp(jnp.sum(lp[0:1] * lp[1:2], axis=-1, keepdims=True))
           - jnp.exp(jnp.sum(lp[2:3] * lp[3:4], axis=-1, keepdims=True)) + lam_init)
    hb = qt_ref.shape[0]
    items = [(hh, j) for hh in range(hb) for j in range(qt_ref.shape[2] // tq)]
    kc = KEY_CHUNK

    key_chunks = []
    for i in range(n_kv):
        for c in range(kv_refs[2 * i].shape[0] // kc):
            key_chunks.append((kv_refs[2 * i], kv_refs[2 * i + 1], c))

    def masked_q(item):
        hh, j = item
        qt = qt_ref[hh, :, j * tq:(j + 1) * tq]
        row = lax.broadcasted_iota(jnp.int32, qt.shape, 0)
        zero = jnp.zeros_like(qt)
        return jnp.concatenate([jnp.where(row < DA_HEAD_DIM, qt, zero),
                                jnp.where(row >= DA_HEAD_DIM, qt, zero)], axis=1)

    def score_chunk(n, hh, qst, s_scr, m):
        k_ref, _, c = key_chunks[n]
        s = jnp.dot(k_ref[c * kc:(c + 1) * kc, hh * LANES:(hh + 1) * LANES], qst,
                    preferred_element_type=F32)
        s_scr[n * kc:(n + 1) * kc, :] = s
        return jnp.maximum(m, s.reshape(kc // 8, 8, 2 * tq).max(axis=0))

    def value_chunk(n, hh, s_scr, m, acc):
        _, vt_ref, c = key_chunks[n]
        r = pl.multiple_of(n * kc + jnp.minimum(pl.program_id(0), 0) * kc, kc)
        p = jnp.exp2((s_scr[pl.ds(r, kc), :] - m).astype(BF16))
        return acc + jnp.dot(vt_ref[hh, :, c * kc:(c + 1) * kc], p, preferred_element_type=F32)

    def finish(item, acc):
        hh, j = item
        o2 = acc[:DA_V_DIM] / acc[DA_V_DIM:DA_V_DIM + 1]
        o = o2[:, :tq] - lam * o2[:, tq:]
        o = o * lax.rsqrt(jnp.mean(o * o, axis=0, keepdims=True) + EPS) * gh_ref[...]
        o_ref[j * tq:(j + 1) * tq, hh * LANES:(hh + 1) * LANES] = (o * (1.0 - lam_init)).T.astype(BF16)

    neg = jnp.full((8, 2 * tq), -jnp.inf, F32)
    qst = masked_q(items[0])
    m8 = neg
    for n in range(len(key_chunks)):
        m8 = score_chunk(n, items[0][0], qst, bufs[0], m8)
    for i, item in enumerate(items):
        nxt = items[i + 1] if i + 1 < len(items) else None
        m = m8.max(axis=0, keepdims=True)
        acc = jnp.zeros((V_ROWS, 2 * tq), F32)
        if nxt is not None:
            qst = masked_q(nxt)
            m8 = neg
        for n in range(len(key_chunks)):
            if nxt is not None:
                m8 = score_chunk(n, nxt[0], qst, bufs[(i + 1) % 2], m8)
            acc = value_chunk(n, item[0], bufs[i % 2], m, acc)
        finish(item, acc)


def _attention(qt, kv_list, lam_p, g_head, lam_init, tq, hb):
    nb, _, _, lq = qt.shape
    in_specs = [_const_spec((4, DA_HEAD_DIM)), _const_spec((DA_V_DIM, 1)),
                pl.BlockSpec((None, hb, LANES, lq), lambda b, h: (b, h, 0, 0))]
    args = [lam_p, g_head, qt]
    total = 0
    for k, vt in kv_list:
        lk = k.shape[1]
        total += lk
        in_specs.append(pl.BlockSpec((None, lk, hb * LANES), lambda b, h: (b, 0, h)))
        in_specs.append(pl.BlockSpec((None, hb, V_ROWS, lk), lambda b, h: (b, h, 0, 0)))
        args += [k, vt]
    n_bufs = min(2, hb * (lq // tq))
    return pl.pallas_call(
        functools.partial(_attn_kernel, n_kv=len(kv_list), lam_init=lam_init, tq=tq),
        out_shape=jax.ShapeDtypeStruct((nb, lq, D_MODEL), BF16),
        grid=(nb, DA_HEADS // hb),
        in_specs=in_specs,
        out_specs=pl.BlockSpec((None, lq, hb * LANES), lambda b, h: (b, 0, h)),
        scratch_shapes=[pltpu.VMEM((total, 2 * tq), F32)] * n_bufs,
        compiler_params=_params(2), name="diff_attn",
    )(*args)


def kernel(x, c, ctx, c_ctx, norm1_g, norm2_g, ada_w, ada_b, mlp_w1, mlp_w2, ev_w_in, ev_w_out, ssm_a_re, ssm_a_im, ssm_log_dt, ssm_b_re, ssm_b_im, ssm_c_re, ssm_c_im, ssm_d, ssm_w_glu, ssm_b_glu, od_w_in, od_w_out, od_q_norm, od_k_norm, od_lambda, od_head_norm):
    nb, length, d = x.shape
    clen = ctx.shape[1]
    assert d == D_MODEL and length % GRID_W == 0
    assert length % SSM_CHUNK == 0 and clen % SSM_CHUNK == 0

    xs = x.reshape(nb * length, d)
    cs = ctx.reshape(nb * clen, d)
    rows_x = _Rows(nb, length, _tile(length, 512), None)
    rows_c = _Rows(1, nb * clen, _tile(nb * clen, 512), nb)
    rows_cb = _Rows(nb, clen, clen, nb)
    chunks_x = _ChunkRows(nb, length // SSM_CHUNK, None)
    chunks_c = _ChunkRows(nb, clen // SSM_CHUNK, nb)

    n_cond = nb + 1
    pad = (-n_cond) % 8
    cond = jnp.concatenate([c, c_ctx[None, :], jnp.zeros((pad, d), F32)], axis=0)
    mods = _adaln(cond, ada_w, ada_b)[:, :n_cond].reshape(DEPTH, n_cond, 1, 6 * d)

    kf = lax.iota(jnp.int32, FOURIER_CH)
    ang = ((kf[:, None] * kf[None, :]) % FOURIER_CH).astype(F32) * (2.0 * math.pi / FOURIER_CH)
    cs_ch = jnp.concatenate([jnp.cos(ang), jnp.sin(ang)], axis=1).astype(BF16)
    dft_x = _dft_mats(length)
    dft_c = _dft_mats(clen)
    rope_tabs = _rope_tables(length)
    lane = jnp.arange(LANES)
    ones_bd = (lane[:, None] // DA_HEAD_DIM == lane[None, :] // DA_HEAD_DIM).astype(BF16)

    ncc = clen // SSM_CHUNK
    ncx = length // SSM_CHUNK
    nct = ncx + ncc

    for i in range(DEPTH):
        last = i == DEPTH - 1
        j = i // 2
        mod = mods[i]
        n1 = norm1_g[i].reshape(1, d)
        n2 = norm2_g[i].reshape(1, d)
        if i % 2 == 0:
            w_in = ev_w_in[j].astype(BF16)
            u_shape = (SLABS, nct, nb, SSM_CHUNK * LANES)
            x3 = xs.reshape(nb, length, d)
            c3 = cs.reshape(nb, clen, d)
            vc_x, vs_x, u = _even_in(x3, chunks_x, n1, mod, w_in, cs_ch, u_shape, 0)
            vc_c, vs_c, u = _even_in(c3, chunks_c, n1, mod, w_in, cs_ch, u_shape, ncx, u)
            ya_x = _fourier(vc_x, vs_x, dft_x)
            m, bm, wh, at_re, at_im = _ssm_ops(ssm_a_re[j], ssm_a_im[j], ssm_log_dt[j], ssm_b_re[j],
                                                ssm_b_im[j], ssm_c_re[j], ssm_c_im[j], ssm_d[j])
            u = u.reshape(SLABS, nct * nb, SSM_CHUNK * LANES)
            contrib = _ssm_contrib(u, bm)
            h_re, h_im = _ssm_scan(contrib, at_re, at_im, ncc, nct, nb)
            y = _ssm_read(u, h_re, h_im, m, wh).reshape(u_shape)
            w_glu = ssm_w_glu[j].astype(BF16)
            b_glu = ssm_b_glu[j].reshape(1, D_SSM)
            w_out = ev_w_out[j].astype(BF16)
            w1 = mlp_w1[i].astype(BF16)
            w2 = mlp_w2[i].astype(BF16)
            xs = _even_tail(ya_x, y, 0, x3, chunks_x, mod, w_glu, b_glu, w_out, n2, w1, w2).reshape(nb * length, d)
            if not last:
                ya_c = _fourier(vc_c, vs_c, dft_c)
                cs = _even_tail(ya_c, y, ncx, c3, chunks_c, mod, w_glu, b_glu, w_out, n2, w1, w2).reshape(nb * clen, d)
        else:
            lam_init = 0.8 - 0.6 * math.exp(-0.3 * i)
            w_in = od_w_in[j].astype(BF16)
            gqk = jnp.stack([jnp.tile(od_q_norm[j], 2) * Q_SCALE,
                             jnp.tile(od_k_norm[j], 2)]).reshape(2, 1, LANES)
            qt_x, k_x, vt_x = _qkv(xs, rows_x, n1, mod, w_in, gqk, ones_bd, rope_tabs)
            qt_c, k_c, vt_c = _qkv(cs, rows_cb, n1, mod, w_in, gqk, ones_bd, None)
            g_head = od_head_norm[j].reshape(DA_V_DIM, 1)
            w_out = od_w_out[j].astype(BF16)
            o_x = _attention(qt_x, [(k_x, vt_x), (k_c, vt_c)], od_lambda[j], g_head, lam_init,
                             _tile(length, 512, LANES), 1)
            w1 = mlp_w1[i].astype(BF16)
            w2 = mlp_w2[i].astype(BF16)
            xs = _odd_tail(o_x, xs, rows_x, mod, w_out, n2, w1, w2)
            if not last:
                o_c = _attention(qt_c, [(k_c, vt_c)], od_lambda[j], g_head, lam_init, clen, DA_HEADS)
                cs = _odd_tail(o_c, cs, rows_c, mod, w_out, n2, w1, w2)
    return xs.reshape(nb, length, d)
```

```python
import functools
import math

import jax
import jax.numpy as jnp
from jax import lax
from jax.experimental import pallas as pl
from jax.experimental.pallas import tpu as pltpu

F32 = jnp.float32
BF16 = jnp.bfloat16

D_MODEL = 1024
DEPTH = 4
GRID_W = 64
EPS = 1e-6
D_FF = 4 * D_MODEL
D_FOURIER = D_MODEL // 2
FOURIER_GROUPS = 4
FOURIER_CH = D_FOURIER // FOURIER_GROUPS
D_SSM = D_MODEL - D_FOURIER
SSM_GROUP = 16
SSM_GROUPS = D_SSM // SSM_GROUP
SSM_STATE = 64
DA_HEAD_DIM = 64
DA_V_DIM = 2 * DA_HEAD_DIM
DA_HEADS = D_MODEL // DA_V_DIM
ROPE_THETA = 10000.0

LANES = 128
SSM_CHUNK = 16
SLABS = D_SSM // LANES
SLAB_GROUPS = LANES // SSM_GROUP
SLAB_STATE = SLAB_GROUPS * SSM_STATE
VMEM_LIMIT = 56 * 1024 * 1024


def _params(n_axes, flags=None):
    return pltpu.CompilerParams(dimension_semantics=("arbitrary",) * n_axes,
                                vmem_limit_bytes=VMEM_LIMIT, flags=flags)


def _tile(n, target, mult=8):
    t = min(n, target)
    while t > 1 and (n % t or t % mult):
        t -= 1
    return t if n % t == 0 else n


def _const_spec(shape):
    nd = len(shape)
    return pl.BlockSpec(shape, lambda *_: (0,) * nd, pipeline_mode=pl.Buffered(1))


def _sigmoid(x):
    return 1.0 / (1.0 + jnp.exp(-x))


def _gelu_tanh(x):
    c = math.sqrt(2.0 / math.pi)
    return x * (0.5 * (1.0 + jnp.tanh(c * (x + 0.044715 * (x * x * x)))))


def _norm_mod(x, g, shift, scale):
    ms = jnp.mean(x * x, axis=-1, keepdims=True)
    y = x * lax.rsqrt(ms + EPS) * g
    return y * (1.0 + scale) + shift


def _adaln_kernel(c_ref, w_ref, b_ref, o_ref):
    c = c_ref[...]
    s = (c * _sigmoid(c)).astype(BF16)
    o_ref[0] = jnp.dot(s, w_ref[0].astype(BF16), preferred_element_type=F32) + b_ref[0]


def _adaln(cond, ada_w, ada_b):
    depth, d, n = ada_w.shape
    r = cond.shape[0]
    tn = _tile(n, 1536, LANES)
    return pl.pallas_call(
        _adaln_kernel,
        out_shape=jax.ShapeDtypeStruct((depth, r, n), F32),
        grid=(depth, n // tn),
        in_specs=[pl.BlockSpec((r, d), lambda l, j: (0, 0)),
                  pl.BlockSpec((1, d, tn), lambda l, j: (l, 0, j)),
                  pl.BlockSpec((1, 1, tn), lambda l, j: (l, 0, j))],
        out_specs=pl.BlockSpec((1, r, tn), lambda l, j: (l, 0, j)),
        compiler_params=_params(2), name="adaln",
    )(cond, ada_w, ada_b.reshape(depth, 1, n))


class _Rows:
    def __init__(self, nb, rpb, tm, mod_row):
        self.nb, self.rpb, self.tm = nb, rpb, tm
        self.nt = rpb // tm
        self.grid = (nb, self.nt)
        self.mod_row = mod_row

    def rows(self, cols):
        nt = self.nt
        return pl.BlockSpec((self.tm, cols), lambda b, i: (b * nt + i, 0))

    def rows3(self, per, cols):
        nt, tm = self.nt, self.tm
        if tm <= per:
            k = per // tm
            return pl.BlockSpec((None, tm, cols), lambda b, i: ((b * nt + i) // k, (b * nt + i) % k, 0))
        return pl.BlockSpec((tm // per, per, cols), lambda b, i: (b * nt + i, 0, 0))

    def mod(self, k):
        if self.mod_row is None:
            return pl.BlockSpec((None, 1, D_MODEL), lambda b, i: (b, 0, k))
        r = self.mod_row
        return pl.BlockSpec((None, 1, D_MODEL), lambda b, i: (r, 0, k))


class _ChunkRows:
    def __init__(self, nb, n_chunks, mod_row):
        self.nb, self.grid, self.mod_row = nb, (n_chunks,), mod_row
        self.tm = nb * SSM_CHUNK

    def rows(self, cols):
        return pl.BlockSpec((self.nb, SSM_CHUNK, cols), lambda c: (0, c, 0))

    def mod(self, k):
        if self.mod_row is None:
            return pl.BlockSpec((self.nb, 1, D_MODEL), lambda c: (0, 0, k))
        r = self.mod_row
        return pl.BlockSpec((1, 1, D_MODEL), lambda c: (r, 0, k))

    def chunk(self, first_chunk):
        return pl.BlockSpec((SLABS, None, self.nb, SSM_CHUNK * LANES), lambda c: (0, first_chunk + c, 0, 0))


def _even_in_kernel(*refs, aliased):
    if aliased:
        refs = refs[1:]
    x_ref, g_ref, sh_ref, sc_ref, w_ref, cs_ref, vc_ref, vs_ref, u_ref = refs[:9]
    zs_scr = refs[9:]
    nb, t_len, _ = x_ref.shape
    h = _norm_mod(x_ref[...], g_ref[...], sh_ref[...], sc_ref[...])
    z = jnp.dot(h.reshape(nb * t_len, D_MODEL).astype(BF16), w_ref[...], preferred_element_type=F32)
    zf = z[:, :D_FOURIER].astype(BF16)
    vc, vs = [], []
    for gi in range(FOURIER_GROUPS):
        v = jnp.dot(zf[:, gi * FOURIER_CH:(gi + 1) * FOURIER_CH], cs_ref[...], preferred_element_type=F32)
        vc.append(v[:, :FOURIER_CH].astype(BF16))
        vs.append(v[:, FOURIER_CH:].astype(BF16))
    vc_ref[...] = jnp.concatenate(vc, axis=1).reshape(nb, t_len, D_FOURIER)
    vs_ref[...] = jnp.concatenate(vs, axis=1).reshape(nb, t_len, D_FOURIER)
    for s in range(SLABS):
        zs_scr[s][...] = z[:, D_FOURIER + s * LANES:D_FOURIER + (s + 1) * LANES]
        for t in range(t_len):
            u_ref[s, :, t * LANES:(t + 1) * LANES] = zs_scr[s][pl.ds(t, nb, stride=t_len), :].astype(BF16)


def _even_in(x, rows, norm_g, mod, w_in, cs, u_shape, first_chunk, u=None):
    nb, n_tok, _ = x.shape
    outs = (jax.ShapeDtypeStruct((nb, n_tok, D_FOURIER), BF16), jax.ShapeDtypeStruct((nb, n_tok, D_FOURIER), BF16),
            jax.ShapeDtypeStruct(u_shape, BF16))
    in_specs = [rows.rows(D_MODEL), _const_spec((1, D_MODEL)), rows.mod(0), rows.mod(1),
                _const_spec((D_MODEL, D_MODEL)), _const_spec((FOURIER_CH, 2 * FOURIER_CH))]
    args = [x, norm_g, mod, mod, w_in, cs]
    aliases = {}
    if u is not None:
        in_specs = [pl.BlockSpec(memory_space=pl.ANY)] + in_specs
        args = [u] + args
        aliases = {0: 2}
    return pl.pallas_call(
        functools.partial(_even_in_kernel, aliased=u is not None), out_shape=outs, grid=rows.grid,
        in_specs=in_specs,
        out_specs=(rows.rows(D_FOURIER), rows.rows(D_FOURIER), rows.chunk(first_chunk)),
        scratch_shapes=[pltpu.VMEM((rows.tm, LANES), F32)] * SLABS,
        input_output_aliases=aliases,
        compiler_params=_params(1), name="even_in",
    )(*args)


DFT_PAD = 16


def _fourier_kernel(cl_ref, sl_ref, rev_ref, vc_ref, vs_ref, o_ref, *, scale):
    half = rev_ref.shape[0]
    a = jnp.dot(cl_ref[...], vc_ref[0], preferred_element_type=F32)
    b = jnp.dot(sl_ref[...], vs_ref[0], preferred_element_type=F32)
    o_ref[0, :half, :] = ((a[:half] - b[:half]) * scale).astype(BF16)
    mirrored = ((a + b) * scale).astype(BF16)
    o_ref[0, half:, :] = jnp.dot(rev_ref[...], mirrored, preferred_element_type=F32).astype(BF16)


def _fourier(vc3, vs3, mats):
    cl, sl, rev = mats
    nb, length, _ = vc3.shape
    scale = 1.0 / math.sqrt(length * FOURIER_CH)
    blk = pl.BlockSpec((1, length, D_FOURIER), lambda b: (b, 0, 0))
    return pl.pallas_call(
        functools.partial(_fourier_kernel, scale=scale),
        out_shape=jax.ShapeDtypeStruct((nb, length, D_FOURIER), BF16),
        grid=(nb,),
        in_specs=[_const_spec(cl.shape), _const_spec(sl.shape), _const_spec(rev.shape), blk, blk],
        out_specs=blk,
        compiler_params=_params(1), name="fourier",
    )(cl, sl, rev, vc3, vs3)


def _dft_mats(n):
    half = n // 2
    k = lax.iota(jnp.int32, half + DFT_PAD)
    t = lax.iota(jnp.int32, n)
    ang = ((k[:, None] * t[None, :]) % n).astype(F32) * (2.0 * math.pi / n)
    cl = jnp.where((k <= half)[:, None], jnp.cos(ang), 0.0)
    sl = jnp.where((k < half)[:, None], jnp.sin(ang), 0.0)
    j = lax.iota(jnp.int32, half)
    rev = k[None, :] == (half - j)[:, None]
    return cl.astype(BF16), sl.astype(BF16), rev.astype(BF16)


def _ssm_ops_kernel(ar_ref, ai_ref, ldt_ref, btr_ref, bti_ref, ctr_ref, cti_ref,
                    bm_ref, wh_ref, atr_ref, ati_ref, e_ref):
    t = SSM_CHUNK
    n = SLAB_STATE
    rows = lax.broadcasted_iota(jnp.int32, (LANES, n), 0)
    cols = lax.broadcasted_iota(jnp.int32, (LANES, n), 1)
    same_group = (rows // SSM_GROUP) == (cols // SSM_STATE)
    lag = lax.broadcasted_iota(jnp.int32, (t + 8, n), 0).astype(F32)
    nt_dims = (((1,), (1,)), ((), ()))
    for d in range(2):
        ar, ai = ar_ref[d], ai_ref[d]
        dt = jnp.exp(ldt_ref[d])
        xr, xi = ar * dt, ai * dt
        mag = jnp.exp(lag * xr)
        pr = mag * jnp.cos(lag * xi)
        pi = mag * jnp.sin(lag * xi)
        abr, abi = pr[1:2], pi[1:2]
        den = ar * ar + ai * ai
        fr = ((abr - 1.0) * ar + abi * ai) / den
        fi = (abi * ar - (abr - 1.0) * ai) / den
        br = jnp.where(same_group, btr_ref[d], 0.0)
        bi = jnp.where(same_group, bti_ref[d], 0.0)
        bbr = fr * br - fi * bi
        bbi = fr * bi + fi * br
        cr = jnp.where(same_group, ctr_ref[d], 0.0)
        ci = jnp.where(same_group, cti_ref[d], 0.0)
        for k in range(t):
            pkr, pki = pr[k:k + 1], pi[k:k + 1]
            x_r = bbr * pkr - bbi * pki
            x_i = bbr * pki + bbi * pkr
            s = t - 1 - k if d == 0 else k
            bm_ref[s * LANES:(s + 1) * LANES, 2 * d * n:(2 * d + 1) * n] = x_r.astype(BF16)
            bm_ref[s * LANES:(s + 1) * LANES, (2 * d + 1) * n:(2 * d + 2) * n] = x_i.astype(BF16)
            e_ref[d, k] = (lax.dot_general(x_r, cr, nt_dims, precision=lax.Precision.HIGHEST,
                                           preferred_element_type=F32)
                           - lax.dot_general(x_i, ci, nt_dims, precision=lax.Precision.HIGHEST,
                                             preferred_element_type=F32))
        for k in range(1, t + 1):
            pkr, pki = pr[k:k + 1], pi[k:k + 1]
            w_r = cr * pkr - ci * pki
            w_i = cr * pki + ci * pkr
            tt = k - 1 if d == 0 else t - k
            wh_ref[2 * d * n:(2 * d + 1) * n, tt * LANES:(tt + 1) * LANES] = w_r.T.astype(BF16)
            wh_ref[(2 * d + 1) * n:(2 * d + 2) * n, tt * LANES:(tt + 1) * LANES] = (-w_i).T.astype(BF16)
        atr_ref[d] = pr[t:t + 1]
        ati_ref[d] = pi[t:t + 1]


def _ssm_toeplitz_kernel(e_ref, dsk_ref, m_ref):
    t = SSM_CHUNK
    s = pl.program_id(1)
    r = lax.broadcasted_iota(jnp.int32, (LANES, LANES), 0)
    c = lax.broadcasted_iota(jnp.int32, (LANES, LANES), 1)
    skip = jnp.where(r == c, jnp.broadcast_to(dsk_ref[...], (LANES, LANES)), 0.0)
    for tt in range(t):
        blk = ((tt >= s).astype(F32) * e_ref[0, jnp.maximum(tt - s, 0)]
               + (s >= tt).astype(F32) * e_ref[1, jnp.maximum(s - tt, 0)]
               + (s == tt).astype(F32) * skip)
        m_ref[:, tt * LANES:(tt + 1) * LANES] = blk.astype(BF16)


def _ssm_ops(a_re, a_im, log_dt, b_re, b_im, c_re, c_im, d_skip):
    t = SSM_CHUNK
    n = SLAB_STATE

    def per_state(x):
        return x.astype(F32).reshape(2, SLABS, 1, n)

    def per_channel(x):
        x = x.astype(F32).reshape(2, SLABS, LANES, SSM_STATE)
        return jnp.tile(x, (1, 1, 1, SLAB_GROUPS))

    ldt = jnp.broadcast_to(log_dt[..., None], a_re.shape)
    vec = pl.BlockSpec((2, None, 1, n), lambda z: (0, z, 0, 0))
    mat = pl.BlockSpec((2, None, LANES, n), lambda z: (0, z, 0, 0))
    bm, wh, at_re, at_im, e = pl.pallas_call(
        _ssm_ops_kernel,
        out_shape=(jax.ShapeDtypeStruct((SLABS, t * LANES, 4 * n), BF16),
                   jax.ShapeDtypeStruct((SLABS, 4 * n, t * LANES), BF16),
                   jax.ShapeDtypeStruct((SLABS, 2, 1, n), F32),
                   jax.ShapeDtypeStruct((SLABS, 2, 1, n), F32),
                   jax.ShapeDtypeStruct((SLABS, 2, t, LANES, LANES), F32)),
        grid=(SLABS,),
        in_specs=[vec, vec, vec, mat, mat, mat, mat],
        out_specs=(pl.BlockSpec((None, t * LANES, 4 * n), lambda z: (z, 0, 0)),
                   pl.BlockSpec((None, 4 * n, t * LANES), lambda z: (z, 0, 0)),
                   pl.BlockSpec((None, 2, 1, n), lambda z: (z, 0, 0, 0)),
                   pl.BlockSpec((None, 2, 1, n), lambda z: (z, 0, 0, 0)),
                   pl.BlockSpec((None, 2, t, LANES, LANES), lambda z: (z, 0, 0, 0, 0))),
        compiler_params=_params(1), name="ssm_ops",
    )(per_state(a_re), per_state(a_im), per_state(ldt),
      per_channel(b_re.transpose(0, 1, 3, 2)), per_channel(b_im.transpose(0, 1, 3, 2)),
      per_channel(c_re), per_channel(c_im))
    m = pl.pallas_call(
        _ssm_toeplitz_kernel,
        out_shape=jax.ShapeDtypeStruct((SLABS, t * LANES, t * LANES), BF16),
        grid=(SLABS, t),
        in_specs=[pl.BlockSpec((None, 2, t, LANES, LANES), lambda z, s: (z, 0, 0, 0, 0)),
                  pl.BlockSpec((None, 1, LANES), lambda z, s: (z, 0, 0))],
        out_specs=pl.BlockSpec((None, LANES, t * LANES), lambda z, s: (z, s, 0)),
        compiler_params=_params(2), name="ssm_toeplitz",
    )(e, d_skip.astype(F32).reshape(SLABS, 1, LANES))
    return m, bm, wh, at_re.reshape(SLABS * 2, 1, n), at_im.reshape(SLABS * 2, 1, n)


def _ssm_contrib_kernel(u_ref, bm_ref, o_ref):
    o_ref[...] = jnp.dot(u_ref[...], bm_ref[...], preferred_element_type=F32)


def _ssm_contrib(u, bm):
    slabs, r, k = u.shape
    n = bm.shape[2]
    tr = _tile(r, 576)
    return pl.pallas_call(
        _ssm_contrib_kernel,
        out_shape=jax.ShapeDtypeStruct((slabs, r, n), F32),
        grid=(slabs, r // tr),
        in_specs=[pl.BlockSpec((None, tr, k), lambda s, i: (s, i, 0)),
                  pl.BlockSpec((None, k, n), lambda s, i: (s, 0, 0))],
        out_specs=pl.BlockSpec((None, tr, n), lambda s, i: (s, i, 0)),
        compiler_params=_params(2), name="ssm_contrib",
    )(u, bm)


SCAN_UNROLL = 8


def _ssm_scan_kernel(ar_ref, ai_ref, cr_ref, ci_ref, hr_ref, hi_ref, *, ncc, nct, rb):
    d = pl.program_id(1)
    lt = ar_ref.shape[-1]
    ncx = nct - ncc
    ar = jnp.broadcast_to(ar_ref[...], (rb, lt))
    ai = jnp.broadcast_to(ai_ref[...], (rb, lt))
    zero = jnp.zeros((rb, lt), F32)

    def advance(i_src, i_dst, hr, hi):
        r0 = pl.multiple_of(i_src * rb, rb)
        r1 = pl.multiple_of(i_dst * rb, rb)
        nr = ar * hr - ai * hi + cr_ref[pl.ds(r0, rb), :]
        ni = ar * hi + ai * hr + ci_ref[pl.ds(r0, rb), :]
        hr_ref[pl.ds(r1, rb), :] = nr
        hi_ref[pl.ds(r1, rb), :] = ni
        return nr, ni

    def start(i):
        hr_ref[pl.ds(i * rb, rb), :] = zero
        hi_ref[pl.ds(i * rb, rb), :] = zero
        return zero, zero

    @pl.when(d == 0)
    def _():
        c = start(ncx)
        c = lax.fori_loop(0, ncc - 1, lambda k, c: advance(ncx + k, ncx + k + 1, *c), c, unroll=SCAN_UNROLL)
        c = advance(nct - 1, 0, *c)
        lax.fori_loop(0, ncx - 1, lambda k, c: advance(k, k + 1, *c), c, unroll=SCAN_UNROLL)

    @pl.when(d == 1)
    def _():
        c = start(nct - 1)
        c = lax.fori_loop(0, ncc - 1, lambda k, c: advance(nct - 1 - k, nct - 2 - k, *c), c, unroll=SCAN_UNROLL)
        c = advance(ncx, ncx - 1, *c)
        lax.fori_loop(0, ncx - 1, lambda k, c: advance(ncx - 1 - k, ncx - 2 - k, *c), c, unroll=SCAN_UNROLL)


def _ssm_scan(contrib, at_re, at_im, ncc, nct, rb):
    slabs, r, n = contrib.shape
    lt = 2 * LANES
    q = SLAB_STATE // lt
    re_spec = pl.BlockSpec((None, r, lt), lambda s, d, j: (s, 0, d * 2 * q + j))
    im_spec = pl.BlockSpec((None, r, lt), lambda s, d, j: (s, 0, d * 2 * q + q + j))
    a_spec = pl.BlockSpec((None, 1, lt), lambda s, d, j: (s * 2 + d, 0, j))
    h_spec = pl.BlockSpec((None, r, lt), lambda s, d, j: (s, 0, d * q + j))
    out = jax.ShapeDtypeStruct((slabs, r, n // 2), F32)
    return pl.pallas_call(
        functools.partial(_ssm_scan_kernel, ncc=ncc, nct=nct, rb=rb),
        out_shape=(out, out),
        grid=(slabs, 2, q),
        in_specs=[a_spec, a_spec, re_spec, im_spec],
        out_specs=(h_spec, h_spec),
        compiler_params=_params(3), name="ssm_scan",
    )(at_re, at_im, contrib, contrib)


def _ssm_read_kernel(u_ref, hr_ref, hi_ref, m_ref, wh_ref, o_ref):
    q = SLAB_STATE
    acc = jnp.dot(u_ref[...], m_ref[...], preferred_element_type=F32)
    for d in range(2):
        acc += jnp.dot(hr_ref[:, d * q:(d + 1) * q].astype(BF16), wh_ref[2 * d * q:(2 * d + 1) * q, :],
                       preferred_element_type=F32)
        acc += jnp.dot(hi_ref[:, d * q:(d + 1) * q].astype(BF16), wh_ref[(2 * d + 1) * q:(2 * d + 2) * q, :],
                       preferred_element_type=F32)
    o_ref[...] = acc.astype(BF16)


def _ssm_read(u, h_re, h_im, m, wh):
    slabs, r, k = u.shape
    ns = h_re.shape[2]
    nw = wh.shape[1]
    tr = _tile(r, 576)
    tn = k // 2
    return pl.pallas_call(
        _ssm_read_kernel,
        out_shape=jax.ShapeDtypeStruct((slabs, r, k), BF16),
        grid=(slabs, 2, r // tr),
        in_specs=[pl.BlockSpec((None, tr, k), lambda s, j, i: (s, i, 0)),
                  pl.BlockSpec((None, tr, ns), lambda s, j, i: (s, i, 0)),
                  pl.BlockSpec((None, tr, ns), lambda s, j, i: (s, i, 0)),
                  pl.BlockSpec((None, k, tn), lambda s, j, i: (s, 0, j)),
                  pl.BlockSpec((None, nw, tn), lambda s, j, i: (s, 0, j))],
        out_specs=pl.BlockSpec((None, tr, tn), lambda s, j, i: (s, i, j)),
        compiler_params=_params(3), name="ssm_read",
    )(u, h_re, h_im, m, wh)


MLP_FF_CHUNK = 1024


def _mlp_tail(x, g_ref, sh_ref, sc_ref, gate_ref, w1_ref, w2_ref, o_ref):
    fc = MLP_FF_CHUNK
    h = _norm_mod(x, g_ref[...], sh_ref[...], sc_ref[...]).reshape(-1, D_MODEL).astype(BF16)
    acc = jnp.zeros(h.shape, F32)
    for k in range(D_FF // fc):
        a = jnp.dot(h, w1_ref[:, k * fc:(k + 1) * fc], preferred_element_type=F32)
        a = jnp.square(jnp.maximum(a, 0.0)).astype(BF16)
        acc += jnp.dot(a, w2_ref[k * fc:(k + 1) * fc, :], preferred_element_type=F32)
    o_ref[...] = x + gate_ref[...] * acc.reshape(x.shape)


def _even_tail_kernel(ya_ref, y_ref, x_ref, gate1_ref, wg_ref, bg_ref, wa_ref, wb_ref,
                      g_ref, sh_ref, sc_ref, gate2_ref, w1_ref, w2_ref, o_ref, *ys_scr):
    nb, t_len, _ = x_ref.shape
    for s in range(SLABS):
        for t in range(t_len):
            ys_scr[s][pl.ds(t, nb, stride=t_len), :] = y_ref[s, :, t * LANES:(t + 1) * LANES].astype(F32)
    y = _gelu_tanh(jnp.concatenate([r[...] for r in ys_scr], axis=1))
    glu = y * _sigmoid(jnp.dot(y.astype(BF16), wg_ref[...], preferred_element_type=F32) + bg_ref[...])
    acc = jnp.dot(ya_ref[...].reshape(nb * t_len, D_FOURIER), wa_ref[...], preferred_element_type=F32)
    acc += jnp.dot(glu.astype(BF16), wb_ref[...], preferred_element_type=F32)
    x1 = x_ref[...] + gate1_ref[...] * acc.reshape(nb, t_len, D_MODEL)
    _mlp_tail(x1, g_ref, sh_ref, sc_ref, gate2_ref, w1_ref, w2_ref, o_ref)


def _odd_tail_kernel(a_ref, x_ref, gate1_ref, wo_ref, g_ref, sh_ref, sc_ref, gate2_ref, w1_ref, w2_ref, o_ref):
    a = a_ref[...].reshape(x_ref.shape)
    x1 = x_ref[...] + gate1_ref[...] * jnp.dot(a, wo_ref[...], preferred_element_type=F32)
    _mlp_tail(x1, g_ref, sh_ref, sc_ref, gate2_ref, w1_ref, w2_ref, o_ref)


def _mlp_specs(rows):
    return [_const_spec((1, D_MODEL)), rows.mod(3), rows.mod(4), rows.mod(5),
            _const_spec((D_MODEL, D_FF)), _const_spec((D_FF, D_MODEL))]


def _even_tail(ya, y, first_chunk, x, rows, mod, w_glu, b_glu, w_out, norm_g, w1, w2):
    return pl.pallas_call(
        _even_tail_kernel, out_shape=jax.ShapeDtypeStruct(x.shape, F32), grid=rows.grid,
        in_specs=[rows.rows(D_FOURIER), rows.chunk(first_chunk), rows.rows(D_MODEL), rows.mod(2),
                  _const_spec((D_SSM, D_SSM)), _const_spec((1, D_SSM)),
                  _const_spec((D_FOURIER, D_MODEL)), _const_spec((D_SSM, D_MODEL))] + _mlp_specs(rows),
        out_specs=rows.rows(D_MODEL),
        scratch_shapes=[pltpu.VMEM((rows.tm, LANES), F32)] * SLABS,
        compiler_params=_params(1), name="even_tail",
    )(ya, y, x, mod, w_glu, b_glu, w_out[:D_FOURIER], w_out[D_FOURIER:], norm_g, mod, mod, mod, w1, w2)


def _odd_tail(a, x, rows, mod, w_out, norm_g, w1, w2):
    return pl.pallas_call(
        _odd_tail_kernel, out_shape=jax.ShapeDtypeStruct(x.shape, F32), grid=rows.grid,
        in_specs=[rows.rows3(a.shape[1], D_MODEL), rows.rows(D_MODEL), rows.mod(2),
                  _const_spec((D_MODEL, D_MODEL))] + _mlp_specs(rows),
        out_specs=rows.rows(D_MODEL),
        compiler_params=_params(2), name="odd_tail",
    )(a, x, mod, w_out, norm_g, mod, mod, mod, w1, w2)


KEY_CHUNK = 256
V_ROWS = DA_V_DIM + 16
Q_SCALE = DA_HEAD_DIM ** -0.5 * math.log2(math.e)


def _qkv_kernel(*refs, rope):
    if rope:
        (x_ref, g_ref, sh_ref, sc_ref, w_ref, gqk_ref, ones_ref,
         cos_ref, sa_ref, sb_ref, qt_ref, k_ref, vt_ref) = refs
    else:
        x_ref, g_ref, sh_ref, sc_ref, w_ref, gqk_ref, ones_ref, qt_ref, k_ref, vt_ref = refs
    h = _norm_mod(x_ref[...], g_ref[...], sh_ref[...], sc_ref[...]).astype(BF16)
    y = jnp.dot(h, w_ref[...], preferred_element_type=F32)
    tm = y.shape[0]
    for s in range(2 * DA_HEADS):
        blk = y[:, s * LANES:(s + 1) * LANES]
        ssq = jnp.dot((blk * blk).astype(BF16), ones_ref[...], preferred_element_type=F32)
        nrm = blk * gqk_ref[s // DA_HEADS]
        if rope:
            nrm = (nrm * cos_ref[...] + pltpu.roll(nrm, LANES - 16, 1) * sa_ref[...]
                   + pltpu.roll(nrm, 16, 1) * sb_ref[...])
        nrm = nrm * lax.rsqrt(ssq * (1.0 / DA_HEAD_DIM) + EPS)
        if s < DA_HEADS:
            qt_ref[s] = nrm.T.astype(BF16)
        else:
            hd = s - DA_HEADS
            k_ref[:, hd * LANES:(hd + 1) * LANES] = nrm.astype(BF16)
    ones = jnp.ones((V_ROWS - DA_V_DIM, tm), BF16)
    for hd in range(DA_HEADS):
        v = y[:, (2 * DA_HEADS + hd) * LANES:(2 * DA_HEADS + hd + 1) * LANES]
        vt_ref[hd, :DA_V_DIM, :] = v.T.astype(BF16)
        vt_ref[hd, DA_V_DIM:, :] = ones


def _qkv(x, rows, norm_g, mod, w_in, gqk, ones_bd, rope_tabs):
    nb, rpb, tm = rows.nb, rows.rpb, rows.tm
    assert tm % KEY_CHUNK == 0
    rope = rope_tabs is not None
    in_specs = [rows.rows(D_MODEL), _const_spec((1, D_MODEL)), rows.mod(0), rows.mod(1),
                _const_spec((D_MODEL, 3 * D_MODEL)), _const_spec((2, 1, LANES)),
                _const_spec((LANES, LANES))]
    args = [x, norm_g, mod, mod, w_in, gqk, ones_bd]
    if rope:
        tab = pl.BlockSpec((tm, LANES), lambda b, i: (i, 0))
        in_specs += [tab, tab, tab]
        args += list(rope_tabs)
    out_shape = (jax.ShapeDtypeStruct((nb, DA_HEADS, LANES, rpb), BF16),
                 jax.ShapeDtypeStruct((nb, rpb, D_MODEL), BF16),
                 jax.ShapeDtypeStruct((nb, DA_HEADS, V_ROWS, rpb), BF16))
    out_specs = (pl.BlockSpec((None, DA_HEADS, LANES, tm), lambda b, i: (b, 0, 0, i)),
                 pl.BlockSpec((None, tm, D_MODEL), lambda b, i: (b, i, 0)),
                 pl.BlockSpec((None, DA_HEADS, V_ROWS, tm), lambda b, i: (b, 0, 0, i)))
    return pl.pallas_call(
        functools.partial(_qkv_kernel, rope=rope),
        out_shape=out_shape, grid=rows.grid, in_specs=in_specs, out_specs=out_specs,
        compiler_params=_params(2), name="qkv",
    )(*args)


def _rope_tables(length):
    pos = jnp.arange(length, dtype=jnp.int32)
    row = (pos // GRID_W).astype(F32)
    col = (pos % GRID_W).astype(F32)
    half = DA_HEAD_DIM // 2
    inv_freq = ROPE_THETA ** (-jnp.arange(0, half, 2, dtype=F32) / half)
    lane = jnp.arange(LANES)
    f = lane % 16
    is_col = (lane % DA_HEAD_DIM) >= half
    second = (lane % half) >= 16
    ang = jnp.where(is_col[None, :], col[:, None], row[:, None]) * inv_freq[f][None, :]
    cos, sin = jnp.cos(ang), jnp.sin(ang)
    sin_a = jnp.where(second[None, :], 0.0, -sin)
    sin_b = jnp.where(second[None, :], sin, 0.0)
    return cos, sin_a, sin_b


def _attn_kernel(*refs, n_kv, lam_init, tq):
    lp_ref, gh_ref, qt_ref = refs[:3]
    kv_refs = refs[3:3 + 2 * n_kv]
    o_ref = refs[3 + 2 * n_kv]
    bufs = refs[4 + 2 * n_kv:]
    lp = lp_ref[...]
    lam = (jnp.exp(jnp.sum(lp[0:1] * lp[1:2], axis=-1, keepdims=True))
           - jnp.exp(jnp.sum(lp[2:3] * lp[3:4], axis=-1, keepdims=True)) + lam_init)
    hb = qt_ref.shape[0]
    items = [(hh, j) for hh in range(hb) for j in range(qt_ref.shape[2] // tq)]
    kc = KEY_CHUNK

    key_chunks = []
    for i in range(n_kv):
        for c in range(kv_refs[2 * i].shape[0] // kc):
            key_chunks.append((kv_refs[2 * i], kv_refs[2 * i + 1], c))

    def masked_q(item):
        hh, j = item
        qt = qt_ref[hh, :, j * tq:(j + 1) * tq]
        row = lax.broadcasted_iota(jnp.int32, qt.shape, 0)
        zero = jnp.zeros_like(qt)
        return jnp.concatenate([jnp.where(row < DA_HEAD_DIM, qt, zero),
                                jnp.where(row >= DA_HEAD_DIM, qt, zero)], axis=1)

    def score_chunk(n, hh, qst, s_scr, m):
        k_ref, _, c = key_chunks[n]
        s = jnp.dot(k_ref[c * kc:(c + 1) * kc, hh * LANES:(hh + 1) * LANES], qst,
                    preferred_element_type=F32)
        s_scr[n * kc:(n + 1) * kc, :] = s
        return jnp.maximum(m, s.reshape(kc // 8, 8, 2 * tq).max(axis=0))

    def value_chunk(n, hh, s_scr, m, acc):
        _, vt_ref, c = key_chunks[n]
        r = pl.multiple_of(n * kc + jnp.minimum(pl.program_id(0), 0) * kc, kc)
        p = jnp.exp2((s_scr[pl.ds(r, kc), :] - m).astype(BF16))
        return acc + jnp.dot(vt_ref[hh, :, c * kc:(c + 1) * kc], p, preferred_element_type=F32)

    def finish(item, acc):
        hh, j = item
        o2 = acc[:DA_V_DIM] / acc[DA_V_DIM:DA_V_DIM + 1]
        o = o2[:, :tq] - lam * o2[:, tq:]
        o = o * lax.rsqrt(jnp.mean(o * o, axis=0, keepdims=True) + EPS) * gh_ref[...]
        o_ref[j * tq:(j + 1) * tq, hh * LANES:(hh + 1) * LANES] = (o * (1.0 - lam_init)).T.astype(BF16)

    neg = jnp.full((8, 2 * tq), -jnp.inf, F32)
    qst = masked_q(items[0])
    m8 = neg
    for n in range(len(key_chunks)):
        m8 = score_chunk(n, items[0][0], qst, bufs[0], m8)
    for i, item in enumerate(items):
        nxt = items[i + 1] if i + 1 < len(items) else None
        m = m8.max(axis=0, keepdims=True)
        acc = jnp.zeros((V_ROWS, 2 * tq), F32)
        if nxt is not None:
            qst = masked_q(nxt)
            m8 = neg
        for n in range(len(key_chunks)):
            if nxt is not None:
                m8 = score_chunk(n, nxt[0], qst, bufs[(i + 1) % 2], m8)
            acc = value_chunk(n, item[0], bufs[i % 2], m, acc)
        finish(item, acc)


def _attention(qt, kv_list, lam_p, g_head, lam_init, tq, hb):
    nb, _, _, lq = qt.shape
    in_specs = [_const_spec((4, DA_HEAD_DIM)), _const_spec((DA_V_DIM, 1)),
                pl.BlockSpec((None, hb, LANES, lq), lambda b, h: (b, h, 0, 0))]
    args = [lam_p, g_head, qt]
    total = 0
    for k, vt in kv_list:
        lk = k.shape[1]
        total += lk
        in_specs.append(pl.BlockSpec((None, lk, hb * LANES), lambda b, h: (b, 0, h)))
        in_specs.append(pl.BlockSpec((None, hb, V_ROWS, lk), lambda b, h: (b, h, 0, 0)))
        args += [k, vt]
    n_bufs = min(2, hb * (lq // tq))
    return pl.pallas_call(
        functools.partial(_attn_kernel, n_kv=len(kv_list), lam_init=lam_init, tq=tq),
        out_shape=jax.ShapeDtypeStruct((nb, lq, D_MODEL), BF16),
        grid=(nb, DA_HEADS // hb),
        in_specs=in_specs,
        out_specs=pl.BlockSpec((None, lq, hb * LANES), lambda b, h: (b, 0, h)),
        scratch_shapes=[pltpu.VMEM((total, 2 * tq), F32)] * n_bufs,
        compiler_params=_params(2), name="diff_attn",
    )(*args)


def kernel(x, c, ctx, c_ctx, norm1_g, norm2_g, ada_w, ada_b, mlp_w1, mlp_w2, ev_w_in, ev_w_out, ssm_a_re, ssm_a_im, ssm_log_dt, ssm_b_re, ssm_b_im, ssm_c_re, ssm_c_im, ssm_d, ssm_w_glu, ssm_b_glu, od_w_in, od_w_out, od_q_norm, od_k_norm, od_lambda, od_head_norm):
    nb, length, d = x.shape
    clen = ctx.shape[1]
    assert d == D_MODEL and length % GRID_W == 0
    assert length % SSM_CHUNK == 0 and clen % SSM_CHUNK == 0

    xs = x.reshape(nb * length, d)
    cs = ctx.reshape(nb * clen, d)
    rows_x = _Rows(nb, length, _tile(length, 512), None)
    rows_c = _Rows(1, nb * clen, _tile(nb * clen, 512), nb)
    rows_cb = _Rows(nb, clen, clen, nb)
    chunks_x = _ChunkRows(nb, length // SSM_CHUNK, None)
    chunks_c = _ChunkRows(nb, clen // SSM_CHUNK, nb)

    n_cond = nb + 1
    pad = (-n_cond) % 8
    cond = jnp.concatenate([c, c_ctx[None, :], jnp.zeros((pad, d), F32)], axis=0)
    mods = _adaln(cond, ada_w, ada_b)[:, :n_cond].reshape(DEPTH, n_cond, 1, 6 * d)

    kf = lax.iota(jnp.int32, FOURIER_CH)
    ang = ((kf[:, None] * kf[None, :]) % FOURIER_CH).astype(F32) * (2.0 * math.pi / FOURIER_CH)
    cs_ch = jnp.concatenate([jnp.cos(ang), jnp.sin(ang)], axis=1).astype(BF16)
    dft_x = _dft_mats(length)
    dft_c = _dft_mats(clen)
    rope_tabs = _rope_tables(length)
    lane = jnp.arange(LANES)
    ones_bd = (lane[:, None] // DA_HEAD_DIM == lane[None, :] // DA_HEAD_DIM).astype(BF16)

    ncc = clen // SSM_CHUNK
    ncx = length // SSM_CHUNK
    nct = ncx + ncc

    for i in range(DEPTH):
        last = i == DEPTH - 1
        j = i // 2
        mod = mods[i]
        n1 = norm1_g[i].reshape(1, d)
        n2 = norm2_g[i].reshape(1, d)
        if i % 2 == 0:
            w_in = ev_w_in[j].astype(BF16)
            u_shape = (SLABS, nct, nb, SSM_CHUNK * LANES)
            x3 = xs.reshape(nb, length, d)
            c3 = cs.reshape(nb, clen, d)
            vc_x, vs_x, u = _even_in(x3, chunks_x, n1, mod, w_in, cs_ch, u_shape, 0)
            vc_c, vs_c, u = _even_in(c3, chunks_c, n1, mod, w_in, cs_ch, u_shape, ncx, u)
            ya_x = _fourier(vc_x, vs_x, dft_x)
            m, bm, wh, at_re, at_im = _ssm_ops(ssm_a_re[j], ssm_a_im[j], ssm_log_dt[j], ssm_b_re[j],
                                                ssm_b_im[j], ssm_c_re[j], ssm_c_im[j], ssm_d[j])
            u = u.reshape(SLABS, nct * nb, SSM_CHUNK * LANES)
            contrib = _ssm_contrib(u, bm)
            h_re, h_im = _ssm_scan(contrib, at_re, at_im, ncc, nct, nb)
            y = _ssm_read(u, h_re, h_im, m, wh).reshape(u_shape)
            w_glu = ssm_w_glu[j].astype(BF16)
            b_glu = ssm_b_glu[j].reshape(1, D_SSM)
            w_out = ev_w_out[j].astype(BF16)
            w1 = mlp_w1[i].astype(BF16)
            w2 = mlp_w2[i].astype(BF16)
            xs = _even_tail(ya_x, y, 0, x3, chunks_x, mod, w_glu, b_glu, w_out, n2, w1, w2).reshape(nb * length, d)
            if not last:
                ya_c = _fourier(vc_c, vs_c, dft_c)
                cs = _even_tail(ya_c, y, ncx, c3, chunks_c, mod, w_glu, b_glu, w_out, n2, w1, w2).reshape(nb * clen, d)
        else:
            lam_init = 0.8 - 0.6 * math.exp(-0.3 * i)
            w_in = od_w_in[j].astype(BF16)
            gqk = jnp.stack([jnp.tile(od_q_norm[j], 2) * Q_SCALE,
                             jnp.tile(od_k_norm[j], 2)]).reshape(2, 1, LANES)
            qt_x, k_x, vt_x = _qkv(xs, rows_x, n1, mod, w_in, gqk, ones_bd, rope_tabs)
            qt_c, k_c, vt_c = _qkv(cs, rows_cb, n1, mod, w_in, gqk, ones_bd, None)
            g_head = od_head_norm[j].reshape(DA_V_DIM, 1)
            w_out = od_w_out[j].astype(BF16)
            o_x = _attention(qt_x, [(k_x, vt_x), (k_c, vt_c)], od_lambda[j], g_head, lam_init,
                             _tile(length, 512, LANES), 1)
            w1 = mlp_w1[i].astype(BF16)
            w2 = mlp_w2[i].astype(BF16)
            xs = _odd_tail(o_x, xs, rows_x, mod, w_out, n2, w1, w2)
            if not last:
                o_c = _attention(qt_c, [(k_c, vt_c)], od_lambda[j], g_head, lam_init, clen, DA_HEADS)
                cs = _odd_tail(o_c, cs, rows_c, mod, w_out, n2, w1, w2)
    return xs.reshape(nb, length, d)
```

```python
import functools
import math

import jax
import jax.numpy as jnp
from jax import lax
from jax.experimental import pallas as pl
from jax.experimental.pallas import tpu as pltpu

F32 = jnp.float32
BF16 = jnp.bfloat16

D_MODEL = 1024
DEPTH = 4
GRID_W = 64
EPS = 1e-6
D_FF = 4 * D_MODEL
D_FOURIER = D_MODEL // 2
FOURIER_GROUPS = 4
FOURIER_CH = D_FOURIER // FOURIER_GROUPS
D_SSM = D_MODEL - D_FOURIER
SSM_GROUP = 16
SSM_GROUPS = D_SSM // SSM_GROUP
SSM_STATE = 64
DA_HEAD_DIM = 64
DA_V_DIM = 2 * DA_HEAD_DIM
DA_HEADS = D_MODEL // DA_V_DIM
ROPE_THETA = 10000.0

LANES = 128
SSM_CHUNK = 16
SLABS = D_SSM // LANES
SLAB_GROUPS = LANES // SSM_GROUP
SLAB_STATE = SLAB_GROUPS * SSM_STATE
VMEM_LIMIT = 56 * 1024 * 1024


def _params(n_axes, flags=None):
    return pltpu.CompilerParams(dimension_semantics=("arbitrary",) * n_axes,
                                vmem_limit_bytes=VMEM_LIMIT, flags=flags)


def _tile(n, target, mult=8):
    t = min(n, target)
    while t > 1 and (n % t or t % mult):
        t -= 1
    return t if n % t == 0 else n


def _const_spec(shape):
    nd = len(shape)
    return pl.BlockSpec(shape, lambda *_: (0,) * nd, pipeline_mode=pl.Buffered(1))


def _sigmoid(x):
    return 1.0 / (1.0 + jnp.exp(-x))


def _gelu_tanh(x):
    c = math.sqrt(2.0 / math.pi)
    return x * (0.5 * (1.0 + jnp.tanh(c * (x + 0.044715 * (x * x * x)))))


def _norm_mod(x, g, shift, scale):
    ms = jnp.mean(x * x, axis=-1, keepdims=True)
    y = x * lax.rsqrt(ms + EPS) * g
    return y * (1.0 + scale) + shift


def _adaln_kernel(c_ref, w_ref, b_ref, o_ref):
    c = c_ref[...]
    s = (c * _sigmoid(c)).astype(BF16)
    o_ref[0] = jnp.dot(s, w_ref[0].astype(BF16), preferred_element_type=F32) + b_ref[0]


def _adaln(cond, ada_w, ada_b):
    depth, d, n = ada_w.shape
    r = cond.shape[0]
    tn = _tile(n, 1536, LANES)
    return pl.pallas_call(
        _adaln_kernel,
        out_shape=jax.ShapeDtypeStruct((depth, r, n), F32),
        grid=(depth, n // tn),
        in_specs=[pl.BlockSpec((r, d), lambda l, j: (0, 0)),
                  pl.BlockSpec((1, d, tn), lambda l, j: (l, 0, j)),
                  pl.BlockSpec((1, 1, tn), lambda l, j: (l, 0, j))],
        out_specs=pl.BlockSpec((1, r, tn), lambda l, j: (l, 0, j)),
        compiler_params=_params(2), name="adaln",
    )(cond, ada_w, ada_b.reshape(depth, 1, n))


class _Rows:
    def __init__(self, nb, rpb, tm, mod_row):
        self.nb, self.rpb, self.tm = nb, rpb, tm
        self.nt = rpb // tm
        self.grid = (nb, self.nt)
        self.mod_row = mod_row

    def rows(self, cols):
        nt = self.nt
        return pl.BlockSpec((self.tm, cols), lambda b, i: (b * nt + i, 0))

    def rows3(self, per, cols):
        nt, tm = self.nt, self.tm
        if tm <= per:
            k = per // tm
            return pl.BlockSpec((None, tm, cols), lambda b, i: ((b * nt + i) // k, (b * nt + i) % k, 0))
        return pl.BlockSpec((tm // per, per, cols), lambda b, i: (b * nt + i, 0, 0))

    def mod(self, k):
        if self.mod_row is None:
            return pl.BlockSpec((None, 1, D_MODEL), lambda b, i: (b, 0, k))
        r = self.mod_row
        return pl.BlockSpec((None, 1, D_MODEL), lambda b, i: (r, 0, k))


class _ChunkRows:
    def __init__(self, nb, n_chunks, mod_row):
        self.nb, self.grid, self.mod_row = nb, (n_chunks,), mod_row
        self.tm = nb * SSM_CHUNK

    def rows(self, cols):
        return pl.BlockSpec((self.nb, SSM_CHUNK, cols), lambda c: (0, c, 0))

    def mod(self, k):
        if self.mod_row is None:
            return pl.BlockSpec((self.nb, 1, D_MODEL), lambda c: (0, 0, k))
        r = self.mod_row
        return pl.BlockSpec((1, 1, D_MODEL), lambda c: (r, 0, k))

    def chunk(self, first_chunk):
        return pl.BlockSpec((SLABS, None, self.nb, SSM_CHUNK * LANES), lambda c: (0, first_chunk + c, 0, 0))


def _even_in_kernel(*refs, aliased):
    if aliased:
        refs = refs[1:]
    x_ref, g_ref, sh_ref, sc_ref, w_ref, cs_ref, vc_ref, vs_ref, u_ref = refs[:9]
    zs_scr = refs[9:]
    nb, t_len, _ = x_ref.shape
    h = _norm_mod(x_ref[...], g_ref[...], sh_ref[...], sc_ref[...])
    z = jnp.dot(h.reshape(nb * t_len, D_MODEL).astype(BF16), w_ref[...], preferred_element_type=F32)
    zf = z[:, :D_FOURIER].astype(BF16)
    vc, vs = [], []
    for gi in range(FOURIER_GROUPS):
        v = jnp.dot(zf[:, gi * FOURIER_CH:(gi + 1) * FOURIER_CH], cs_ref[...], preferred_element_type=F32)
        vc.append(v[:, :FOURIER_CH].astype(BF16))
        vs.append(v[:, FOURIER_CH:].astype(BF16))
    vc_ref[...] = jnp.concatenate(vc, axis=1).reshape(nb, t_len, D_FOURIER)
    vs_ref[...] = jnp.concatenate(vs, axis=1).reshape(nb, t_len, D_FOURIER)
    for s in range(SLABS):
        zs_scr[s][...] = z[:, D_FOURIER + s * LANES:D_FOURIER + (s + 1) * LANES]
        for t in range(t_len):
            u_ref[s, :, t * LANES:(t + 1) * LANES] = zs_scr[s][pl.ds(t, nb, stride=t_len), :].astype(BF16)


def _even_in(x, rows, norm_g, mod, w_in, cs, u_shape, first_chunk, u=None):
    nb, n_tok, _ = x.shape
    outs = (jax.ShapeDtypeStruct((nb, n_tok, D_FOURIER), BF16), jax.ShapeDtypeStruct((nb, n_tok, D_FOURIER), BF16),
            jax.ShapeDtypeStruct(u_shape, BF16))
    in_specs = [rows.rows(D_MODEL), _const_spec((1, D_MODEL)), rows.mod(0), rows.mod(1),
                _const_spec((D_MODEL, D_MODEL)), _const_spec((FOURIER_CH, 2 * FOURIER_CH))]
    args = [x, norm_g, mod, mod, w_in, cs]
    aliases = {}
    if u is not None:
        in_specs = [pl.BlockSpec(memory_space=pl.ANY)] + in_specs
        args = [u] + args
        aliases = {0: 2}
    return pl.pallas_call(
        functools.partial(_even_in_kernel, aliased=u is not None), out_shape=outs, grid=rows.grid,
        in_specs=in_specs,
        out_specs=(rows.rows(D_FOURIER), rows.rows(D_FOURIER), rows.chunk(first_chunk)),
        scratch_shapes=[pltpu.VMEM((rows.tm, LANES), F32)] * SLABS,
        input_output_aliases=aliases,
        compiler_params=_params(1), name="even_in",
    )(*args)


DFT_PAD = 16


def _fourier_kernel(cl_ref, sl_ref, rev_ref, vc_ref, vs_ref, o_ref, *, scale):
    half = rev_ref.shape[0]
    a = jnp.dot(cl_ref[...], vc_ref[0], preferred_element_type=F32)
    b = jnp.dot(sl_ref[...], vs_ref[0], preferred_element_type=F32)
    o_ref[0, :half, :] = ((a[:half] - b[:half]) * scale).astype(BF16)
    mirrored = ((a + b) * scale).astype(BF16)
    o_ref[0, half:, :] = jnp.dot(rev_ref[...], mirrored, preferred_element_type=F32).astype(BF16)


def _fourier(vc3, vs3, mats):
    cl, sl, rev = mats
    nb, length, _ = vc3.shape
    scale = 1.0 / math.sqrt(length * FOURIER_CH)
    blk = pl.BlockSpec((1, length, D_FOURIER), lambda b: (b, 0, 0))
    return pl.pallas_call(
        functools.partial(_fourier_kernel, scale=scale),
        out_shape=jax.ShapeDtypeStruct((nb, length, D_FOURIER), BF16),
        grid=(nb,),
        in_specs=[_const_spec(cl.shape), _const_spec(sl.shape), _const_spec(rev.shape), blk, blk],
        out_specs=blk,
        compiler_params=_params(1), name="fourier",
    )(cl, sl, rev, vc3, vs3)


def _dft_mats(n):
    half = n // 2
    k = lax.iota(jnp.int32, half + DFT_PAD)
    t = lax.iota(jnp.int32, n)
    ang = ((k[:, None] * t[None, :]) % n).astype(F32) * (2.0 * math.pi / n)
    cl = jnp.where((k <= half)[:, None], jnp.cos(ang), 0.0)
    sl = jnp.where((k < half)[:, None], jnp.sin(ang), 0.0)
    j = lax.iota(jnp.int32, half)
    rev = k[None, :] == (half - j)[:, None]
    return cl.astype(BF16), sl.astype(BF16), rev.astype(BF16)


def _ssm_ops_kernel(ar_ref, ai_ref, ldt_ref, btr_ref, bti_ref, ctr_ref, cti_ref,
                    bm_ref, wh_ref, atr_ref, ati_ref, e_ref):
    t = SSM_CHUNK
    n = SLAB_STATE
    rows = lax.broadcasted_iota(jnp.int32, (LANES, n), 0)
    cols = lax.broadcasted_iota(jnp.int32, (LANES, n), 1)
    same_group = (rows // SSM_GROUP) == (cols // SSM_STATE)
    lag = lax.broadcasted_iota(jnp.int32, (t + 8, n), 0).astype(F32)
    nt_dims = (((1,), (1,)), ((), ()))
    for d in range(2):
        ar, ai = ar_ref[d], ai_ref[d]
        dt = jnp.exp(ldt_ref[d])
        xr, xi = ar * dt, ai * dt
        mag = jnp.exp(lag * xr)
        pr = mag * jnp.cos(lag * xi)
        pi = mag * jnp.sin(lag * xi)
        abr, abi = pr[1:2], pi[1:2]
        den = ar * ar + ai * ai
        fr = ((abr - 1.0) * ar + abi * ai) / den
        fi = (abi * ar - (abr - 1.0) * ai) / den
        br = jnp.where(same_group, btr_ref[d], 0.0)
        bi = jnp.where(same_group, bti_ref[d], 0.0)
        bbr = fr * br - fi * bi
        bbi = fr * bi + fi * br
        cr = jnp.where(same_group, ctr_ref[d], 0.0)
        ci = jnp.where(same_group, cti_ref[d], 0.0)
        for k in range(t):
            pkr, pki = pr[k:k + 1], pi[k:k + 1]
            x_r = bbr * pkr - bbi * pki
            x_i = bbr * pki + bbi * pkr
            s = t - 1 - k if d == 0 else k
            bm_ref[s * LANES:(s + 1) * LANES, 2 * d * n:(2 * d + 1) * n] = x_r.astype(BF16)
            bm_ref[s * LANES:(s + 1) * LANES, (2 * d + 1) * n:(2 * d + 2) * n] = x_i.astype(BF16)
            e_ref[d, k] = (lax.dot_general(x_r, cr, nt_dims, precision=lax.Precision.HIGHEST,
                                           preferred_element_type=F32)
                           - lax.dot_general(x_i, ci, nt_dims, precision=lax.Precision.HIGHEST,
                                             preferred_element_type=F32))
        for k in range(1, t + 1):
            pkr, pki = pr[k:k + 1], pi[k:k + 1]
            w_r = cr * pkr - ci * pki
            w_i = cr * pki + ci * pkr
            tt = k - 1 if d == 0 else t - k
            wh_ref[2 * d * n:(2 * d + 1) * n, tt * LANES:(tt + 1) * LANES] = w_r.T.astype(BF16)
            wh_ref[(2 * d + 1) * n:(2 * d + 2) * n, tt * LANES:(tt + 1) * LANES] = (-w_i).T.astype(BF16)
        atr_ref[d] = pr[t:t + 1]
        ati_ref[d] = pi[t:t + 1]


def _ssm_toeplitz_kernel(e_ref, dsk_ref, m_ref):
    t = SSM_CHUNK
    s = pl.program_id(1)
    r = lax.broadcasted_iota(jnp.int32, (LANES, LANES), 0)
    c = lax.broadcasted_iota(jnp.int32, (LANES, LANES), 1)
    skip = jnp.where(r == c, jnp.broadcast_to(dsk_ref[...], (LANES, LANES)), 0.0)
    for tt in range(t):
        blk = ((tt >= s).astype(F32) * e_ref[0, jnp.maximum(tt - s, 0)]
               + (s >= tt).astype(F32) * e_ref[1, jnp.maximum(s - tt, 0)]
               + (s == tt).astype(F32) * skip)
        m_ref[:, tt * LANES:(tt + 1) * LANES] = blk.astype(BF16)


def _ssm_ops(a_re, a_im, log_dt, b_re, b_im, c_re, c_im, d_skip):
    t = SSM_CHUNK
    n = SLAB_STATE

    def per_state(x):
        return x.astype(F32).reshape(2, SLABS, 1, n)

    def per_channel(x):
        x = x.astype(F32).reshape(2, SLABS, LANES, SSM_STATE)
        return jnp.tile(x, (1, 1, 1, SLAB_GROUPS))

    ldt = jnp.broadcast_to(log_dt[..., None], a_re.shape)
    vec = pl.BlockSpec((2, None, 1, n), lambda z: (0, z, 0, 0))
    mat = pl.BlockSpec((2, None, LANES, n), lambda z: (0, z, 0, 0))
    bm, wh, at_re, at_im, e = pl.pallas_call(
        _ssm_ops_kernel,
        out_shape=(jax.ShapeDtypeStruct((SLABS, t * LANES, 4 * n), BF16),
                   jax.ShapeDtypeStruct((SLABS, 4 * n, t * LANES), BF16),
                   jax.ShapeDtypeStruct((SLABS, 2, 1, n), F32),
                   jax.ShapeDtypeStruct((SLABS, 2, 1, n), F32),
                   jax.ShapeDtypeStruct((SLABS, 2, t, LANES, LANES), F32)),
        grid=(SLABS,),
        in_specs=[vec, vec, vec, mat, mat, mat, mat],
        out_specs=(pl.BlockSpec((None, t * LANES, 4 * n), lambda z: (z, 0, 0)),
                   pl.BlockSpec((None, 4 * n, t * LANES), lambda z: (z, 0, 0)),
                   pl.BlockSpec((None, 2, 1, n), lambda z: (z, 0, 0, 0)),
                   pl.BlockSpec((None, 2, 1, n), lambda z: (z, 0, 0, 0)),
                   pl.BlockSpec((None, 2, t, LANES, LANES), lambda z: (z, 0, 0, 0, 0))),
        compiler_params=_params(1), name="ssm_ops",
    )(per_state(a_re), per_state(a_im), per_state(ldt),
      per_channel(b_re.transpose(0, 1, 3, 2)), per_channel(b_im.transpose(0, 1, 3, 2)),
      per_channel(c_re), per_channel(c_im))
    m = pl.pallas_call(
        _ssm_toeplitz_kernel,
        out_shape=jax.ShapeDtypeStruct((SLABS, t * LANES, t * LANES), BF16),
        grid=(SLABS, t),
        in_specs=[pl.BlockSpec((None, 2, t, LANES, LANES), lambda z, s: (z, 0, 0, 0, 0)),
                  pl.BlockSpec((None, 1, LANES), lambda z, s: (z, 0, 0))],
        out_specs=pl.BlockSpec((None, LANES, t * LANES), lambda z, s: (z, s, 0)),
        compiler_params=_params(2), name="ssm_toeplitz",
    )(e, d_skip.astype(F32).reshape(SLABS, 1, LANES))
    return m, bm, wh, at_re.reshape(SLABS * 2, 1, n), at_im.reshape(SLABS * 2, 1, n)


def _ssm_contrib_kernel(u_ref, bm_ref, o_ref):
    o_ref[...] = jnp.dot(u_ref[...], bm_ref[...], preferred_element_type=F32)


def _ssm_contrib(u, bm):
    slabs, r, k = u.shape
    n = bm.shape[2]
    tr = _tile(r, 576)
    return pl.pallas_call(
        _ssm_contrib_kernel,
        out_shape=jax.ShapeDtypeStruct((slabs, r, n), F32),
        grid=(slabs, r // tr),
        in_specs=[pl.BlockSpec((None, tr, k), lambda s, i: (s, i, 0)),
                  pl.BlockSpec((None, k, n), lambda s, i: (s, 0, 0))],
        out_specs=pl.BlockSpec((None, tr, n), lambda s, i: (s, i, 0)),
        compiler_params=_params(2), name="ssm_contrib",
    )(u, bm)


SCAN_UNROLL = 8


def _ssm_scan_kernel(ar_ref, ai_ref, cr_ref, ci_ref, hr_ref, hi_ref, *, ncc, nct, rb):
    d = pl.program_id(1)
    lt = ar_ref.shape[-1]
    ncx = nct - ncc
    ar = jnp.broadcast_to(ar_ref[...], (rb, lt))
    ai = jnp.broadcast_to(ai_ref[...], (rb, lt))
    zero = jnp.zeros((rb, lt), F32)

    def advance(i_src, i_dst, hr, hi):
        r0 = pl.multiple_of(i_src * rb, rb)
        r1 = pl.multiple_of(i_dst * rb, rb)
        nr = ar * hr - ai * hi + cr_ref[pl.ds(r0, rb), :]
        ni = ar * hi + ai * hr + ci_ref[pl.ds(r0, rb), :]
        hr_ref[pl.ds(r1, rb), :] = nr.astype(BF16)
        hi_ref[pl.ds(r1, rb), :] = ni.astype(BF16)
        return nr, ni

    def start(i):
        hr_ref[pl.ds(i * rb, rb), :] = zero.astype(BF16)
        hi_ref[pl.ds(i * rb, rb), :] = zero.astype(BF16)
        return zero, zero

    @pl.when(d == 0)
    def _():
        c = start(ncx)
        c = lax.fori_loop(0, ncc - 1, lambda k, c: advance(ncx + k, ncx + k + 1, *c), c, unroll=SCAN_UNROLL)
        c = advance(nct - 1, 0, *c)
        lax.fori_loop(0, ncx - 1, lambda k, c: advance(k, k + 1, *c), c, unroll=SCAN_UNROLL)

    @pl.when(d == 1)
    def _():
        c = start(nct - 1)
        c = lax.fori_loop(0, ncc - 1, lambda k, c: advance(nct - 1 - k, nct - 2 - k, *c), c, unroll=SCAN_UNROLL)
        c = advance(ncx, ncx - 1, *c)
        lax.fori_loop(0, ncx - 1, lambda k, c: advance(ncx - 1 - k, ncx - 2 - k, *c), c, unroll=SCAN_UNROLL)


def _ssm_scan(contrib, at_re, at_im, ncc, nct, rb):
    slabs, r, n = contrib.shape
    lt = 2 * LANES
    q = SLAB_STATE // lt
    re_spec = pl.BlockSpec((None, r, lt), lambda s, d, j: (s, 0, d * 2 * q + j))
    im_spec = pl.BlockSpec((None, r, lt), lambda s, d, j: (s, 0, d * 2 * q + q + j))
    a_spec = pl.BlockSpec((None, 1, lt), lambda s, d, j: (s * 2 + d, 0, j))
    h_spec = pl.BlockSpec((None, r, lt), lambda s, d, j: (s, 0, d * q + j))
    out = jax.ShapeDtypeStruct((slabs, r, n // 2), BF16)
    return pl.pallas_call(
        functools.partial(_ssm_scan_kernel, ncc=ncc, nct=nct, rb=rb),
        out_shape=(out, out),
        grid=(slabs, 2, q),
        in_specs=[a_spec, a_spec, re_spec, im_spec],
        out_specs=(h_spec, h_spec),
        compiler_params=_params(3), name="ssm_scan",
    )(at_re, at_im, contrib, contrib)


def _ssm_read_kernel(u_ref, hr_ref, hi_ref, m_ref, wh_ref, o_ref):
    q = SLAB_STATE
    acc = jnp.dot(u_ref[...], m_ref[...], preferred_element_type=F32)
    for d in range(2):
        acc += jnp.dot(hr_ref[:, d * q:(d + 1) * q], wh_ref[2 * d * q:(2 * d + 1) * q, :],
                       preferred_element_type=F32)
        acc += jnp.dot(hi_ref[:, d * q:(d + 1) * q], wh_ref[(2 * d + 1) * q:(2 * d + 2) * q, :],
                       preferred_element_type=F32)
    o_ref[...] = acc.astype(BF16)


def _ssm_read(u, h_re, h_im, m, wh):
    slabs, r, k = u.shape
    ns = h_re.shape[2]
    nw = wh.shape[1]
    tr = _tile(r, 576)
    tn = k // 2
    return pl.pallas_call(
        _ssm_read_kernel,
        out_shape=jax.ShapeDtypeStruct((slabs, r, k), BF16),
        grid=(slabs, 2, r // tr),
        in_specs=[pl.BlockSpec((None, tr, k), lambda s, j, i: (s, i, 0)),
                  pl.BlockSpec((None, tr, ns), lambda s, j, i: (s, i, 0)),
                  pl.BlockSpec((None, tr, ns), lambda s, j, i: (s, i, 0)),
                  pl.BlockSpec((None, k, tn), lambda s, j, i: (s, 0, j)),
                  pl.BlockSpec((None, nw, tn), lambda s, j, i: (s, 0, j))],
        out_specs=pl.BlockSpec((None, tr, tn), lambda s, j, i: (s, i, j)),
        compiler_params=_params(3), name="ssm_read",
    )(u, h_re, h_im, m, wh)


MLP_FF_CHUNK = 1024


def _mlp_tail(x, g_ref, sh_ref, sc_ref, gate_ref, w1_ref, w2_ref, o_ref):
    fc = MLP_FF_CHUNK
    h = _norm_mod(x, g_ref[...], sh_ref[...], sc_ref[...]).reshape(-1, D_MODEL).astype(BF16)
    acc = jnp.zeros(h.shape, F32)
    for k in range(D_FF // fc):
        a = jnp.dot(h, w1_ref[:, k * fc:(k + 1) * fc], preferred_element_type=F32)
        a = jnp.square(jnp.maximum(a, 0.0)).astype(BF16)
        acc += jnp.dot(a, w2_ref[k * fc:(k + 1) * fc, :], preferred_element_type=F32)
    o_ref[...] = x + gate_ref[...] * acc.reshape(x.shape)


def _even_tail_kernel(ya_ref, y_ref, x_ref, gate1_ref, wg_ref, bg_ref, wa_ref, wb_ref,
                      g_ref, sh_ref, sc_ref, gate2_ref, w1_ref, w2_ref, o_ref, *ys_scr):
    nb, t_len, _ = x_ref.shape
    for s in range(SLABS):
        for t in range(t_len):
            ys_scr[s][pl.ds(t, nb, stride=t_len), :] = y_ref[s, :, t * LANES:(t + 1) * LANES].astype(F32)
    y = _gelu_tanh(jnp.concatenate([r[...] for r in ys_scr], axis=1))
    glu = y * _sigmoid(jnp.dot(y.astype(BF16), wg_ref[...], preferred_element_type=F32) + bg_ref[...])
    acc = jnp.dot(ya_ref[...].reshape(nb * t_len, D_FOURIER), wa_ref[...], preferred_element_type=F32)
    acc += jnp.dot(glu.astype(BF16), wb_ref[...], preferred_element_type=F32)
    x1 = x_ref[...] + gate1_ref[...] * acc.reshape(nb, t_len, D_MODEL)
    _mlp_tail(x1, g_ref, sh_ref, sc_ref, gate2_ref, w1_ref, w2_ref, o_ref)


def _odd_tail_kernel(a_ref, x_ref, gate1_ref, wo_ref, g_ref, sh_ref, sc_ref, gate2_ref, w1_ref, w2_ref, o_ref):
    a = a_ref[...].reshape(x_ref.shape)
    x1 = x_ref[...] + gate1_ref[...] * jnp.dot(a, wo_ref[...], preferred_element_type=F32)
    _mlp_tail(x1, g_ref, sh_ref, sc_ref, gate2_ref, w1_ref, w2_ref, o_ref)


def _mlp_specs(rows):
    return [_const_spec((1, D_MODEL)), rows.mod(3), rows.mod(4), rows.mod(5),
            _const_spec((D_MODEL, D_FF)), _const_spec((D_FF, D_MODEL))]


def _even_tail(ya, y, first_chunk, x, rows, mod, w_glu, b_glu, w_out, norm_g, w1, w2):
    return pl.pallas_call(
        _even_tail_kernel, out_shape=jax.ShapeDtypeStruct(x.shape, F32), grid=rows.grid,
        in_specs=[rows.rows(D_FOURIER), rows.chunk(first_chunk), rows.rows(D_MODEL), rows.mod(2),
                  _const_spec((D_SSM, D_SSM)), _const_spec((1, D_SSM)),
                  _const_spec((D_FOURIER, D_MODEL)), _const_spec((D_SSM, D_MODEL))] + _mlp_specs(rows),
        out_specs=rows.rows(D_MODEL),
        scratch_shapes=[pltpu.VMEM((rows.tm, LANES), F32)] * SLABS,
        compiler_params=_params(1), name="even_tail",
    )(ya, y, x, mod, w_glu, b_glu, w_out[:D_FOURIER], w_out[D_FOURIER:], norm_g, mod, mod, mod, w1, w2)


def _odd_tail(a, x, rows, mod, w_out, norm_g, w1, w2):
    return pl.pallas_call(
        _odd_tail_kernel, out_shape=jax.ShapeDtypeStruct(x.shape, F32), grid=rows.grid,
        in_specs=[rows.rows3(a.shape[1], D_MODEL), rows.rows(D_MODEL), rows.mod(2),
                  _const_spec((D_MODEL, D_MODEL))] + _mlp_specs(rows),
        out_specs=rows.rows(D_MODEL),
        compiler_params=_params(2), name="odd_tail",
    )(a, x, mod, w_out, norm_g, mod, mod, mod, w1, w2)


KEY_CHUNK = 256
V_ROWS = DA_V_DIM + 16
Q_SCALE = DA_HEAD_DIM ** -0.5 * math.log2(math.e)


def _qkv_kernel(*refs, rope):
    if rope:
        (x_ref, g_ref, sh_ref, sc_ref, w_ref, gqk_ref, ones_ref,
         cos_ref, sa_ref, sb_ref, qt_ref, k_ref, vt_ref) = refs
    else:
        x_ref, g_ref, sh_ref, sc_ref, w_ref, gqk_ref, ones_ref, qt_ref, k_ref, vt_ref = refs
    h = _norm_mod(x_ref[...], g_ref[...], sh_ref[...], sc_ref[...]).astype(BF16)
    y = jnp.dot(h, w_ref[...], preferred_element_type=F32)
    tm = y.shape[0]
    for s in range(2 * DA_HEADS):
        blk = y[:, s * LANES:(s + 1) * LANES]
        ssq = jnp.dot((blk * blk).astype(BF16), ones_ref[...], preferred_element_type=F32)
        nrm = blk * gqk_ref[s // DA_HEADS]
        if rope:
            nrm = (nrm * cos_ref[...] + pltpu.roll(nrm, LANES - 16, 1) * sa_ref[...]
                   + pltpu.roll(nrm, 16, 1) * sb_ref[...])
        nrm = nrm * lax.rsqrt(ssq * (1.0 / DA_HEAD_DIM) + EPS)
        if s < DA_HEADS:
            qt_ref[s] = nrm.T.astype(BF16)
        else:
            hd = s - DA_HEADS
            k_ref[:, hd * LANES:(hd + 1) * LANES] = nrm.astype(BF16)
    ones = jnp.ones((V_ROWS - DA_V_DIM, tm), BF16)
    for hd in range(DA_HEADS):
        v = y[:, (2 * DA_HEADS + hd) * LANES:(2 * DA_HEADS + hd + 1) * LANES]
        vt_ref[hd, :DA_V_DIM, :] = v.T.astype(BF16)
        vt_ref[hd, DA_V_DIM:, :] = ones


def _qkv(x, rows, norm_g, mod, w_in, gqk, ones_bd, rope_tabs):
    nb, rpb, tm = rows.nb, rows.rpb, rows.tm
    assert tm % KEY_CHUNK == 0
    rope = rope_tabs is not None
    in_specs = [rows.rows(D_MODEL), _const_spec((1, D_MODEL)), rows.mod(0), rows.mod(1),
                _const_spec((D_MODEL, 3 * D_MODEL)), _const_spec((2, 1, LANES)),
                _const_spec((LANES, LANES))]
    args = [x, norm_g, mod, mod, w_in, gqk, ones_bd]
    if rope:
        tab = pl.BlockSpec((tm, LANES), lambda b, i: (i, 0))
        in_specs += [tab, tab, tab]
        args += list(rope_tabs)
    out_shape = (jax.ShapeDtypeStruct((nb, DA_HEADS, LANES, rpb), BF16),
                 jax.ShapeDtypeStruct((nb, rpb, D_MODEL), BF16),
                 jax.ShapeDtypeStruct((nb, DA_HEADS, V_ROWS, rpb), BF16))
    out_specs = (pl.BlockSpec((None, DA_HEADS, LANES, tm), lambda b, i: (b, 0, 0, i)),
                 pl.BlockSpec((None, tm, D_MODEL), lambda b, i: (b, i, 0)),
                 pl.BlockSpec((None, DA_HEADS, V_ROWS, tm), lambda b, i: (b, 0, 0, i)))
    return pl.pallas_call(
        functools.partial(_qkv_kernel, rope=rope),
        out_shape=out_shape, grid=rows.grid, in_specs=in_specs, out_specs=out_specs,
        compiler_params=_params(2), name="qkv",
    )(*args)


def _rope_tables(length):
    pos = jnp.arange(length, dtype=jnp.int32)
    row = (pos // GRID_W).astype(F32)
    col = (pos % GRID_W).astype(F32)
    half = DA_HEAD_DIM // 2
    inv_freq = ROPE_THETA ** (-jnp.arange(0, half, 2, dtype=F32) / half)
    lane = jnp.arange(LANES)
    f = lane % 16
    is_col = (lane % DA_HEAD_DIM) >= half
    second = (lane % half) >= 16
    ang = jnp.where(is_col[None, :], col[:, None], row[:, None]) * inv_freq[f][None, :]
    cos, sin = jnp.cos(ang), jnp.sin(ang)
    sin_a = jnp.where(second[None, :], 0.0, -sin)
    sin_b = jnp.where(second[None, :], sin, 0.0)
    return cos, sin_a, sin_b


def _attn_kernel(*refs, n_kv, lam_init, tq):
    lp_ref, gh_ref, qt_ref = refs[:3]
    kv_refs = refs[3:3 + 2 * n_kv]
    o_ref = refs[3 + 2 * n_kv]
    bufs = refs[4 + 2 * n_kv:]
    lp = lp_ref[...]
    lam = (jnp.exp(jnp.sum(lp[0:1] * lp[1:2], axis=-1, keepdims=True))
           - jnp.exp(jnp.sum(lp[2:3] * lp[3:4], axis=-1, keepdims=True)) + lam_init)
    hb = qt_ref.shape[0]
    items = [(hh, j) for hh in range(hb) for j in range(qt_ref.shape[2] // tq)]
    kc = KEY_CHUNK

    key_chunks = []
    for i in range(n_kv):
        for c in range(kv_refs[2 * i].shape[0] // kc):
            key_chunks.append((kv_refs[2 * i], kv_refs[2 * i + 1], c))

    def masked_q(item):
        hh, j = item
        qt = qt_ref[hh, :, j * tq:(j + 1) * tq]
        row = lax.broadcasted_iota(jnp.int32, qt.shape, 0)
        zero = jnp.zeros_like(qt)
        return jnp.concatenate([jnp.where(row < DA_HEAD_DIM, qt, zero),
                                jnp.where(row >= DA_HEAD_DIM, qt, zero)], axis=1)

    def score_chunk(n, hh, qst, s_scr, m):
        k_ref, _, c = key_chunks[n]
        s = jnp.dot(k_ref[c * kc:(c + 1) * kc, hh * LANES:(hh + 1) * LANES], qst,
                    preferred_element_type=F32)
        s_scr[n * kc:(n + 1) * kc, :] = s
        return jnp.maximum(m, s.reshape(kc // 8, 8, 2 * tq).max(axis=0))

    def value_chunk(n, hh, s_scr, m, acc):
        _, vt_ref, c = key_chunks[n]
        r = pl.multiple_of(n * kc + jnp.minimum(pl.program_id(0), 0) * kc, kc)
        p = jnp.exp2((s_scr[pl.ds(r, kc), :] - m).astype(BF16))
        return acc + jnp.dot(vt_ref[hh, :, c * kc:(c + 1) * kc], p, preferred_element_type=F32)

    def finish(item, acc):
        hh, j = item
        o2 = acc[:DA_V_DIM] / acc[DA_V_DIM:DA_V_DIM + 1]
        o = o2[:, :tq] - lam * o2[:, tq:]
        o = o * lax.rsqrt(jnp.mean(o * o, axis=0, keepdims=True) + EPS) * gh_ref[...]
        o_ref[j * tq:(j + 1) * tq, hh * LANES:(hh + 1) * LANES] = (o * (1.0 - lam_init)).T.astype(BF16)

    neg = jnp.full((8, 2 * tq), -jnp.inf, F32)
    qst = masked_q(items[0])
    m8 = neg
    for n in range(len(key_chunks)):
        m8 = score_chunk(n, items[0][0], qst, bufs[0], m8)
    for i, item in enumerate(items):
        nxt = items[i + 1] if i + 1 < len(items) else None
        m = m8.max(axis=0, keepdims=True)
        acc = jnp.zeros((V_ROWS, 2 * tq), F32)
        if nxt is not None:
            qst = masked_q(nxt)
            m8 = neg
        for n in range(len(key_chunks)):
            if nxt is not None:
                m8 = score_chunk(n, nxt[0], qst, bufs[(i + 1) % 2], m8)
            acc = value_chunk(n, item[0], bufs[i % 2], m, acc)
        finish(item, acc)


def _attention(qt, kv_list, lam_p, g_head, lam_init, tq, hb):
    nb, _, _, lq = qt.shape
    in_specs = [_const_spec((4, DA_HEAD_DIM)), _const_spec((DA_V_DIM, 1)),
                pl.BlockSpec((None, hb, LANES, lq), lambda b, h: (b, h, 0, 0))]
    args = [lam_p, g_head, qt]
    total = 0
    for k, vt in kv_list:
        lk = k.shape[1]
        total += lk
        in_specs.append(pl.BlockSpec((None, lk, hb * LANES), lambda b, h: (b, 0, h)))
        in_specs.append(pl.BlockSpec((None, hb, V_ROWS, lk), lambda b, h: (b, h, 0, 0)))
        args += [k, vt]
    n_bufs = min(2, hb * (lq // tq))
    return pl.pallas_call(
        functools.partial(_attn_kernel, n_kv=len(kv_list), lam_init=lam_init, tq=tq),
        out_shape=jax.ShapeDtypeStruct((nb, lq, D_MODEL), BF16),
        grid=(nb, DA_HEADS // hb),
        in_specs=in_specs,
        out_specs=pl.BlockSpec((None, lq, hb * LANES), lambda b, h: (b, 0, h)),
        scratch_shapes=[pltpu.VMEM((total, 2 * tq), F32)] * n_bufs,
        compiler_params=_params(2), name="diff_attn",
    )(*args)


def kernel(x, c, ctx, c_ctx, norm1_g, norm2_g, ada_w, ada_b, mlp_w1, mlp_w2, ev_w_in, ev_w_out, ssm_a_re, ssm_a_im, ssm_log_dt, ssm_b_re, ssm_b_im, ssm_c_re, ssm_c_im, ssm_d, ssm_w_glu, ssm_b_glu, od_w_in, od_w_out, od_q_norm, od_k_norm, od_lambda, od_head_norm):
    nb, length, d = x.shape
    clen = ctx.shape[1]
    assert d == D_MODEL and length % GRID_W == 0
    assert length % SSM_CHUNK == 0 and clen % SSM_CHUNK == 0

    xs = x.reshape(nb * length, d)
    cs = ctx.reshape(nb * clen, d)
    rows_x = _Rows(nb, length, _tile(length, 512), None)
    rows_c = _Rows(1, nb * clen, _tile(nb * clen, 512), nb)
    rows_cb = _Rows(nb, clen, clen, nb)
    chunks_x = _ChunkRows(nb, length // SSM_CHUNK, None)
    chunks_c = _ChunkRows(nb, clen // SSM_CHUNK, nb)

    n_cond = nb + 1
    pad = (-n_cond) % 8
    cond = jnp.concatenate([c, c_ctx[None, :], jnp.zeros((pad, d), F32)], axis=0)
    mods = _adaln(cond, ada_w, ada_b)[:, :n_cond].reshape(DEPTH, n_cond, 1, 6 * d)

    kf = lax.iota(jnp.int32, FOURIER_CH)
    ang = ((kf[:, None] * kf[None, :]) % FOURIER_CH).astype(F32) * (2.0 * math.pi / FOURIER_CH)
    cs_ch = jnp.concatenate([jnp.cos(ang), jnp.sin(ang)], axis=1).astype(BF16)
    dft_x = _dft_mats(length)
    dft_c = _dft_mats(clen)
    rope_tabs = _rope_tables(length)
    lane = jnp.arange(LANES)
    ones_bd = (lane[:, None] // DA_HEAD_DIM == lane[None, :] // DA_HEAD_DIM).astype(BF16)

    ncc = clen // SSM_CHUNK
    ncx = length // SSM_CHUNK
    nct = ncx + ncc

    for i in range(DEPTH):
        last = i == DEPTH - 1
        j = i // 2
        mod = mods[i]
        n1 = norm1_g[i].reshape(1, d)
        n2 = norm2_g[i].reshape(1, d)
        if i % 2 == 0:
            w_in = ev_w_in[j].astype(BF16)
            u_shape = (SLABS, nct, nb, SSM_CHUNK * LANES)
            x3 = xs.reshape(nb, length, d)
            c3 = cs.reshape(nb, clen, d)
            vc_x, vs_x, u = _even_in(x3, chunks_x, n1, mod, w_in, cs_ch, u_shape, 0)
            vc_c, vs_c, u = _even_in(c3, chunks_c, n1, mod, w_in, cs_ch, u_shape, ncx, u)
            ya_x = _fourier(vc_x, vs_x, dft_x)
            m, bm, wh, at_re, at_im = _ssm_ops(ssm_a_re[j], ssm_a_im[j], ssm_log_dt[j], ssm_b_re[j],
                                                ssm_b_im[j], ssm_c_re[j], ssm_c_im[j], ssm_d[j])
            u = u.reshape(SLABS, nct * nb, SSM_CHUNK * LANES)
            contrib = _ssm_contrib(u, bm)
            h_re, h_im = _ssm_scan(contrib, at_re, at_im, ncc, nct, nb)
            y = _ssm_read(u, h_re, h_im, m, wh).reshape(u_shape)
            w_glu = ssm_w_glu[j].astype(BF16)
            b_glu = ssm_b_glu[j].reshape(1, D_SSM)
            w_out = ev_w_out[j].astype(BF16)
            w1 = mlp_w1[i].astype(BF16)
            w2 = mlp_w2[i].astype(BF16)
            xs = _even_tail(ya_x, y, 0, x3, chunks_x, mod, w_glu, b_glu, w_out, n2, w1, w2).reshape(nb * length, d)
            if not last:
                ya_c = _fourier(vc_c, vs_c, dft_c)
                cs = _even_tail(ya_c, y, ncx, c3, chunks_c, mod, w_glu, b_glu, w_out, n2, w1, w2).reshape(nb * clen, d)
        else:
            lam_init = 0.8 - 0.6 * math.exp(-0.3 * i)
            w_in = od_w_in[j].astype(BF16)
            gqk = jnp.stack([jnp.tile(od_q_norm[j], 2) * Q_SCALE,
                             jnp.tile(od_k_norm[j], 2)]).reshape(2, 1, LANES)
            qt_x, k_x, vt_x = _qkv(xs, rows_x, n1, mod, w_in, gqk, ones_bd, rope_tabs)
            qt_c, k_c, vt_c = _qkv(cs, rows_cb, n1, mod, w_in, gqk, ones_bd, None)
            g_head = od_head_norm[j].reshape(DA_V_DIM, 1)
            w_out = od_w_out[j].astype(BF16)
            o_x = _attention(qt_x, [(k_x, vt_x), (k_c, vt_c)], od_lambda[j], g_head, lam_init,
                             _tile(length, 512, LANES), 2)
            w1 = mlp_w1[i].astype(BF16)
            w2 = mlp_w2[i].astype(BF16)
            xs = _odd_tail(o_x, xs, rows_x, mod, w_out, n2, w1, w2)
            if not last:
                o_c = _attention(qt_c, [(k_c, vt_c)], od_lambda[j], g_head, lam_init, clen, DA_HEADS)
                cs = _odd_tail(o_c, cs, rows_c, mod, w_out, n2, w1, w2)
    return xs.reshape(nb, length, d)
```

```python
import functools
import math

import jax
import jax.numpy as jnp
from jax import lax
from jax.experimental import pallas as pl
from jax.experimental.pallas import tpu as pltpu

F32 = jnp.float32
BF16 = jnp.bfloat16

D_MODEL = 1024
DEPTH = 4
GRID_W = 64
EPS = 1e-6
D_FF = 4 * D_MODEL
D_FOURIER = D_MODEL // 2
FOURIER_GROUPS = 4
FOURIER_CH = D_FOURIER // FOURIER_GROUPS
D_SSM = D_MODEL - D_FOURIER
SSM_GROUP = 16
SSM_STATE = 64
DA_HEAD_DIM = 64
DA_V_DIM = 2 * DA_HEAD_DIM
DA_HEADS = D_MODEL // DA_V_DIM
ROPE_THETA = 10000.0

LANES = 128
SSM_CHUNK = 16
SLABS = D_SSM // LANES
SLAB_GROUPS = LANES // SSM_GROUP
SLAB_STATE = SLAB_GROUPS * SSM_STATE
VMEM_LIMIT = 56 * 1024 * 1024
ROW_TILE = 512
SSM_ROW_TILE = 576
ADALN_COL_TILE = 1536
ATTN_Q_TILE = 512
ATTN_HEADS_PER_STEP = 2


def _params(n_axes, flags=None):
    return pltpu.CompilerParams(dimension_semantics=("arbitrary",) * n_axes,
                                vmem_limit_bytes=VMEM_LIMIT, flags=flags)


def _tile(n, target, mult=8):
    t = min(n, target)
    while t > 1 and (n % t or t % mult):
        t -= 1
    return t if n % t == 0 else n


def _const_spec(shape):
    nd = len(shape)
    return pl.BlockSpec(shape, lambda *_: (0,) * nd, pipeline_mode=pl.Buffered(1))


def _sigmoid(x):
    return 1.0 / (1.0 + jnp.exp(-x))


def _gelu_tanh(x):
    c = math.sqrt(2.0 / math.pi)
    return x * (0.5 * (1.0 + jnp.tanh(c * (x + 0.044715 * (x * x * x)))))


def _norm_mod(x, g, shift, scale):
    ms = jnp.mean(x * x, axis=-1, keepdims=True)
    y = x * lax.rsqrt(ms + EPS) * g
    return y * (1.0 + scale) + shift


def _adaln_kernel(c_ref, w_ref, b_ref, o_ref):
    c = c_ref[...]
    s = (c * _sigmoid(c)).astype(BF16)
    o_ref[0] = jnp.dot(s, w_ref[0].astype(BF16), preferred_element_type=F32) + b_ref[0]


def _adaln(cond, ada_w, ada_b):
    depth, d, n = ada_w.shape
    r = cond.shape[0]
    tn = _tile(n, ADALN_COL_TILE, LANES)
    return pl.pallas_call(
        _adaln_kernel,
        out_shape=jax.ShapeDtypeStruct((depth, r, n), F32),
        grid=(depth, n // tn),
        in_specs=[pl.BlockSpec((r, d), lambda l, j: (0, 0)),
                  pl.BlockSpec((1, d, tn), lambda l, j: (l, 0, j)),
                  pl.BlockSpec((1, 1, tn), lambda l, j: (l, 0, j))],
        out_specs=pl.BlockSpec((1, r, tn), lambda l, j: (l, 0, j)),
        compiler_params=_params(2), name="adaln",
    )(cond, ada_w, ada_b.reshape(depth, 1, n))


class _Rows:
    def __init__(self, nb, rpb, tm, mod_row):
        self.nb, self.rpb, self.tm = nb, rpb, tm
        self.nt = rpb // tm
        self.grid = (nb, self.nt)
        self.mod_row = mod_row

    def rows(self, cols):
        nt = self.nt
        return pl.BlockSpec((self.tm, cols), lambda b, i: (b * nt + i, 0))

    def rows3(self, per, cols):
        nt, tm = self.nt, self.tm
        if tm <= per:
            k = per // tm
            return pl.BlockSpec((None, tm, cols), lambda b, i: ((b * nt + i) // k, (b * nt + i) % k, 0))
        return pl.BlockSpec((tm // per, per, cols), lambda b, i: (b * nt + i, 0, 0))

    def mod(self, k):
        if self.mod_row is None:
            return pl.BlockSpec((None, 1, D_MODEL), lambda b, i: (b, 0, k))
        r = self.mod_row
        return pl.BlockSpec((None, 1, D_MODEL), lambda b, i: (r, 0, k))


class _ChunkRows:
    def __init__(self, nb, n_chunks, mod_row):
        self.nb, self.grid, self.mod_row = nb, (n_chunks,), mod_row
        self.tm = nb * SSM_CHUNK

    def rows(self, cols):
        return pl.BlockSpec((self.nb, SSM_CHUNK, cols), lambda c: (0, c, 0))

    def mod(self, k):
        if self.mod_row is None:
            return pl.BlockSpec((self.nb, 1, D_MODEL), lambda c: (0, 0, k))
        r = self.mod_row
        return pl.BlockSpec((1, 1, D_MODEL), lambda c: (r, 0, k))

    def chunk(self, first_chunk):
        return pl.BlockSpec((SLABS, None, self.nb, SSM_CHUNK * LANES), lambda c: (0, first_chunk + c, 0, 0))


def _even_in_kernel(*refs, aliased):
    if aliased:
        refs = refs[1:]
    x_ref, g_ref, sh_ref, sc_ref, w_ref, cs_ref, vc_ref, vs_ref, u_ref = refs[:9]
    zs_scr = refs[9:]
    nb, t_len, _ = x_ref.shape
    h = _norm_mod(x_ref[...], g_ref[...], sh_ref[...], sc_ref[...])
    z = jnp.dot(h.reshape(nb * t_len, D_MODEL).astype(BF16), w_ref[...], preferred_element_type=F32)
    zf = z[:, :D_FOURIER].astype(BF16)
    vc, vs = [], []
    for gi in range(FOURIER_GROUPS):
        v = jnp.dot(zf[:, gi * FOURIER_CH:(gi + 1) * FOURIER_CH], cs_ref[...], preferred_element_type=F32)
        vc.append(v[:, :FOURIER_CH].astype(BF16))
        vs.append(v[:, FOURIER_CH:].astype(BF16))
    vc_ref[...] = jnp.concatenate(vc, axis=1).reshape(nb, t_len, D_FOURIER)
    vs_ref[...] = jnp.concatenate(vs, axis=1).reshape(nb, t_len, D_FOURIER)
    for s in range(SLABS):
        zs_scr[s][...] = z[:, D_FOURIER + s * LANES:D_FOURIER + (s + 1) * LANES]
        for t in range(t_len):
            u_ref[s, :, t * LANES:(t + 1) * LANES] = zs_scr[s][pl.ds(t, nb, stride=t_len), :].astype(BF16)


def _even_in(x, rows, norm_g, mod, w_in, cs, u_shape, first_chunk, u=None):
    nb, n_tok, _ = x.shape
    outs = (jax.ShapeDtypeStruct((nb, n_tok, D_FOURIER), BF16), jax.ShapeDtypeStruct((nb, n_tok, D_FOURIER), BF16),
            jax.ShapeDtypeStruct(u_shape, BF16))
    in_specs = [rows.rows(D_MODEL), _const_spec((1, D_MODEL)), rows.mod(0), rows.mod(1),
                _const_spec((D_MODEL, D_MODEL)), _const_spec((FOURIER_CH, 2 * FOURIER_CH))]
    args = [x, norm_g, mod, mod, w_in, cs]
    aliases = {}
    if u is not None:
        in_specs = [pl.BlockSpec(memory_space=pl.ANY)] + in_specs
        args = [u] + args
        aliases = {0: 2}
    return pl.pallas_call(
        functools.partial(_even_in_kernel, aliased=u is not None), out_shape=outs, grid=rows.grid,
        in_specs=in_specs,
        out_specs=(rows.rows(D_FOURIER), rows.rows(D_FOURIER), rows.chunk(first_chunk)),
        scratch_shapes=[pltpu.VMEM((rows.tm, LANES), F32)] * SLABS,
        input_output_aliases=aliases,
        compiler_params=_params(1), name="even_in",
    )(*args)


DFT_PAD = 16


def _fourier_kernel(cl_ref, sl_ref, rev_ref, vc_ref, vs_ref, o_ref, *, scale):
    half = rev_ref.shape[0]
    a = jnp.dot(cl_ref[...], vc_ref[0], preferred_element_type=F32)
    b = jnp.dot(sl_ref[...], vs_ref[0], preferred_element_type=F32)
    o_ref[0, :half, :] = ((a[:half] - b[:half]) * scale).astype(BF16)
    mirrored = ((a + b) * scale).astype(BF16)
    o_ref[0, half:, :] = jnp.dot(rev_ref[...], mirrored, preferred_element_type=F32).astype(BF16)


def _fourier(vc3, vs3, mats):
    cl, sl, rev = mats
    nb, length, _ = vc3.shape
    scale = 1.0 / math.sqrt(length * FOURIER_CH)
    blk = pl.BlockSpec((1, length, D_FOURIER), lambda b: (b, 0, 0))
    return pl.pallas_call(
        functools.partial(_fourier_kernel, scale=scale),
        out_shape=jax.ShapeDtypeStruct((nb, length, D_FOURIER), BF16),
        grid=(nb,),
        in_specs=[_const_spec(cl.shape), _const_spec(sl.shape), _const_spec(rev.shape), blk, blk],
        out_specs=blk,
        compiler_params=_params(1), name="fourier",
    )(cl, sl, rev, vc3, vs3)


def _dft_mats(n):
    half = n // 2
    k = lax.iota(jnp.int32, half + DFT_PAD)
    t = lax.iota(jnp.int32, n)
    ang = ((k[:, None] * t[None, :]) % n).astype(F32) * (2.0 * math.pi / n)
    cl = jnp.where((k <= half)[:, None], jnp.cos(ang), 0.0)
    sl = jnp.where((k < half)[:, None], jnp.sin(ang), 0.0)
    j = lax.iota(jnp.int32, half)
    rev = k[None, :] == (half - j)[:, None]
    return cl.astype(BF16), sl.astype(BF16), rev.astype(BF16)


def _ssm_ops_kernel(ar_ref, ai_ref, ldt_ref, btr_ref, bti_ref, ctr_ref, cti_ref,
                    bm_ref, wh_ref, atr_ref, ati_ref, e_ref):
    t = SSM_CHUNK
    n = SLAB_STATE
    rows = lax.broadcasted_iota(jnp.int32, (LANES, n), 0)
    cols = lax.broadcasted_iota(jnp.int32, (LANES, n), 1)
    same_group = (rows // SSM_GROUP) == (cols // SSM_STATE)
    lag = lax.broadcasted_iota(jnp.int32, (t + 8, n), 0).astype(F32)
    nt_dims = (((1,), (1,)), ((), ()))
    for d in range(2):
        ar, ai = ar_ref[d], ai_ref[d]
        dt = jnp.exp(ldt_ref[d])
        xr, xi = ar * dt, ai * dt
        mag = jnp.exp(lag * xr)
        pr = mag * jnp.cos(lag * xi)
        pi = mag * jnp.sin(lag * xi)
        abr, abi = pr[1:2], pi[1:2]
        den = ar * ar + ai * ai
        fr = ((abr - 1.0) * ar + abi * ai) / den
        fi = (abi * ar - (abr - 1.0) * ai) / den
        br = jnp.where(same_group, btr_ref[d], 0.0)
        bi = jnp.where(same_group, bti_ref[d], 0.0)
        bbr = fr * br - fi * bi
        bbi = fr * bi + fi * br
        cr = jnp.where(same_group, ctr_ref[d], 0.0)
        ci = jnp.where(same_group, cti_ref[d], 0.0)
        for k in range(t):
            pkr, pki = pr[k:k + 1], pi[k:k + 1]
            x_r = bbr * pkr - bbi * pki
            x_i = bbr * pki + bbi * pkr
            s = t - 1 - k if d == 0 else k
            bm_ref[s * LANES:(s + 1) * LANES, 2 * d * n:(2 * d + 1) * n] = x_r.astype(BF16)
            bm_ref[s * LANES:(s + 1) * LANES, (2 * d + 1) * n:(2 * d + 2) * n] = x_i.astype(BF16)
            e_ref[d, k] = (lax.dot_general(x_r, cr, nt_dims, precision=lax.Precision.HIGHEST,
                                           preferred_element_type=F32)
                           - lax.dot_general(x_i, ci, nt_dims, precision=lax.Precision.HIGHEST,
                                             preferred_element_type=F32))
        for k in range(1, t + 1):
            pkr, pki = pr[k:k + 1], pi[k:k + 1]
            w_r = cr * pkr - ci * pki
            w_i = cr * pki + ci * pkr
            tt = k - 1 if d == 0 else t - k
            wh_ref[2 * d * n:(2 * d + 1) * n, tt * LANES:(tt + 1) * LANES] = w_r.T.astype(BF16)
            wh_ref[(2 * d + 1) * n:(2 * d + 2) * n, tt * LANES:(tt + 1) * LANES] = (-w_i).T.astype(BF16)
        atr_ref[d] = pr[t:t + 1]
        ati_ref[d] = pi[t:t + 1]


def _ssm_toeplitz_kernel(e_ref, dsk_ref, m_ref):
    t = SSM_CHUNK
    s = pl.program_id(1)
    r = lax.broadcasted_iota(jnp.int32, (LANES, LANES), 0)
    c = lax.broadcasted_iota(jnp.int32, (LANES, LANES), 1)
    skip = jnp.where(r == c, jnp.broadcast_to(dsk_ref[...], (LANES, LANES)), 0.0)
    for tt in range(t):
        blk = ((tt >= s).astype(F32) * e_ref[0, jnp.maximum(tt - s, 0)]
               + (s >= tt).astype(F32) * e_ref[1, jnp.maximum(s - tt, 0)]
               + (s == tt).astype(F32) * skip)
        m_ref[:, tt * LANES:(tt + 1) * LANES] = blk.astype(BF16)


def _ssm_ops(a_re, a_im, log_dt, b_re, b_im, c_re, c_im, d_skip):
    t = SSM_CHUNK
    n = SLAB_STATE

    def per_state(x):
        return x.astype(F32).reshape(2, SLABS, 1, n)

    def per_channel(x):
        x = x.astype(F32).reshape(2, SLABS, LANES, SSM_STATE)
        return jnp.tile(x, (1, 1, 1, SLAB_GROUPS))

    ldt = jnp.broadcast_to(log_dt[..., None], a_re.shape)
    vec = pl.BlockSpec((2, None, 1, n), lambda z: (0, z, 0, 0))
    mat = pl.BlockSpec((2, None, LANES, n), lambda z: (0, z, 0, 0))
    bm, wh, at_re, at_im, e = pl.pallas_call(
        _ssm_ops_kernel,
        out_shape=(jax.ShapeDtypeStruct((SLABS, t * LANES, 4 * n), BF16),
                   jax.ShapeDtypeStruct((SLABS, 4 * n, t * LANES), BF16),
                   jax.ShapeDtypeStruct((SLABS, 2, 1, n), F32),
                   jax.ShapeDtypeStruct((SLABS, 2, 1, n), F32),
                   jax.ShapeDtypeStruct((SLABS, 2, t, LANES, LANES), F32)),
        grid=(SLABS,),
        in_specs=[vec, vec, vec, mat, mat, mat, mat],
        out_specs=(pl.BlockSpec((None, t * LANES, 4 * n), lambda z: (z, 0, 0)),
                   pl.BlockSpec((None, 4 * n, t * LANES), lambda z: (z, 0, 0)),
                   pl.BlockSpec((None, 2, 1, n), lambda z: (z, 0, 0, 0)),
                   pl.BlockSpec((None, 2, 1, n), lambda z: (z, 0, 0, 0)),
                   pl.BlockSpec((None, 2, t, LANES, LANES), lambda z: (z, 0, 0, 0, 0))),
        compiler_params=_params(1), name="ssm_ops",
    )(per_state(a_re), per_state(a_im), per_state(ldt),
      per_channel(b_re.transpose(0, 1, 3, 2)), per_channel(b_im.transpose(0, 1, 3, 2)),
      per_channel(c_re), per_channel(c_im))
    m = pl.pallas_call(
        _ssm_toeplitz_kernel,
        out_shape=jax.ShapeDtypeStruct((SLABS, t * LANES, t * LANES), BF16),
        grid=(SLABS, t),
        in_specs=[pl.BlockSpec((None, 2, t, LANES, LANES), lambda z, s: (z, 0, 0, 0, 0)),
                  pl.BlockSpec((None, 1, LANES), lambda z, s: (z, 0, 0))],
        out_specs=pl.BlockSpec((None, LANES, t * LANES), lambda z, s: (z, s, 0)),
        compiler_params=_params(2), name="ssm_toeplitz",
    )(e, d_skip.astype(F32).reshape(SLABS, 1, LANES))
    return m, bm, wh, at_re.reshape(SLABS * 2, 1, n), at_im.reshape(SLABS * 2, 1, n)


def _ssm_contrib_kernel(u_ref, bm_ref, o_ref):
    o_ref[...] = jnp.dot(u_ref[...], bm_ref[...], preferred_element_type=F32)


def _ssm_contrib(u, bm):
    slabs, r, k = u.shape
    n = bm.shape[2]
    tr = _tile(r, SSM_ROW_TILE)
    return pl.pallas_call(
        _ssm_contrib_kernel,
        out_shape=jax.ShapeDtypeStruct((slabs, r, n), F32),
        grid=(slabs, r // tr),
        in_specs=[pl.BlockSpec((None, tr, k), lambda s, i: (s, i, 0)),
                  pl.BlockSpec((None, k, n), lambda s, i: (s, 0, 0))],
        out_specs=pl.BlockSpec((None, tr, n), lambda s, i: (s, i, 0)),
        compiler_params=_params(2), name="ssm_contrib",
    )(u, bm)


SCAN_UNROLL = 8


def _ssm_scan_kernel(ar_ref, ai_ref, cr_ref, ci_ref, hr_ref, hi_ref, *, ncc, nct, rb):
    d = pl.program_id(1)
    lt = ar_ref.shape[-1]
    ncx = nct - ncc
    ar = jnp.broadcast_to(ar_ref[...], (rb, lt))
    ai = jnp.broadcast_to(ai_ref[...], (rb, lt))
    zero = jnp.zeros((rb, lt), F32)

    def advance(i_src, i_dst, hr, hi):
        r0 = pl.multiple_of(i_src * rb, rb)
        r1 = pl.multiple_of(i_dst * rb, rb)
        nr = ar * hr - ai * hi + cr_ref[pl.ds(r0, rb), :]
        ni = ar * hi + ai * hr + ci_ref[pl.ds(r0, rb), :]
        hr_ref[pl.ds(r1, rb), :] = nr.astype(BF16)
        hi_ref[pl.ds(r1, rb), :] = ni.astype(BF16)
        return nr, ni

    def start(i):
        hr_ref[pl.ds(i * rb, rb), :] = zero.astype(BF16)
        hi_ref[pl.ds(i * rb, rb), :] = zero.astype(BF16)
        return zero, zero

    @pl.when(d == 0)
    def _():
        c = start(ncx)
        c = lax.fori_loop(0, ncc - 1, lambda k, c: advance(ncx + k, ncx + k + 1, *c), c, unroll=SCAN_UNROLL)
        c = advance(nct - 1, 0, *c)
        lax.fori_loop(0, ncx - 1, lambda k, c: advance(k, k + 1, *c), c, unroll=SCAN_UNROLL)

    @pl.when(d == 1)
    def _():
        c = start(nct - 1)
        c = lax.fori_loop(0, ncc - 1, lambda k, c: advance(nct - 1 - k, nct - 2 - k, *c), c, unroll=SCAN_UNROLL)
        c = advance(ncx, ncx - 1, *c)
        lax.fori_loop(0, ncx - 1, lambda k, c: advance(ncx - 1 - k, ncx - 2 - k, *c), c, unroll=SCAN_UNROLL)


def _ssm_scan(contrib, at_re, at_im, ncc, nct, rb):
    slabs, r, n = contrib.shape
    lt = 2 * LANES
    q = SLAB_STATE // lt
    re_spec = pl.BlockSpec((None, r, lt), lambda s, d, j: (s, 0, d * 2 * q + j))
    im_spec = pl.BlockSpec((None, r, lt), lambda s, d, j: (s, 0, d * 2 * q + q + j))
    a_spec = pl.BlockSpec((None, 1, lt), lambda s, d, j: (s * 2 + d, 0, j))
    h_spec = pl.BlockSpec((None, r, lt), lambda s, d, j: (s, 0, d * q + j))
    out = jax.ShapeDtypeStruct((slabs, r, n // 2), BF16)
    return pl.pallas_call(
        functools.partial(_ssm_scan_kernel, ncc=ncc, nct=nct, rb=rb),
        out_shape=(out, out),
        grid=(slabs, 2, q),
        in_specs=[a_spec, a_spec, re_spec, im_spec],
        out_specs=(h_spec, h_spec),
        compiler_params=_params(3), name="ssm_scan",
    )(at_re, at_im, contrib, contrib)


def _ssm_read_kernel(u_ref, hr_ref, hi_ref, m_ref, wh_ref, o_ref):
    q = SLAB_STATE
    acc = jnp.dot(u_ref[...], m_ref[...], preferred_element_type=F32)
    for d in range(2):
        acc += jnp.dot(hr_ref[:, d * q:(d + 1) * q], wh_ref[2 * d * q:(2 * d + 1) * q, :],
                       preferred_element_type=F32)
        acc += jnp.dot(hi_ref[:, d * q:(d + 1) * q], wh_ref[(2 * d + 1) * q:(2 * d + 2) * q, :],
                       preferred_element_type=F32)
    o_ref[...] = acc.astype(BF16)


def _ssm_read(u, h_re, h_im, m, wh):
    slabs, r, k = u.shape
    ns = h_re.shape[2]
    nw = wh.shape[1]
    tr = _tile(r, SSM_ROW_TILE)
    tn = k // 2
    return pl.pallas_call(
        _ssm_read_kernel,
        out_shape=jax.ShapeDtypeStruct((slabs, r, k), BF16),
        grid=(slabs, 2, r // tr),
        in_specs=[pl.BlockSpec((None, tr, k), lambda s, j, i: (s, i, 0)),
                  pl.BlockSpec((None, tr, ns), lambda s, j, i: (s, i, 0)),
                  pl.BlockSpec((None, tr, ns), lambda s, j, i: (s, i, 0)),
                  pl.BlockSpec((None, k, tn), lambda s, j, i: (s, 0, j)),
                  pl.BlockSpec((None, nw, tn), lambda s, j, i: (s, 0, j))],
        out_specs=pl.BlockSpec((None, tr, tn), lambda s, j, i: (s, i, j)),
        compiler_params=_params(3), name="ssm_read",
    )(u, h_re, h_im, m, wh)


MLP_FF_CHUNK = 1024


def _mlp_tail(x, g_ref, sh_ref, sc_ref, gate_ref, w1_ref, w2_ref, o_ref):
    fc = MLP_FF_CHUNK
    h = _norm_mod(x, g_ref[...], sh_ref[...], sc_ref[...]).reshape(-1, D_MODEL).astype(BF16)
    acc = jnp.zeros(h.shape, F32)
    for k in range(D_FF // fc):
        a = jnp.dot(h, w1_ref[:, k * fc:(k + 1) * fc], preferred_element_type=F32)
        a = jnp.square(jnp.maximum(a, 0.0)).astype(BF16)
        acc += jnp.dot(a, w2_ref[k * fc:(k + 1) * fc, :], preferred_element_type=F32)
    o_ref[...] = x + gate_ref[...] * acc.reshape(x.shape)


def _even_tail_kernel(ya_ref, y_ref, x_ref, gate1_ref, wg_ref, bg_ref, wa_ref, wb_ref,
                      g_ref, sh_ref, sc_ref, gate2_ref, w1_ref, w2_ref, o_ref, *ys_scr):
    nb, t_len, _ = x_ref.shape
    for s in range(SLABS):
        for t in range(t_len):
            ys_scr[s][pl.ds(t, nb, stride=t_len), :] = y_ref[s, :, t * LANES:(t + 1) * LANES].astype(F32)
    y = _gelu_tanh(jnp.concatenate([r[...] for r in ys_scr], axis=1))
    glu = y * _sigmoid(jnp.dot(y.astype(BF16), wg_ref[...], preferred_element_type=F32) + bg_ref[...])
    acc = jnp.dot(ya_ref[...].reshape(nb * t_len, D_FOURIER), wa_ref[...], preferred_element_type=F32)
    acc += jnp.dot(glu.astype(BF16), wb_ref[...], preferred_element_type=F32)
    x1 = x_ref[...] + gate1_ref[...] * acc.reshape(nb, t_len, D_MODEL)
    _mlp_tail(x1, g_ref, sh_ref, sc_ref, gate2_ref, w1_ref, w2_ref, o_ref)


def _odd_tail_kernel(a_ref, x_ref, gate1_ref, wo_ref, g_ref, sh_ref, sc_ref, gate2_ref, w1_ref, w2_ref, o_ref):
    a = a_ref[...].reshape(x_ref.shape)
    x1 = x_ref[...] + gate1_ref[...] * jnp.dot(a, wo_ref[...], preferred_element_type=F32)
    _mlp_tail(x1, g_ref, sh_ref, sc_ref, gate2_ref, w1_ref, w2_ref, o_ref)


def _mlp_specs(rows):
    return [_const_spec((1, D_MODEL)), rows.mod(3), rows.mod(4), rows.mod(5),
            _const_spec((D_MODEL, D_FF)), _const_spec((D_FF, D_MODEL))]


def _even_tail(ya, y, first_chunk, x, rows, mod, w_glu, b_glu, w_out, norm_g, w1, w2):
    return pl.pallas_call(
        _even_tail_kernel, out_shape=jax.ShapeDtypeStruct(x.shape, F32), grid=rows.grid,
        in_specs=[rows.rows(D_FOURIER), rows.chunk(first_chunk), rows.rows(D_MODEL), rows.mod(2),
                  _const_spec((D_SSM, D_SSM)), _const_spec((1, D_SSM)),
                  _const_spec((D_FOURIER, D_MODEL)), _const_spec((D_SSM, D_MODEL))] + _mlp_specs(rows),
        out_specs=rows.rows(D_MODEL),
        scratch_shapes=[pltpu.VMEM((rows.tm, LANES), F32)] * SLABS,
        compiler_params=_params(1), name="even_tail",
    )(ya, y, x, mod, w_glu, b_glu, w_out[:D_FOURIER], w_out[D_FOURIER:], norm_g, mod, mod, mod, w1, w2)


def _odd_tail(a, x, rows, mod, w_out, norm_g, w1, w2):
    return pl.pallas_call(
        _odd_tail_kernel, out_shape=jax.ShapeDtypeStruct(x.shape, F32), grid=rows.grid,
        in_specs=[rows.rows3(a.shape[1], D_MODEL), rows.rows(D_MODEL), rows.mod(2),
                  _const_spec((D_MODEL, D_MODEL))] + _mlp_specs(rows),
        out_specs=rows.rows(D_MODEL),
        compiler_params=_params(2), name="odd_tail",
    )(a, x, mod, w_out, norm_g, mod, mod, mod, w1, w2)


KEY_CHUNK = 256
V_ROWS = DA_V_DIM + 16
Q_SCALE = DA_HEAD_DIM ** -0.5 * math.log2(math.e)


def _qkv_kernel(*refs, rope):
    if rope:
        (x_ref, g_ref, sh_ref, sc_ref, w_ref, gqk_ref, ones_ref,
         cos_ref, sa_ref, sb_ref, qt_ref, k_ref, vt_ref) = refs
    else:
        x_ref, g_ref, sh_ref, sc_ref, w_ref, gqk_ref, ones_ref, qt_ref, k_ref, vt_ref = refs
    h = _norm_mod(x_ref[...], g_ref[...], sh_ref[...], sc_ref[...]).astype(BF16)
    y = jnp.dot(h, w_ref[...], preferred_element_type=F32)
    tm = y.shape[0]
    for s in range(2 * DA_HEADS):
        blk = y[:, s * LANES:(s + 1) * LANES]
        ssq = jnp.dot((blk * blk).astype(BF16), ones_ref[...], preferred_element_type=F32)
        nrm = blk * gqk_ref[s // DA_HEADS]
        if rope:
            nrm = (nrm * cos_ref[...] + pltpu.roll(nrm, LANES - 16, 1) * sa_ref[...]
                   + pltpu.roll(nrm, 16, 1) * sb_ref[...])
        nrm = nrm * lax.rsqrt(ssq * (1.0 / DA_HEAD_DIM) + EPS)
        if s < DA_HEADS:
            qt_ref[s] = nrm.T.astype(BF16)
        else:
            hd = s - DA_HEADS
            k_ref[:, hd * LANES:(hd + 1) * LANES] = nrm.astype(BF16)
    ones = jnp.ones((V_ROWS - DA_V_DIM, tm), BF16)
    for hd in range(DA_HEADS):
        v = y[:, (2 * DA_HEADS + hd) * LANES:(2 * DA_HEADS + hd + 1) * LANES]
        vt_ref[hd, :DA_V_DIM, :] = v.T.astype(BF16)
        vt_ref[hd, DA_V_DIM:, :] = ones


def _qkv(x, rows, norm_g, mod, w_in, gqk, ones_bd, rope_tabs):
    nb, rpb, tm = rows.nb, rows.rpb, rows.tm
    assert tm % KEY_CHUNK == 0
    rope = rope_tabs is not None
    in_specs = [rows.rows(D_MODEL), _const_spec((1, D_MODEL)), rows.mod(0), rows.mod(1),
                _const_spec((D_MODEL, 3 * D_MODEL)), _const_spec((2, 1, LANES)),
                _const_spec((LANES, LANES))]
    args = [x, norm_g, mod, mod, w_in, gqk, ones_bd]
    if rope:
        tab = pl.BlockSpec((tm, LANES), lambda b, i: (i, 0))
        in_specs += [tab, tab, tab]
        args += list(rope_tabs)
    out_shape = (jax.ShapeDtypeStruct((nb, DA_HEADS, LANES, rpb), BF16),
                 jax.ShapeDtypeStruct((nb, rpb, D_MODEL), BF16),
                 jax.ShapeDtypeStruct((nb, DA_HEADS, V_ROWS, rpb), BF16))
    out_specs = (pl.BlockSpec((None, DA_HEADS, LANES, tm), lambda b, i: (b, 0, 0, i)),
                 pl.BlockSpec((None, tm, D_MODEL), lambda b, i: (b, i, 0)),
                 pl.BlockSpec((None, DA_HEADS, V_ROWS, tm), lambda b, i: (b, 0, 0, i)))
    return pl.pallas_call(
        functools.partial(_qkv_kernel, rope=rope),
        out_shape=out_shape, grid=rows.grid, in_specs=in_specs, out_specs=out_specs,
        compiler_params=_params(2), name="qkv",
    )(*args)


def _rope_tables(length):
    pos = jnp.arange(length, dtype=jnp.int32)
    row = (pos // GRID_W).astype(F32)
    col = (pos % GRID_W).astype(F32)
    half = DA_HEAD_DIM // 2
    inv_freq = ROPE_THETA ** (-jnp.arange(0, half, 2, dtype=F32) / half)
    lane = jnp.arange(LANES)
    f = lane % 16
    is_col = (lane % DA_HEAD_DIM) >= half
    second = (lane % half) >= 16
    ang = jnp.where(is_col[None, :], col[:, None], row[:, None]) * inv_freq[f][None, :]
    cos, sin = jnp.cos(ang), jnp.sin(ang)
    sin_a = jnp.where(second[None, :], 0.0, -sin)
    sin_b = jnp.where(second[None, :], sin, 0.0)
    return cos, sin_a, sin_b


def _attn_kernel(*refs, n_kv, lam_init, tq):
    lp_ref, gh_ref, qt_ref = refs[:3]
    kv_refs = refs[3:3 + 2 * n_kv]
    o_ref = refs[3 + 2 * n_kv]
    bufs = refs[4 + 2 * n_kv:]
    lp = lp_ref[...]
    lam = (jnp.exp(jnp.sum(lp[0:1] * lp[1:2], axis=-1, keepdims=True))
           - jnp.exp(jnp.sum(lp[2:3] * lp[3:4], axis=-1, keepdims=True)) + lam_init)
    hb = qt_ref.shape[0]
    items = [(hh, j) for hh in range(hb) for j in range(qt_ref.shape[2] // tq)]
    kc = KEY_CHUNK

    key_chunks = []
    for i in range(n_kv):
        for c in range(kv_refs[2 * i].shape[0] // kc):
            key_chunks.append((kv_refs[2 * i], kv_refs[2 * i + 1], c))

    def masked_q(item):
        hh, j = item
        qt = qt_ref[hh, :, j * tq:(j + 1) * tq]
        row = lax.broadcasted_iota(jnp.int32, qt.shape, 0)
        zero = jnp.zeros_like(qt)
        return jnp.concatenate([jnp.where(row < DA_HEAD_DIM, qt, zero),
                                jnp.where(row >= DA_HEAD_DIM, qt, zero)], axis=1)

    def score_chunk(n, hh, qst, s_scr, m):
        k_ref, _, c = key_chunks[n]
        s = jnp.dot(k_ref[c * kc:(c + 1) * kc, hh * LANES:(hh + 1) * LANES], qst,
                    preferred_element_type=F32)
        s_scr[n * kc:(n + 1) * kc, :] = s
        return jnp.maximum(m, s.reshape(kc // 8, 8, 2 * tq).max(axis=0))

    def value_chunk(n, hh, s_scr, m, acc):
        _, vt_ref, c = key_chunks[n]
        r = pl.multiple_of(n * kc + jnp.minimum(pl.program_id(0), 0) * kc, kc)
        p = jnp.exp2((s_scr[pl.ds(r, kc), :] - m).astype(BF16))
        return acc + jnp.dot(vt_ref[hh, :, c * kc:(c + 1) * kc], p, preferred_element_type=F32)

    def finish(item, acc):
        hh, j = item
        o2 = acc[:DA_V_DIM] / acc[DA_V_DIM:DA_V_DIM + 1]
        o = o2[:, :tq] - lam * o2[:, tq:]
        o = o * lax.rsqrt(jnp.mean(o * o, axis=0, keepdims=True) + EPS) * gh_ref[...]
        o_ref[j * tq:(j + 1) * tq, hh * LANES:(hh + 1) * LANES] = (o * (1.0 - lam_init)).T.astype(BF16)

    neg = jnp.full((8, 2 * tq), -jnp.inf, F32)
    qst = masked_q(items[0])
    m8 = neg
    for n in range(len(key_chunks)):
        m8 = score_chunk(n, items[0][0], qst, bufs[0], m8)
    for i, item in enumerate(items):
        nxt = items[i + 1] if i + 1 < len(items) else None
        m = m8.max(axis=0, keepdims=True)
        acc = jnp.zeros((V_ROWS, 2 * tq), F32)
        if nxt is not None:
            qst = masked_q(nxt)
            m8 = neg
        for n in range(len(key_chunks)):
            if nxt is not None:
                m8 = score_chunk(n, nxt[0], qst, bufs[(i + 1) % 2], m8)
            acc = value_chunk(n, item[0], bufs[i % 2], m, acc)
        finish(item, acc)


def _attention(qt, kv_list, lam_p, g_head, lam_init, tq, hb):
    nb, _, _, lq = qt.shape
    in_specs = [_const_spec((4, DA_HEAD_DIM)), _const_spec((DA_V_DIM, 1)),
                pl.BlockSpec((None, hb, LANES, lq), lambda b, h: (b, h, 0, 0))]
    args = [lam_p, g_head, qt]
    total = 0
    for k, vt in kv_list:
        lk = k.shape[1]
        total += lk
        in_specs.append(pl.BlockSpec((None, lk, hb * LANES), lambda b, h: (b, 0, h)))
        in_specs.append(pl.BlockSpec((None, hb, V_ROWS, lk), lambda b, h: (b, h, 0, 0)))
        args += [k, vt]
    n_bufs = min(2, hb * (lq // tq))
    return pl.pallas_call(
        functools.partial(_attn_kernel, n_kv=len(kv_list), lam_init=lam_init, tq=tq),
        out_shape=jax.ShapeDtypeStruct((nb, lq, D_MODEL), BF16),
        grid=(nb, DA_HEADS // hb),
        in_specs=in_specs,
        out_specs=pl.BlockSpec((None, lq, hb * LANES), lambda b, h: (b, 0, h)),
        scratch_shapes=[pltpu.VMEM((total, 2 * tq), F32)] * n_bufs,
        compiler_params=_params(2), name="diff_attn",
    )(*args)


def kernel(x, c, ctx, c_ctx, norm1_g, norm2_g, ada_w, ada_b, mlp_w1, mlp_w2, ev_w_in, ev_w_out, ssm_a_re, ssm_a_im, ssm_log_dt, ssm_b_re, ssm_b_im, ssm_c_re, ssm_c_im, ssm_d, ssm_w_glu, ssm_b_glu, od_w_in, od_w_out, od_q_norm, od_k_norm, od_lambda, od_head_norm):
    nb, length, d = x.shape
    clen = ctx.shape[1]
    assert d == D_MODEL and length % GRID_W == 0
    assert length % SSM_CHUNK == 0 and clen % SSM_CHUNK == 0

    xs = x.reshape(nb * length, d)
    cs = ctx.reshape(nb * clen, d)
    rows_x = _Rows(nb, length, _tile(length, ROW_TILE), None)
    rows_c = _Rows(1, nb * clen, _tile(nb * clen, ROW_TILE), nb)
    rows_cb = _Rows(nb, clen, clen, nb)
    chunks_x = _ChunkRows(nb, length // SSM_CHUNK, None)
    chunks_c = _ChunkRows(nb, clen // SSM_CHUNK, nb)

    n_cond = nb + 1
    pad = (-n_cond) % 8
    cond = jnp.concatenate([c, c_ctx[None, :], jnp.zeros((pad, d), F32)], axis=0)
    mods = _adaln(cond, ada_w, ada_b)[:, :n_cond].reshape(DEPTH, n_cond, 1, 6 * d)

    kf = lax.iota(jnp.int32, FOURIER_CH)
    ang = ((kf[:, None] * kf[None, :]) % FOURIER_CH).astype(F32) * (2.0 * math.pi / FOURIER_CH)
    cs_ch = jnp.concatenate([jnp.cos(ang), jnp.sin(ang)], axis=1).astype(BF16)
    dft_x = _dft_mats(length)
    dft_c = _dft_mats(clen)
    rope_tabs = _rope_tables(length)
    lane = jnp.arange(LANES)
    ones_bd = (lane[:, None] // DA_HEAD_DIM == lane[None, :] // DA_HEAD_DIM).astype(BF16)

    ncc = clen // SSM_CHUNK
    ncx = length // SSM_CHUNK
    nct = ncx + ncc

    for i in range(DEPTH):
        last = i == DEPTH - 1
        j = i // 2
        mod = mods[i]
        n1 = norm1_g[i].reshape(1, d)
        n2 = norm2_g[i].reshape(1, d)
        if i % 2 == 0:
            w_in = ev_w_in[j].astype(BF16)
            u_shape = (SLABS, nct, nb, SSM_CHUNK * LANES)
            x3 = xs.reshape(nb, length, d)
            c3 = cs.reshape(nb, clen, d)
            vc_x, vs_x, u = _even_in(x3, chunks_x, n1, mod, w_in, cs_ch, u_shape, 0)
            vc_c, vs_c, u = _even_in(c3, chunks_c, n1, mod, w_in, cs_ch, u_shape, ncx, u)
            ya_x = _fourier(vc_x, vs_x, dft_x)
            m, bm, wh, at_re, at_im = _ssm_ops(ssm_a_re[j], ssm_a_im[j], ssm_log_dt[j], ssm_b_re[j],
                                                ssm_b_im[j], ssm_c_re[j], ssm_c_im[j], ssm_d[j])
            u = u.reshape(SLABS, nct * nb, SSM_CHUNK * LANES)
            contrib = _ssm_contrib(u, bm)
            h_re, h_im = _ssm_scan(contrib, at_re, at_im, ncc, nct, nb)
            y = _ssm_read(u, h_re, h_im, m, wh).reshape(u_shape)
            w_glu = ssm_w_glu[j].astype(BF16)
            b_glu = ssm_b_glu[j].reshape(1, D_SSM)
            w_out = ev_w_out[j].astype(BF16)
            w1 = mlp_w1[i].astype(BF16)
            w2 = mlp_w2[i].astype(BF16)
            xs = _even_tail(ya_x, y, 0, x3, chunks_x, mod, w_glu, b_glu, w_out, n2, w1, w2).reshape(nb * length, d)
            if not last:
                ya_c = _fourier(vc_c, vs_c, dft_c)
                cs = _even_tail(ya_c, y, ncx, c3, chunks_c, mod, w_glu, b_glu, w_out, n2, w1, w2).reshape(nb * clen, d)
        else:
            lam_init = 0.8 - 0.6 * math.exp(-0.3 * i)
            w_in = od_w_in[j].astype(BF16)
            gqk = jnp.stack([jnp.tile(od_q_norm[j], 2) * Q_SCALE,
                             jnp.tile(od_k_norm[j], 2)]).reshape(2, 1, LANES)
            qt_x, k_x, vt_x = _qkv(xs, rows_x, n1, mod, w_in, gqk, ones_bd, rope_tabs)
            qt_c, k_c, vt_c = _qkv(cs, rows_cb, n1, mod, w_in, gqk, ones_bd, None)
            g_head = od_head_norm[j].reshape(DA_V_DIM, 1)
            w_out = od_w_out[j].astype(BF16)
            o_x = _attention(qt_x, [(k_x, vt_x), (k_c, vt_c)], od_lambda[j], g_head, lam_init,
                             _tile(length, ATTN_Q_TILE, LANES), ATTN_HEADS_PER_STEP)
            w1 = mlp_w1[i].astype(BF16)
            w2 = mlp_w2[i].astype(BF16)
            xs = _odd_tail(o_x, xs, rows_x, mod, w_out, n2, w1, w2)
            if not last:
                o_c = _attention(qt_c, [(k_c, vt_c)], od_lambda[j], g_head, lam_init, clen, DA_HEADS)
                cs = _odd_tail(o_c, cs, rows_c, mod, w_out, n2, w1, w2)
    return xs.reshape(nb, length, d)
```

```python
import functools
import math

import jax
import jax.numpy as jnp
from jax import lax
from jax.experimental import pallas as pl
from jax.experimental.pallas import tpu as pltpu

F32 = jnp.float32
BF16 = jnp.bfloat16

D_MODEL = 1024
DEPTH = 4
GRID_W = 64
EPS = 1e-6
D_FF = 4 * D_MODEL
D_FOURIER = D_MODEL // 2
FOURIER_GROUPS = 4
FOURIER_CH = D_FOURIER // FOURIER_GROUPS
D_SSM = D_MODEL - D_FOURIER
SSM_GROUP = 16
SSM_STATE = 64
DA_HEAD_DIM = 64
DA_V_DIM = 2 * DA_HEAD_DIM
DA_HEADS = D_MODEL // DA_V_DIM
ROPE_THETA = 10000.0

LANES = 128
SSM_CHUNK = 16
SLABS = D_SSM // LANES
SLAB_GROUPS = LANES // SSM_GROUP
SLAB_STATE = SLAB_GROUPS * SSM_STATE
VMEM_LIMIT = 56 * 1024 * 1024
ROW_TILE = 512
SSM_ROW_TILE = 576
ADALN_COL_TILE = 1536
ATTN_Q_TILE = 512
ATTN_HEADS_PER_STEP = 2


def _params(n_axes):
    return pltpu.CompilerParams(dimension_semantics=("arbitrary",) * n_axes, vmem_limit_bytes=VMEM_LIMIT)


def _tile(n, target, mult=8):
    t = min(n, target)
    while t > 1 and (n % t or t % mult):
        t -= 1
    return t if n % t == 0 else n


def _const_spec(shape):
    nd = len(shape)
    return pl.BlockSpec(shape, lambda *_: (0,) * nd, pipeline_mode=pl.Buffered(1))


def _sigmoid(x):
    return 1.0 / (1.0 + jnp.exp(-x))


def _gelu_tanh(x):
    c = math.sqrt(2.0 / math.pi)
    return x * (0.5 * (1.0 + jnp.tanh(c * (x + 0.044715 * (x * x * x)))))


def _norm_mod(x, g, shift, scale):
    ms = jnp.mean(x * x, axis=-1, keepdims=True)
    y = x * lax.rsqrt(ms + EPS) * g
    return y * (1.0 + scale) + shift


def _adaln_kernel(c_ref, w_ref, b_ref, o_ref):
    c = c_ref[...]
    s = (c * _sigmoid(c)).astype(BF16)
    o_ref[0] = jnp.dot(s, w_ref[0].astype(BF16), preferred_element_type=F32) + b_ref[0]


def _adaln(cond, ada_w, ada_b):
    depth, d, n = ada_w.shape
    r = cond.shape[0]
    tn = _tile(n, ADALN_COL_TILE, LANES)
    return pl.pallas_call(
        _adaln_kernel,
        out_shape=jax.ShapeDtypeStruct((depth, r, n), F32),
        grid=(depth, n // tn),
        in_specs=[pl.BlockSpec((r, d), lambda l, j: (0, 0)),
                  pl.BlockSpec((1, d, tn), lambda l, j: (l, 0, j)),
                  pl.BlockSpec((1, 1, tn), lambda l, j: (l, 0, j))],
        out_specs=pl.BlockSpec((1, r, tn), lambda l, j: (l, 0, j)),
        compiler_params=_params(2), name="adaln",
    )(cond, ada_w, ada_b.reshape(depth, 1, n))


class _Rows:
    def __init__(self, nb, rpb, tm, mod_row):
        self.nb, self.rpb, self.tm = nb, rpb, tm
        self.nt = rpb // tm
        self.grid = (nb, self.nt)
        self.mod_row = mod_row

    def rows(self, cols):
        nt = self.nt
        return pl.BlockSpec((self.tm, cols), lambda b, i: (b * nt + i, 0))

    def rows3(self, per, cols):
        nt, tm = self.nt, self.tm
        if tm <= per:
            k = per // tm
            return pl.BlockSpec((None, tm, cols), lambda b, i: ((b * nt + i) // k, (b * nt + i) % k, 0))
        return pl.BlockSpec((tm // per, per, cols), lambda b, i: (b * nt + i, 0, 0))

    def mod(self, k):
        if self.mod_row is None:
            return pl.BlockSpec((None, 1, D_MODEL), lambda b, i: (b, 0, k))
        r = self.mod_row
        return pl.BlockSpec((None, 1, D_MODEL), lambda b, i: (r, 0, k))


class _ChunkRows:
    def __init__(self, nb, n_chunks, mod_row):
        self.nb, self.grid, self.mod_row = nb, (n_chunks,), mod_row
        self.tm = nb * SSM_CHUNK

    def rows(self, cols):
        return pl.BlockSpec((self.nb, SSM_CHUNK, cols), lambda c: (0, c, 0))

    def mod(self, k):
        if self.mod_row is None:
            return pl.BlockSpec((self.nb, 1, D_MODEL), lambda c: (0, 0, k))
        r = self.mod_row
        return pl.BlockSpec((1, 1, D_MODEL), lambda c: (r, 0, k))

    def chunk(self, first_chunk):
        return pl.BlockSpec((SLABS, None, self.nb, SSM_CHUNK * LANES), lambda c: (0, first_chunk + c, 0, 0))


def _zero_chunks_kernel(o_ref):
    o_ref[...] = jnp.zeros(o_ref.shape, o_ref.dtype)


def _zero_chunks(u_shape):
    slabs, n_chunks, nb, width = u_shape
    tc = _tile(n_chunks, SSM_CHUNK, 1)
    return pl.pallas_call(
        _zero_chunks_kernel, out_shape=jax.ShapeDtypeStruct(u_shape, BF16), grid=(slabs, n_chunks // tc),
        out_specs=pl.BlockSpec((None, tc, nb, width), lambda s, c: (s, c, 0, 0)),
        compiler_params=_params(2), name="zero_chunks",
    )()


def _even_in_kernel(u_in_ref, x_ref, g_ref, sh_ref, sc_ref, w_ref, cs_ref, vc_ref, vs_ref, u_ref, *zs_scr):
    del u_in_ref
    nb, t_len, _ = x_ref.shape
    h = _norm_mod(x_ref[...], g_ref[...], sh_ref[...], sc_ref[...])
    z = jnp.dot(h.reshape(nb * t_len, D_MODEL).astype(BF16), w_ref[...], preferred_element_type=F32)
    zf = z[:, :D_FOURIER].astype(BF16)
    vc, vs = [], []
    for gi in range(FOURIER_GROUPS):
        v = jnp.dot(zf[:, gi * FOURIER_CH:(gi + 1) * FOURIER_CH], cs_ref[...], preferred_element_type=F32)
        vc.append(v[:, :FOURIER_CH].astype(BF16))
        vs.append(v[:, FOURIER_CH:].astype(BF16))
    vc_ref[...] = jnp.concatenate(vc, axis=1).reshape(nb, t_len, D_FOURIER)
    vs_ref[...] = jnp.concatenate(vs, axis=1).reshape(nb, t_len, D_FOURIER)
    for s in range(SLABS):
        zs_scr[s][...] = z[:, D_FOURIER + s * LANES:D_FOURIER + (s + 1) * LANES]
        for t in range(t_len):
            u_ref[s, :, t * LANES:(t + 1) * LANES] = zs_scr[s][pl.ds(t, nb, stride=t_len), :].astype(BF16)


def _even_in(x, rows, norm_g, mod, w_in, cs, u, first_chunk):
    nb, n_tok, _ = x.shape
    outs = (jax.ShapeDtypeStruct((nb, n_tok, D_FOURIER), BF16), jax.ShapeDtypeStruct((nb, n_tok, D_FOURIER), BF16),
            jax.ShapeDtypeStruct(u.shape, BF16))
    return pl.pallas_call(
        _even_in_kernel, out_shape=outs, grid=rows.grid,
        in_specs=[pl.BlockSpec(memory_space=pl.ANY), rows.rows(D_MODEL), _const_spec((1, D_MODEL)),
                  rows.mod(0), rows.mod(1), _const_spec((D_MODEL, D_MODEL)),
                  _const_spec((FOURIER_CH, 2 * FOURIER_CH))],
        out_specs=(rows.rows(D_FOURIER), rows.rows(D_FOURIER), rows.chunk(first_chunk)),
        scratch_shapes=[pltpu.VMEM((rows.tm, LANES), F32)] * SLABS,
        input_output_aliases={0: 2},
        compiler_params=_params(1), name="even_in",
    )(u, x, norm_g, mod, mod, w_in, cs)


DFT_PAD = 16


def _fourier_kernel(cl_ref, sl_ref, rev_ref, vc_ref, vs_ref, o_ref, *, scale):
    half = rev_ref.shape[0]
    a = jnp.dot(cl_ref[...], vc_ref[0], preferred_element_type=F32)
    b = jnp.dot(sl_ref[...], vs_ref[0], preferred_element_type=F32)
    o_ref[0, :half, :] = ((a[:half] - b[:half]) * scale).astype(BF16)
    mirrored = ((a + b) * scale).astype(BF16)
    o_ref[0, half:, :] = jnp.dot(rev_ref[...], mirrored, preferred_element_type=F32).astype(BF16)


def _fourier(vc3, vs3, mats):
    cl, sl, rev = mats
    nb, length, _ = vc3.shape
    scale = 1.0 / math.sqrt(length * FOURIER_CH)
    blk = pl.BlockSpec((1, length, D_FOURIER), lambda b: (b, 0, 0))
    return pl.pallas_call(
        functools.partial(_fourier_kernel, scale=scale),
        out_shape=jax.ShapeDtypeStruct((nb, length, D_FOURIER), BF16),
        grid=(nb,),
        in_specs=[_const_spec(cl.shape), _const_spec(sl.shape), _const_spec(rev.shape), blk, blk],
        out_specs=blk,
        compiler_params=_params(1), name="fourier",
    )(cl, sl, rev, vc3, vs3)


def _dft_mats(n):
    half = n // 2
    k = lax.iota(jnp.int32, half + DFT_PAD)
    t = lax.iota(jnp.int32, n)
    ang = ((k[:, None] * t[None, :]) % n).astype(F32) * (2.0 * math.pi / n)
    cl = jnp.where((k <= half)[:, None], jnp.cos(ang), 0.0)
    sl = jnp.where((k < half)[:, None], jnp.sin(ang), 0.0)
    j = lax.iota(jnp.int32, half)
    rev = k[None, :] == (half - j)[:, None]
    return cl.astype(BF16), sl.astype(BF16), rev.astype(BF16)


def _ssm_ops_kernel(ar_ref, ai_ref, ldt_ref, btr_ref, bti_ref, ctr_ref, cti_ref,
                    bm_ref, wh_ref, atr_ref, ati_ref, e_ref):
    t = SSM_CHUNK
    n = SLAB_STATE
    rows = lax.broadcasted_iota(jnp.int32, (LANES, n), 0)
    cols = lax.broadcasted_iota(jnp.int32, (LANES, n), 1)
    same_group = (rows // SSM_GROUP) == (cols // SSM_STATE)
    lag = lax.broadcasted_iota(jnp.int32, (t + 8, n), 0).astype(F32)
    nt_dims = (((1,), (1,)), ((), ()))
    for d in range(2):
        ar, ai = ar_ref[d], ai_ref[d]
        dt = jnp.exp(ldt_ref[d])
        xr, xi = ar * dt, ai * dt
        mag = jnp.exp(lag * xr)
        pr = mag * jnp.cos(lag * xi)
        pi = mag * jnp.sin(lag * xi)
        abr, abi = pr[1:2], pi[1:2]
        den = ar * ar + ai * ai
        fr = ((abr - 1.0) * ar + abi * ai) / den
        fi = (abi * ar - (abr - 1.0) * ai) / den
        br = jnp.where(same_group, btr_ref[d], 0.0)
        bi = jnp.where(same_group, bti_ref[d], 0.0)
        bbr = fr * br - fi * bi
        bbi = fr * bi + fi * br
        cr = jnp.where(same_group, ctr_ref[d], 0.0)
        ci = jnp.where(same_group, cti_ref[d], 0.0)
        for k in range(t):
            pkr, pki = pr[k:k + 1], pi[k:k + 1]
            x_r = bbr * pkr - bbi * pki
            x_i = bbr * pki + bbi * pkr
            s = t - 1 - k if d == 0 else k
            bm_ref[s * LANES:(s + 1) * LANES, 2 * d * n:(2 * d + 1) * n] = x_r.astype(BF16)
            bm_ref[s * LANES:(s + 1) * LANES, (2 * d + 1) * n:(2 * d + 2) * n] = x_i.astype(BF16)
            e_ref[d, k] = (lax.dot_general(x_r, cr, nt_dims, precision=lax.Precision.HIGHEST,
                                           preferred_element_type=F32)
                           - lax.dot_general(x_i, ci, nt_dims, precision=lax.Precision.HIGHEST,
                                             preferred_element_type=F32))
        for k in range(1, t + 1):
            pkr, pki = pr[k:k + 1], pi[k:k + 1]
            w_r = cr * pkr - ci * pki
            w_i = cr * pki + ci * pkr
            tt = k - 1 if d == 0 else t - k
            wh_ref[2 * d * n:(2 * d + 1) * n, tt * LANES:(tt + 1) * LANES] = w_r.T.astype(BF16)
            wh_ref[(2 * d + 1) * n:(2 * d + 2) * n, tt * LANES:(tt + 1) * LANES] = (-w_i).T.astype(BF16)
        atr_ref[d] = pr[t:t + 1]
        ati_ref[d] = pi[t:t + 1]


def _ssm_toeplitz_kernel(e_ref, dsk_ref, m_ref):
    t = SSM_CHUNK
    s = pl.program_id(1)
    r = lax.broadcasted_iota(jnp.int32, (LANES, LANES), 0)
    c = lax.broadcasted_iota(jnp.int32, (LANES, LANES), 1)
    skip = jnp.where(r == c, jnp.broadcast_to(dsk_ref[...], (LANES, LANES)), 0.0)
    for tt in range(t):
        blk = ((tt >= s).astype(F32) * e_ref[0, jnp.maximum(tt - s, 0)]
               + (s >= tt).astype(F32) * e_ref[1, jnp.maximum(s - tt, 0)]
               + (s == tt).astype(F32) * skip)
        m_ref[:, tt * LANES:(tt + 1) * LANES] = blk.astype(BF16)


def _ssm_ops(a_re, a_im, log_dt, b_re, b_im, c_re, c_im, d_skip):
    t = SSM_CHUNK
    n = SLAB_STATE

    def per_state(x):
        return x.astype(F32).reshape(2, SLABS, 1, n)

    def per_channel(x):
        x = x.astype(F32).reshape(2, SLABS, LANES, SSM_STATE)
        return jnp.tile(x, (1, 1, 1, SLAB_GROUPS))

    ldt = jnp.broadcast_to(log_dt[..., None], a_re.shape)
    vec = pl.BlockSpec((2, None, 1, n), lambda z: (0, z, 0, 0))
    mat = pl.BlockSpec((2, None, LANES, n), lambda z: (0, z, 0, 0))
    bm, wh, at_re, at_im, e = pl.pallas_call(
        _ssm_ops_kernel,
        out_shape=(jax.ShapeDtypeStruct((SLABS, t * LANES, 4 * n), BF16),
                   jax.ShapeDtypeStruct((SLABS, 4 * n, t * LANES), BF16),
                   jax.ShapeDtypeStruct((SLABS, 2, 1, n), F32),
                   jax.ShapeDtypeStruct((SLABS, 2, 1, n), F32),
                   jax.ShapeDtypeStruct((SLABS, 2, t, LANES, LANES), F32)),
        grid=(SLABS,),
        in_specs=[vec, vec, vec, mat, mat, mat, mat],
        out_specs=(pl.BlockSpec((None, t * LANES, 4 * n), lambda z: (z, 0, 0)),
                   pl.BlockSpec((None, 4 * n, t * LANES), lambda z: (z, 0, 0)),
                   pl.BlockSpec((None, 2, 1, n), lambda z: (z, 0, 0, 0)),
                   pl.BlockSpec((None, 2, 1, n), lambda z: (z, 0, 0, 0)),
                   pl.BlockSpec((None, 2, t, LANES, LANES), lambda z: (z, 0, 0, 0, 0))),
        compiler_params=_params(1), name="ssm_ops",
    )(per_state(a_re), per_state(a_im), per_state(ldt),
      per_channel(b_re.transpose(0, 1, 3, 2)), per_channel(b_im.transpose(0, 1, 3, 2)),
      per_channel(c_re), per_channel(c_im))
    m = pl.pallas_call(
        _ssm_toeplitz_kernel,
        out_shape=jax.ShapeDtypeStruct((SLABS, t * LANES, t * LANES), BF16),
        grid=(SLABS, t),
        in_specs=[pl.BlockSpec((None, 2, t, LANES, LANES), lambda z, s: (z, 0, 0, 0, 0)),
                  pl.BlockSpec((None, 1, LANES), lambda z, s: (z, 0, 0))],
        out_specs=pl.BlockSpec((None, LANES, t * LANES), lambda z, s: (z, s, 0)),
        compiler_params=_params(2), name="ssm_toeplitz",
    )(e, d_skip.astype(F32).reshape(SLABS, 1, LANES))
    return m, bm, wh, at_re.reshape(SLABS * 2, 1, n), at_im.reshape(SLABS * 2, 1, n)


def _ssm_contrib_kernel(u_ref, bm_ref, o_ref):
    o_ref[...] = jnp.dot(u_ref[...], bm_ref[...], preferred_element_type=F32)


def _ssm_contrib(u, bm):
    slabs, r, k = u.shape
    n = bm.shape[2]
    tr = _tile(r, SSM_ROW_TILE)
    return pl.pallas_call(
        _ssm_contrib_kernel,
        out_shape=jax.ShapeDtypeStruct((slabs, r, n), F32),
        grid=(slabs, r // tr),
        in_specs=[pl.BlockSpec((None, tr, k), lambda s, i: (s, i, 0)),
                  pl.BlockSpec((None, k, n), lambda s, i: (s, 0, 0))],
        out_specs=pl.BlockSpec((None, tr, n), lambda s, i: (s, i, 0)),
        compiler_params=_params(2), name="ssm_contrib",
    )(u, bm)


SCAN_UNROLL = 8


def _ssm_scan_kernel(ar_ref, ai_ref, cr_ref, ci_ref, hr_ref, hi_ref, *, ncc, nct, rb):
    d = pl.program_id(1)
    lt = ar_ref.shape[-1]
    ncx = nct - ncc
    ar = jnp.broadcast_to(ar_ref[...], (rb, lt))
    ai = jnp.broadcast_to(ai_ref[...], (rb, lt))
    zero = jnp.zeros((rb, lt), F32)

    def advance(i_src, i_dst, hr, hi):
        r0 = pl.multiple_of(i_src * rb, rb)
        r1 = pl.multiple_of(i_dst * rb, rb)
        nr = ar * hr - ai * hi + cr_ref[pl.ds(r0, rb), :]
        ni = ar * hi + ai * hr + ci_ref[pl.ds(r0, rb), :]
        hr_ref[pl.ds(r1, rb), :] = nr.astype(BF16)
        hi_ref[pl.ds(r1, rb), :] = ni.astype(BF16)
        return nr, ni

    def start(i):
        hr_ref[pl.ds(i * rb, rb), :] = zero.astype(BF16)
        hi_ref[pl.ds(i * rb, rb), :] = zero.astype(BF16)
        return zero, zero

    @pl.when(d == 0)
    def _():
        c = start(ncx)
        c = lax.fori_loop(0, ncc - 1, lambda k, c: advance(ncx + k, ncx + k + 1, *c), c, unroll=SCAN_UNROLL)
        c = advance(nct - 1, 0, *c)
        lax.fori_loop(0, ncx - 1, lambda k, c: advance(k, k + 1, *c), c, unroll=SCAN_UNROLL)

    @pl.when(d == 1)
    def _():
        c = start(nct - 1)
        c = lax.fori_loop(0, ncc - 1, lambda k, c: advance(nct - 1 - k, nct - 2 - k, *c), c, unroll=SCAN_UNROLL)
        c = advance(ncx, ncx - 1, *c)
        lax.fori_loop(0, ncx - 1, lambda k, c: advance(ncx - 1 - k, ncx - 2 - k, *c), c, unroll=SCAN_UNROLL)


def _ssm_scan(contrib, at_re, at_im, ncc, nct, rb):
    slabs, r, n = contrib.shape
    lt = 2 * LANES
    q = SLAB_STATE // lt
    re_spec = pl.BlockSpec((None, r, lt), lambda s, d, j: (s, 0, d * 2 * q + j))
    im_spec = pl.BlockSpec((None, r, lt), lambda s, d, j: (s, 0, d * 2 * q + q + j))
    a_spec = pl.BlockSpec((None, 1, lt), lambda s, d, j: (s * 2 + d, 0, j))
    h_spec = pl.BlockSpec((None, r, lt), lambda s, d, j: (s, 0, d * q + j))
    out = jax.ShapeDtypeStruct((slabs, r, n // 2), BF16)
    return pl.pallas_call(
        functools.partial(_ssm_scan_kernel, ncc=ncc, nct=nct, rb=rb),
        out_shape=(out, out),
        grid=(slabs, 2, q),
        in_specs=[a_spec, a_spec, re_spec, im_spec],
        out_specs=(h_spec, h_spec),
        compiler_params=_params(3), name="ssm_scan",
    )(at_re, at_im, contrib, contrib)


def _ssm_read_kernel(u_ref, hr_ref, hi_ref, m_ref, wh_ref, o_ref):
    q = SLAB_STATE
    acc = jnp.dot(u_ref[...], m_ref[...], preferred_element_type=F32)
    for d in range(2):
        acc += jnp.dot(hr_ref[:, d * q:(d + 1) * q], wh_ref[2 * d * q:(2 * d + 1) * q, :],
                       preferred_element_type=F32)
        acc += jnp.dot(hi_ref[:, d * q:(d + 1) * q], wh_ref[(2 * d + 1) * q:(2 * d + 2) * q, :],
                       preferred_element_type=F32)
    o_ref[...] = acc.astype(BF16)


def _ssm_read(u, h_re, h_im, m, wh):
    slabs, r, k = u.shape
    ns = h_re.shape[2]
    nw = wh.shape[1]
    tr = _tile(r, SSM_ROW_TILE)
    tn = k // 2
    return pl.pallas_call(
        _ssm_read_kernel,
        out_shape=jax.ShapeDtypeStruct((slabs, r, k), BF16),
        grid=(slabs, 2, r // tr),
        in_specs=[pl.BlockSpec((None, tr, k), lambda s, j, i: (s, i, 0)),
                  pl.BlockSpec((None, tr, ns), lambda s, j, i: (s, i, 0)),
                  pl.BlockSpec((None, tr, ns), lambda s, j, i: (s, i, 0)),
                  pl.BlockSpec((None, k, tn), lambda s, j, i: (s, 0, j)),
                  pl.BlockSpec((None, nw, tn), lambda s, j, i: (s, 0, j))],
        out_specs=pl.BlockSpec((None, tr, tn), lambda s, j, i: (s, i, j)),
        compiler_params=_params(3), name="ssm_read",
    )(u, h_re, h_im, m, wh)


MLP_FF_CHUNK = 1024


def _mlp_tail(x, g_ref, sh_ref, sc_ref, gate_ref, w1_ref, w2_ref, o_ref):
    fc = MLP_FF_CHUNK
    h = _norm_mod(x, g_ref[...], sh_ref[...], sc_ref[...]).reshape(-1, D_MODEL).astype(BF16)
    acc = jnp.zeros(h.shape, F32)
    for k in range(D_FF // fc):
        a = jnp.dot(h, w1_ref[:, k * fc:(k + 1) * fc], preferred_element_type=F32)
        a = jnp.square(jnp.maximum(a, 0.0)).astype(BF16)
        acc += jnp.dot(a, w2_ref[k * fc:(k + 1) * fc, :], preferred_element_type=F32)
    o_ref[...] = x + gate_ref[...] * acc.reshape(x.shape)


def _even_tail_kernel(ya_ref, y_ref, x_ref, gate1_ref, wg_ref, bg_ref, wa_ref, wb_ref,
                      g_ref, sh_ref, sc_ref, gate2_ref, w1_ref, w2_ref, o_ref, *ys_scr):
    nb, t_len, _ = x_ref.shape
    for s in range(SLABS):
        for t in range(t_len):
            ys_scr[s][pl.ds(t, nb, stride=t_len), :] = y_ref[s, :, t * LANES:(t + 1) * LANES].astype(F32)
    y = _gelu_tanh(jnp.concatenate([r[...] for r in ys_scr], axis=1))
    glu = y * _sigmoid(jnp.dot(y.astype(BF16), wg_ref[...], preferred_element_type=F32) + bg_ref[...])
    acc = jnp.dot(ya_ref[...].reshape(nb * t_len, D_FOURIER), wa_ref[...], preferred_element_type=F32)
    acc += jnp.dot(glu.astype(BF16), wb_ref[...], preferred_element_type=F32)
    x1 = x_ref[...] + gate1_ref[...] * acc.reshape(nb, t_len, D_MODEL)
    _mlp_tail(x1, g_ref, sh_ref, sc_ref, gate2_ref, w1_ref, w2_ref, o_ref)


def _odd_tail_kernel(a_ref, x_ref, gate1_ref, wo_ref, g_ref, sh_ref, sc_ref, gate2_ref, w1_ref, w2_ref, o_ref):
    a = a_ref[...].reshape(x_ref.shape)
    x1 = x_ref[...] + gate1_ref[...] * jnp.dot(a, wo_ref[...], preferred_element_type=F32)
    _mlp_tail(x1, g_ref, sh_ref, sc_ref, gate2_ref, w1_ref, w2_ref, o_ref)


def _mlp_specs(rows):
    return [_const_spec((1, D_MODEL)), rows.mod(3), rows.mod(4), rows.mod(5),
            _const_spec((D_MODEL, D_FF)), _const_spec((D_FF, D_MODEL))]


def _even_tail(ya, y, first_chunk, x, rows, mod, w_glu, b_glu, w_out, norm_g, w1, w2):
    return pl.pallas_call(
        _even_tail_kernel, out_shape=jax.ShapeDtypeStruct(x.shape, F32), grid=rows.grid,
        in_specs=[rows.rows(D_FOURIER), rows.chunk(first_chunk), rows.rows(D_MODEL), rows.mod(2),
                  _const_spec((D_SSM, D_SSM)), _const_spec((1, D_SSM)),
                  _const_spec((D_FOURIER, D_MODEL)), _const_spec((D_SSM, D_MODEL))] + _mlp_specs(rows),
        out_specs=rows.rows(D_MODEL),
        scratch_shapes=[pltpu.VMEM((rows.tm, LANES), F32)] * SLABS,
        compiler_params=_params(1), name="even_tail",
    )(ya, y, x, mod, w_glu, b_glu, w_out[:D_FOURIER], w_out[D_FOURIER:], norm_g, mod, mod, mod, w1, w2)


def _odd_tail(a, x, rows, mod, w_out, norm_g, w1, w2):
    return pl.pallas_call(
        _odd_tail_kernel, out_shape=jax.ShapeDtypeStruct(x.shape, F32), grid=rows.grid,
        in_specs=[rows.rows3(a.shape[1], D_MODEL), rows.rows(D_MODEL), rows.mod(2),
                  _const_spec((D_MODEL, D_MODEL))] + _mlp_specs(rows),
        out_specs=rows.rows(D_MODEL),
        compiler_params=_params(2), name="odd_tail",
    )(a, x, mod, w_out, norm_g, mod, mod, mod, w1, w2)


KEY_CHUNK = 256
V_ROWS = DA_V_DIM + 16
Q_SCALE = DA_HEAD_DIM ** -0.5 * math.log2(math.e)


def _qkv_kernel(*refs, rope):
    if rope:
        (x_ref, g_ref, sh_ref, sc_ref, w_ref, gqk_ref, ones_ref,
         cos_ref, sa_ref, sb_ref, qt_ref, k_ref, vt_ref) = refs
    else:
        x_ref, g_ref, sh_ref, sc_ref, w_ref, gqk_ref, ones_ref, qt_ref, k_ref, vt_ref = refs
    h = _norm_mod(x_ref[...], g_ref[...], sh_ref[...], sc_ref[...]).astype(BF16)
    y = jnp.dot(h, w_ref[...], preferred_element_type=F32)
    tm = y.shape[0]
    for s in range(2 * DA_HEADS):
        blk = y[:, s * LANES:(s + 1) * LANES]
        ssq = jnp.dot((blk * blk).astype(BF16), ones_ref[...], preferred_element_type=F32)
        nrm = blk * gqk_ref[s // DA_HEADS]
        if rope:
            nrm = (nrm * cos_ref[...] + pltpu.roll(nrm, LANES - 16, 1) * sa_ref[...]
                   + pltpu.roll(nrm, 16, 1) * sb_ref[...])
        nrm = nrm * lax.rsqrt(ssq * (1.0 / DA_HEAD_DIM) + EPS)
        if s < DA_HEADS:
            qt_ref[s] = nrm.T.astype(BF16)
        else:
            hd = s - DA_HEADS
            k_ref[:, hd * LANES:(hd + 1) * LANES] = nrm.astype(BF16)
    ones = jnp.ones((V_ROWS - DA_V_DIM, tm), BF16)
    for hd in range(DA_HEADS):
        v = y[:, (2 * DA_HEADS + hd) * LANES:(2 * DA_HEADS + hd + 1) * LANES]
        vt_ref[hd, :DA_V_DIM, :] = v.T.astype(BF16)
        vt_ref[hd, DA_V_DIM:, :] = ones


def _qkv(x, rows, norm_g, mod, w_in, gqk, ones_bd, rope_tabs):
    nb, rpb, tm = rows.nb, rows.rpb, rows.tm
    assert tm % KEY_CHUNK == 0
    rope = rope_tabs is not None
    in_specs = [rows.rows(D_MODEL), _const_spec((1, D_MODEL)), rows.mod(0), rows.mod(1),
                _const_spec((D_MODEL, 3 * D_MODEL)), _const_spec((2, 1, LANES)),
                _const_spec((LANES, LANES))]
    args = [x, norm_g, mod, mod, w_in, gqk, ones_bd]
    if rope:
        tab = pl.BlockSpec((tm, LANES), lambda b, i: (i, 0))
        in_specs += [tab, tab, tab]
        args += list(rope_tabs)
    out_shape = (jax.ShapeDtypeStruct((nb, DA_HEADS, LANES, rpb), BF16),
                 jax.ShapeDtypeStruct((nb, rpb, D_MODEL), BF16),
                 jax.ShapeDtypeStruct((nb, DA_HEADS, V_ROWS, rpb), BF16))
    out_specs = (pl.BlockSpec((None, DA_HEADS, LANES, tm), lambda b, i: (b, 0, 0, i)),
                 pl.BlockSpec((None, tm, D_MODEL), lambda b, i: (b, i, 0)),
                 pl.BlockSpec((None, DA_HEADS, V_ROWS, tm), lambda b, i: (b, 0, 0, i)))
    return pl.pallas_call(
        functools.partial(_qkv_kernel, rope=rope),
        out_shape=out_shape, grid=rows.grid, in_specs=in_specs, out_specs=out_specs,
        compiler_params=_params(2), name="qkv",
    )(*args)


def _rope_tables(length):
    pos = jnp.arange(length, dtype=jnp.int32)
    row = (pos // GRID_W).astype(F32)
    col = (pos % GRID_W).astype(F32)
    half = DA_HEAD_DIM // 2
    inv_freq = ROPE_THETA ** (-jnp.arange(0, half, 2, dtype=F32) / half)
    lane = jnp.arange(LANES)
    f = lane % 16
    is_col = (lane % DA_HEAD_DIM) >= half
    second = (lane % half) >= 16
    ang = jnp.where(is_col[None, :], col[:, None], row[:, None]) * inv_freq[f][None, :]
    cos, sin = jnp.cos(ang), jnp.sin(ang)
    sin_a = jnp.where(second[None, :], 0.0, -sin)
    sin_b = jnp.where(second[None, :], sin, 0.0)
    return cos, sin_a, sin_b


def _attn_kernel(*refs, n_kv, lam_init, tq):
    lp_ref, gh_ref, qt_ref = refs[:3]
    kv_refs = refs[3:3 + 2 * n_kv]
    o_ref = refs[3 + 2 * n_kv]
    bufs = refs[4 + 2 * n_kv:]
    lp = lp_ref[...]
    lam = (jnp.exp(jnp.sum(lp[0:1] * lp[1:2], axis=-1, keepdims=True))
           - jnp.exp(jnp.sum(lp[2:3] * lp[3:4], axis=-1, keepdims=True)) + lam_init)
    hb = qt_ref.shape[0]
    items = [(hh, j) for hh in range(hb) for j in range(qt_ref.shape[2] // tq)]
    kc = KEY_CHUNK

    key_chunks = []
    for i in range(n_kv):
        for c in range(kv_refs[2 * i].shape[0] // kc):
            key_chunks.append((kv_refs[2 * i], kv_refs[2 * i + 1], c))

    def masked_q(item):
        hh, j = item
        qt = qt_ref[hh, :, j * tq:(j + 1) * tq]
        row = lax.broadcasted_iota(jnp.int32, qt.shape, 0)
        zero = jnp.zeros_like(qt)
        return jnp.concatenate([jnp.where(row < DA_HEAD_DIM, qt, zero),
                                jnp.where(row >= DA_HEAD_DIM, qt, zero)], axis=1)

    def score_chunk(n, hh, qst, s_scr, m):
        k_ref, _, c = key_chunks[n]
        s = jnp.dot(k_ref[c * kc:(c + 1) * kc, hh * LANES:(hh + 1) * LANES], qst,
                    preferred_element_type=F32)
        s_scr[n * kc:(n + 1) * kc, :] = s
        return jnp.maximum(m, s.reshape(kc // 8, 8, 2 * tq).max(axis=0))

    def value_chunk(n, hh, s_scr, m, acc):
        _, vt_ref, c = key_chunks[n]
        r = pl.multiple_of(n * kc + jnp.minimum(pl.program_id(0), 0) * kc, kc)
        p = jnp.exp2((s_scr[pl.ds(r, kc), :] - m).astype(BF16))
        return acc + jnp.dot(vt_ref[hh, :, c * kc:(c + 1) * kc], p, preferred_element_type=F32)

    def finish(item, acc):
        hh, j = item
        o2 = acc[:DA_V_DIM] / acc[DA_V_DIM:DA_V_DIM + 1]
        o = o2[:, :tq] - lam * o2[:, tq:]
        o = o * lax.rsqrt(jnp.mean(o * o, axis=0, keepdims=True) + EPS) * gh_ref[...]
        o_ref[j * tq:(j + 1) * tq, hh * LANES:(hh + 1) * LANES] = (o * (1.0 - lam_init)).T.astype(BF16)

    neg = jnp.full((8, 2 * tq), -jnp.inf, F32)
    qst = masked_q(items[0])
    m8 = neg
    for n in range(len(key_chunks)):
        m8 = score_chunk(n, items[0][0], qst, bufs[0], m8)
    for i, item in enumerate(items):
        nxt = items[i + 1] if i + 1 < len(items) else None
        m = m8.max(axis=0, keepdims=True)
        acc = jnp.zeros((V_ROWS, 2 * tq), F32)
        if nxt is not None:
            qst = masked_q(nxt)
            m8 = neg
        for n in range(len(key_chunks)):
            if nxt is not None:
                m8 = score_chunk(n, nxt[0], qst, bufs[(i + 1) % 2], m8)
            acc = value_chunk(n, item[0], bufs[i % 2], m, acc)
        finish(item, acc)


def _attention(qt, kv_list, lam_p, g_head, lam_init, tq, hb):
    nb, _, _, lq = qt.shape
    in_specs = [_const_spec((4, DA_HEAD_DIM)), _const_spec((DA_V_DIM, 1)),
                pl.BlockSpec((None, hb, LANES, lq), lambda b, h: (b, h, 0, 0))]
    args = [lam_p, g_head, qt]
    total = 0
    for k, vt in kv_list:
        lk = k.shape[1]
        total += lk
        in_specs.append(pl.BlockSpec((None, lk, hb * LANES), lambda b, h: (b, 0, h)))
        in_specs.append(pl.BlockSpec((None, hb, V_ROWS, lk), lambda b, h: (b, h, 0, 0)))
        args += [k, vt]
    n_bufs = min(2, hb * (lq // tq))
    return pl.pallas_call(
        functools.partial(_attn_kernel, n_kv=len(kv_list), lam_init=lam_init, tq=tq),
        out_shape=jax.ShapeDtypeStruct((nb, lq, D_MODEL), BF16),
        grid=(nb, DA_HEADS // hb),
        in_specs=in_specs,
        out_specs=pl.BlockSpec((None, lq, hb * LANES), lambda b, h: (b, 0, h)),
        scratch_shapes=[pltpu.VMEM((total, 2 * tq), F32)] * n_bufs,
        compiler_params=_params(2), name="diff_attn",
    )(*args)


def kernel(x, c, ctx, c_ctx, norm1_g, norm2_g, ada_w, ada_b, mlp_w1, mlp_w2, ev_w_in, ev_w_out, ssm_a_re, ssm_a_im, ssm_log_dt, ssm_b_re, ssm_b_im, ssm_c_re, ssm_c_im, ssm_d, ssm_w_glu, ssm_b_glu, od_w_in, od_w_out, od_q_norm, od_k_norm, od_lambda, od_head_norm):
    nb, length, d = x.shape
    clen = ctx.shape[1]
    assert d == D_MODEL and length % GRID_W == 0
    assert length % SSM_CHUNK == 0 and clen % SSM_CHUNK == 0

    xs = x.reshape(nb * length, d)
    cs = ctx.reshape(nb * clen, d)
    rows_x = _Rows(nb, length, _tile(length, ROW_TILE), None)
    rows_c = _Rows(1, nb * clen, _tile(nb * clen, ROW_TILE), nb)
    rows_cb = _Rows(nb, clen, clen, nb)
    chunks_x = _ChunkRows(nb, length // SSM_CHUNK, None)
    chunks_c = _ChunkRows(nb, clen // SSM_CHUNK, nb)

    n_cond = nb + 1
    pad = (-n_cond) % 8
    cond = jnp.concatenate([c, c_ctx[None, :], jnp.zeros((pad, d), F32)], axis=0)
    mods = _adaln(cond, ada_w, ada_b)[:, :n_cond].reshape(DEPTH, n_cond, 1, 6 * d)

    kf = lax.iota(jnp.int32, FOURIER_CH)
    ang = ((kf[:, None] * kf[None, :]) % FOURIER_CH).astype(F32) * (2.0 * math.pi / FOURIER_CH)
    cs_ch = jnp.concatenate([jnp.cos(ang), jnp.sin(ang)], axis=1).astype(BF16)
    dft_x = _dft_mats(length)
    dft_c = _dft_mats(clen)
    rope_tabs = _rope_tables(length)
    lane = jnp.arange(LANES)
    ones_bd = (lane[:, None] // DA_HEAD_DIM == lane[None, :] // DA_HEAD_DIM).astype(BF16)

    ncc = clen // SSM_CHUNK
    ncx = length // SSM_CHUNK
    nct = ncx + ncc

    for i in range(DEPTH):
        last = i == DEPTH - 1
        j = i // 2
        mod = mods[i]
        n1 = norm1_g[i].reshape(1, d)
        n2 = norm2_g[i].reshape(1, d)
        if i % 2 == 0:
            w_in = ev_w_in[j].astype(BF16)
            u_shape = (SLABS, nct, nb, SSM_CHUNK * LANES)
            x3 = xs.reshape(nb, length, d)
            c3 = cs.reshape(nb, clen, d)
            u = _zero_chunks(u_shape)
            vc_x, vs_x, u = _even_in(x3, chunks_x, n1, mod, w_in, cs_ch, u, 0)
            vc_c, vs_c, u = _even_in(c3, chunks_c, n1, mod, w_in, cs_ch, u, ncx)
            ya_x = _fourier(vc_x, vs_x, dft_x)
            m, bm, wh, at_re, at_im = _ssm_ops(ssm_a_re[j], ssm_a_im[j], ssm_log_dt[j], ssm_b_re[j],
                                                ssm_b_im[j], ssm_c_re[j], ssm_c_im[j], ssm_d[j])
            u = u.reshape(SLABS, nct * nb, SSM_CHUNK * LANES)
            contrib = _ssm_contrib(u, bm)
            h_re, h_im = _ssm_scan(contrib, at_re, at_im, ncc, nct, nb)
            y = _ssm_read(u, h_re, h_im, m, wh).reshape(u_shape)
            w_glu = ssm_w_glu[j].astype(BF16)
            b_glu = ssm_b_glu[j].reshape(1, D_SSM)
            w_out = ev_w_out[j].astype(BF16)
            w1 = mlp_w1[i].astype(BF16)
            w2 = mlp_w2[i].astype(BF16)
            xs = _even_tail(ya_x, y, 0, x3, chunks_x, mod, w_glu, b_glu, w_out, n2, w1, w2).reshape(nb * length, d)
            if not last:
                ya_c = _fourier(vc_c, vs_c, dft_c)
                cs = _even_tail(ya_c, y, ncx, c3, chunks_c, mod, w_glu, b_glu, w_out, n2, w1, w2).reshape(nb * clen, d)
        else:
            lam_init = 0.8 - 0.6 * math.exp(-0.3 * i)
            w_in = od_w_in[j].astype(BF16)
            gqk = jnp.stack([jnp.tile(od_q_norm[j], 2) * Q_SCALE,
                             jnp.tile(od_k_norm[j], 2)]).reshape(2, 1, LANES)
            qt_x, k_x, vt_x = _qkv(xs, rows_x, n1, mod, w_in, gqk, ones_bd, rope_tabs)
            qt_c, k_c, vt_c = _qkv(cs, rows_cb, n1, mod, w_in, gqk, ones_bd, None)
            g_head = od_head_norm[j].reshape(DA_V_DIM, 1)
            w_out = od_w_out[j].astype(BF16)
            o_x = _attention(qt_x, [(k_x, vt_x), (k_c, vt_c)], od_lambda[j], g_head, lam_init,
                             _tile(length, ATTN_Q_TILE, LANES), ATTN_HEADS_PER_STEP)
            w1 = mlp_w1[i].astype(BF16)
            w2 = mlp_w2[i].astype(BF16)
            xs = _odd_tail(o_x, xs, rows_x, mod, w_out, n2, w1, w2)
            if not last:
                o_c = _attention(qt_c, [(k_c, vt_c)], od_lambda[j], g_head, lam_init, clen, DA_HEADS)
                cs = _odd_tail(o_c, cs, rows_c, mod, w_out, n2, w1, w2)
    return xs.reshape(nb, length, d)
```

```python
import functools
import math

import jax
import jax.numpy as jnp
from jax import lax
from jax.experimental import pallas as pl
from jax.experimental.pallas import tpu as pltpu

F32 = jnp.float32
BF16 = jnp.bfloat16

D_MODEL = 1024
DEPTH = 4
GRID_W = 64
EPS = 1e-6
D_FF = 4 * D_MODEL
D_FOURIER = D_MODEL // 2
FOURIER_GROUPS = 4
FOURIER_CH = D_FOURIER // FOURIER_GROUPS
D_SSM = D_MODEL - D_FOURIER
SSM_GROUP = 16
SSM_STATE = 64
DA_HEAD_DIM = 64
DA_V_DIM = 2 * DA_HEAD_DIM
DA_HEADS = D_MODEL // DA_V_DIM
ROPE_THETA = 10000.0

LANES = 128
SSM_CHUNK = 16
SLABS = D_SSM // LANES
SLAB_GROUPS = LANES // SSM_GROUP
SLAB_STATE = SLAB_GROUPS * SSM_STATE
VMEM_LIMIT = 56 * 1024 * 1024
ROW_TILE = 512
SSM_ROW_TILE = 576
ADALN_COL_TILE = 1536
ATTN_Q_TILE = 512
ATTN_HEADS_PER_STEP = 2


def _params(n_axes):
    return pltpu.CompilerParams(dimension_semantics=("arbitrary",) * n_axes, vmem_limit_bytes=VMEM_LIMIT)


def _tile(n, target, mult=8):
    t = min(n, target)
    while t > 1 and (n % t or t % mult):
        t -= 1
    return t if n % t == 0 else n


def _const_spec(shape):
    nd = len(shape)
    return pl.BlockSpec(shape, lambda *_: (0,) * nd, pipeline_mode=pl.Buffered(1))


def _sigmoid(x):
    return 1.0 / (1.0 + jnp.exp(-x))


def _gelu_tanh(x):
    c = math.sqrt(2.0 / math.pi)
    return x * (0.5 * (1.0 + jnp.tanh(c * (x + 0.044715 * (x * x * x)))))


def _norm_mod(x, g, shift, scale):
    ms = jnp.mean(x * x, axis=-1, keepdims=True)
    y = x * lax.rsqrt(ms + EPS) * g
    return y * (1.0 + scale) + shift


def _adaln_kernel(c_ref, w_ref, b_ref, o_ref):
    c = c_ref[...]
    s = (c * _sigmoid(c)).astype(BF16)
    o_ref[0] = jnp.dot(s, w_ref[0].astype(BF16), preferred_element_type=F32) + b_ref[0]


def _adaln(cond, ada_w, ada_b):
    depth, d, n = ada_w.shape
    r = cond.shape[0]
    tn = _tile(n, ADALN_COL_TILE, LANES)
    return pl.pallas_call(
        _adaln_kernel,
        out_shape=jax.ShapeDtypeStruct((depth, r, n), F32),
        grid=(depth, n // tn),
        in_specs=[pl.BlockSpec((r, d), lambda l, j: (0, 0)),
                  pl.BlockSpec((1, d, tn), lambda l, j: (l, 0, j)),
                  pl.BlockSpec((1, 1, tn), lambda l, j: (l, 0, j))],
        out_specs=pl.BlockSpec((1, r, tn), lambda l, j: (l, 0, j)),
        compiler_params=_params(2), name="adaln",
    )(cond, ada_w, ada_b.reshape(depth, 1, n))


class _Rows:
    def __init__(self, nb, rpb, tm, mod_row):
        self.nb, self.rpb, self.tm = nb, rpb, tm
        self.nt = rpb // tm
        self.grid = (nb, self.nt)
        self.mod_row = mod_row

    def rows(self, cols):
        nt = self.nt
        return pl.BlockSpec((self.tm, cols), lambda b, i: (b * nt + i, 0))

    def rows3(self, per, cols):
        nt, tm = self.nt, self.tm
        if tm <= per:
            k = per // tm
            return pl.BlockSpec((None, tm, cols), lambda b, i: ((b * nt + i) // k, (b * nt + i) % k, 0))
        return pl.BlockSpec((tm // per, per, cols), lambda b, i: (b * nt + i, 0, 0))

    def mod(self, k):
        if self.mod_row is None:
            return pl.BlockSpec((None, 1, D_MODEL), lambda b, i: (b, 0, k))
        r = self.mod_row
        return pl.BlockSpec((None, 1, D_MODEL), lambda b, i: (r, 0, k))


class _ChunkRows:
    def __init__(self, nb, n_chunks, mod_row):
        self.nb, self.grid, self.mod_row = nb, (n_chunks,), mod_row
        self.tm = nb * SSM_CHUNK

    def rows(self, cols):
        return pl.BlockSpec((self.nb, SSM_CHUNK, cols), lambda c: (0, c, 0))

    def mod(self, k):
        if self.mod_row is None:
            return pl.BlockSpec((self.nb, 1, D_MODEL), lambda c: (0, 0, k))
        r = self.mod_row
        return pl.BlockSpec((1, 1, D_MODEL), lambda c: (r, 0, k))

    def chunk(self, first_chunk):
        return pl.BlockSpec((SLABS, None, self.nb, SSM_CHUNK * LANES), lambda c: (0, first_chunk + c, 0, 0))


def _zero_chunks_kernel(o_ref):
    o_ref[...] = jnp.zeros(o_ref.shape, o_ref.dtype)


def _zero_chunks(u_shape):
    slabs, n_chunks, nb, width = u_shape
    tc = _tile(n_chunks, SSM_CHUNK, 1)
    return pl.pallas_call(
        _zero_chunks_kernel, out_shape=jax.ShapeDtypeStruct(u_shape, BF16), grid=(slabs, n_chunks // tc),
        out_specs=pl.BlockSpec((None, tc, nb, width), lambda s, c: (s, c, 0, 0)),
        compiler_params=_params(2), name="zero_chunks",
    )()


def _even_in_kernel(u_in_ref, x_ref, g_ref, sh_ref, sc_ref, w_ref, cs_ref, vc_ref, vs_ref, u_ref, *zs_scr):
    del u_in_ref
    nb, t_len, _ = x_ref.shape
    h = _norm_mod(x_ref[...], g_ref[...], sh_ref[...], sc_ref[...])
    z = jnp.dot(h.reshape(nb * t_len, D_MODEL).astype(BF16), w_ref[...], preferred_element_type=F32)
    zf = z[:, :D_FOURIER].astype(BF16)
    vc, vs = [], []
    for gi in range(FOURIER_GROUPS):
        v = jnp.dot(zf[:, gi * FOURIER_CH:(gi + 1) * FOURIER_CH], cs_ref[...], preferred_element_type=F32)
        vc.append(v[:, :FOURIER_CH].astype(BF16))
        vs.append(v[:, FOURIER_CH:].astype(BF16))
    vc_ref[...] = jnp.concatenate(vc, axis=1).reshape(nb, t_len, D_FOURIER)
    vs_ref[...] = jnp.concatenate(vs, axis=1).reshape(nb, t_len, D_FOURIER)
    for s in range(SLABS):
        zs_scr[s][...] = z[:, D_FOURIER + s * LANES:D_FOURIER + (s + 1) * LANES]
        for t in range(t_len):
            u_ref[s, :, t * LANES:(t + 1) * LANES] = zs_scr[s][pl.ds(t, nb, stride=t_len), :].astype(BF16)


def _even_in(x, rows, norm_g, mod, w_in, cs, u, first_chunk):
    nb, n_tok, _ = x.shape
    outs = (jax.ShapeDtypeStruct((nb, n_tok, D_FOURIER), BF16), jax.ShapeDtypeStruct((nb, n_tok, D_FOURIER), BF16),
            jax.ShapeDtypeStruct(u.shape, BF16))
    return pl.pallas_call(
        _even_in_kernel, out_shape=outs, grid=rows.grid,
        in_specs=[pl.BlockSpec(memory_space=pl.ANY), rows.rows(D_MODEL), _const_spec((1, D_MODEL)),
                  rows.mod(0), rows.mod(1), _const_spec((D_MODEL, D_MODEL)),
                  _const_spec((FOURIER_CH, 2 * FOURIER_CH))],
        out_specs=(rows.rows(D_FOURIER), rows.rows(D_FOURIER), rows.chunk(first_chunk)),
        scratch_shapes=[pltpu.VMEM((rows.tm, LANES), F32)] * SLABS,
        input_output_aliases={0: 2},
        compiler_params=_params(1), name="even_in",
    )(u, x, norm_g, mod, mod, w_in, cs)


DFT_PAD = 16


def _fourier_kernel(cl_ref, sl_ref, rev_ref, vc_ref, vs_ref, o_ref, *, scale):
    half = rev_ref.shape[0]
    a = jnp.dot(cl_ref[...], vc_ref[0], preferred_element_type=F32)
    b = jnp.dot(sl_ref[...], vs_ref[0], preferred_element_type=F32)
    o_ref[0, :half, :] = ((a[:half] - b[:half]) * scale).astype(BF16)
    mirrored = ((a + b) * scale).astype(BF16)
    o_ref[0, half:, :] = jnp.dot(rev_ref[...], mirrored, preferred_element_type=F32).astype(BF16)


def _fourier(vc3, vs3, mats):
    cl, sl, rev = mats
    nb, length, _ = vc3.shape
    scale = 1.0 / math.sqrt(length * FOURIER_CH)
    blk = pl.BlockSpec((1, length, D_FOURIER), lambda b: (b, 0, 0))
    return pl.pallas_call(
        functools.partial(_fourier_kernel, scale=scale),
        out_shape=jax.ShapeDtypeStruct((nb, length, D_FOURIER), BF16),
        grid=(nb,),
        in_specs=[_const_spec(cl.shape), _const_spec(sl.shape), _const_spec(rev.shape), blk, blk],
        out_specs=blk,
        compiler_params=_params(1), name="fourier",
    )(cl, sl, rev, vc3, vs3)


def _dft_mats(n):
    half = n // 2
    k = lax.iota(jnp.int32, half + DFT_PAD)
    t = lax.iota(jnp.int32, n)
    ang = ((k[:, None] * t[None, :]) % n).astype(F32) * (2.0 * math.pi / n)
    cl = jnp.where((k <= half)[:, None], jnp.cos(ang), 0.0)
    sl = jnp.where((k < half)[:, None], jnp.sin(ang), 0.0)
    j = lax.iota(jnp.int32, half)
    rev = k[None, :] == (half - j)[:, None]
    return cl.astype(BF16), sl.astype(BF16), rev.astype(BF16)


def _ssm_ops_kernel(ar_ref, ai_ref, ldt_ref, btr_ref, bti_ref, ctr_ref, cti_ref,
                    bm_ref, wh_ref, atr_ref, ati_ref, e_ref):
    t = SSM_CHUNK
    n = SLAB_STATE
    rows = lax.broadcasted_iota(jnp.int32, (LANES, n), 0)
    cols = lax.broadcasted_iota(jnp.int32, (LANES, n), 1)
    same_group = (rows // SSM_GROUP) == (cols // SSM_STATE)
    lag = lax.broadcasted_iota(jnp.int32, (t + 8, n), 0).astype(F32)
    nt_dims = (((1,), (1,)), ((), ()))
    for d in range(2):
        ar, ai = ar_ref[d], ai_ref[d]
        dt = jnp.exp(ldt_ref[d])
        xr, xi = ar * dt, ai * dt
        mag = jnp.exp(lag * xr)
        pr = mag * jnp.cos(lag * xi)
        pi = mag * jnp.sin(lag * xi)
        abr, abi = pr[1:2], pi[1:2]
        den = ar * ar + ai * ai
        fr = ((abr - 1.0) * ar + abi * ai) / den
        fi = (abi * ar - (abr - 1.0) * ai) / den
        br = jnp.where(same_group, btr_ref[d], 0.0)
        bi = jnp.where(same_group, bti_ref[d], 0.0)
        bbr = fr * br - fi * bi
        bbi = fr * bi + fi * br
        cr = jnp.where(same_group, ctr_ref[d], 0.0)
        ci = jnp.where(same_group, cti_ref[d], 0.0)
        for k in range(t):
            pkr, pki = pr[k:k + 1], pi[k:k + 1]
            x_r = bbr * pkr - bbi * pki
            x_i = bbr * pki + bbi * pkr
            s = t - 1 - k if d == 0 else k
            bm_ref[s * LANES:(s + 1) * LANES, 2 * d * n:(2 * d + 1) * n] = x_r.astype(BF16)
            bm_ref[s * LANES:(s + 1) * LANES, (2 * d + 1) * n:(2 * d + 2) * n] = x_i.astype(BF16)
            e_ref[d, k] = (lax.dot_general(x_r, cr, nt_dims, precision=lax.Precision.HIGHEST,
                                           preferred_element_type=F32)
                           - lax.dot_general(x_i, ci, nt_dims, precision=lax.Precision.HIGHEST,
                                             preferred_element_type=F32))
        for k in range(1, t + 1):
            pkr, pki = pr[k:k + 1], pi[k:k + 1]
            w_r = cr * pkr - ci * pki
            w_i = cr * pki + ci * pkr
            tt = k - 1 if d == 0 else t - k
            wh_ref[2 * d * n:(2 * d + 1) * n, tt * LANES:(tt + 1) * LANES] = w_r.T.astype(BF16)
            wh_ref[(2 * d + 1) * n:(2 * d + 2) * n, tt * LANES:(tt + 1) * LANES] = (-w_i).T.astype(BF16)
        atr_ref[d] = pr[t:t + 1]
        ati_ref[d] = pi[t:t + 1]


def _ssm_toeplitz_kernel(e_ref, dsk_ref, m_ref):
    t = SSM_CHUNK
    s = pl.program_id(1)
    r = lax.broadcasted_iota(jnp.int32, (LANES, LANES), 0)
    c = lax.broadcasted_iota(jnp.int32, (LANES, LANES), 1)
    skip = jnp.where(r == c, jnp.broadcast_to(dsk_ref[...], (LANES, LANES)), 0.0)
    for tt in range(t):
        blk = ((tt >= s).astype(F32) * e_ref[0, jnp.maximum(tt - s, 0)]
               + (s >= tt).astype(F32) * e_ref[1, jnp.maximum(s - tt, 0)]
               + (s == tt).astype(F32) * skip)
        m_ref[:, tt * LANES:(tt + 1) * LANES] = blk.astype(BF16)


def _ssm_ops(a_re, a_im, log_dt, b_re, b_im, c_re, c_im, d_skip):
    t = SSM_CHUNK
    n = SLAB_STATE

    def per_state(x):
        return x.astype(F32).reshape(2, SLABS, 1, n)

    def per_channel(x):
        x = x.astype(F32).reshape(2, SLABS, LANES, SSM_STATE)
        return jnp.tile(x, (1, 1, 1, SLAB_GROUPS))

    ldt = jnp.broadcast_to(log_dt[..., None], a_re.shape)
    vec = pl.BlockSpec((2, None, 1, n), lambda z: (0, z, 0, 0))
    mat = pl.BlockSpec((2, None, LANES, n), lambda z: (0, z, 0, 0))
    bm, wh, at_re, at_im, e = pl.pallas_call(
        _ssm_ops_kernel,
        out_shape=(jax.ShapeDtypeStruct((SLABS, t * LANES, 4 * n), BF16),
                   jax.ShapeDtypeStruct((SLABS, 4 * n, t * LANES), BF16),
                   jax.ShapeDtypeStruct((SLABS, 2, 1, n), F32),
                   jax.ShapeDtypeStruct((SLABS, 2, 1, n), F32),
                   jax.ShapeDtypeStruct((SLABS, 2, t, LANES, LANES), F32)),
        grid=(SLABS,),
        in_specs=[vec, vec, vec, mat, mat, mat, mat],
        out_specs=(pl.BlockSpec((None, t * LANES, 4 * n), lambda z: (z, 0, 0)),
                   pl.BlockSpec((None, 4 * n, t * LANES), lambda z: (z, 0, 0)),
                   pl.BlockSpec((None, 2, 1, n), lambda z: (z, 0, 0, 0)),
                   pl.BlockSpec((None, 2, 1, n), lambda z: (z, 0, 0, 0)),
                   pl.BlockSpec((None, 2, t, LANES, LANES), lambda z: (z, 0, 0, 0, 0))),
        compiler_params=_params(1), name="ssm_ops",
    )(per_state(a_re), per_state(a_im), per_state(ldt),
      per_channel(b_re.transpose(0, 1, 3, 2)), per_channel(b_im.transpose(0, 1, 3, 2)),
      per_channel(c_re), per_channel(c_im))
    m = pl.pallas_call(
        _ssm_toeplitz_kernel,
        out_shape=jax.ShapeDtypeStruct((SLABS, t * LANES, t * LANES), BF16),
        grid=(SLABS, t),
        in_specs=[pl.BlockSpec((None, 2, t, LANES, LANES), lambda z, s: (z, 0, 0, 0, 0)),
                  pl.BlockSpec((None, 1, LANES), lambda z, s: (z, 0, 0))],
        out_specs=pl.BlockSpec((None, LANES, t * LANES), lambda z, s: (z, s, 0)),
        compiler_params=_params(2), name="ssm_toeplitz",
    )(e, d_skip.astype(F32).reshape(SLABS, 1, LANES))
    return m, bm, wh, at_re.reshape(SLABS * 2, 1, n), at_im.reshape(SLABS * 2, 1, n)


def _ssm_contrib_kernel(u_ref, bm_ref, o_ref):
    o_ref[...] = jnp.dot(u_ref[...], bm_ref[...], preferred_element_type=F32).astype(BF16)


def _ssm_contrib(u, bm):
    slabs, r, k = u.shape
    n = bm.shape[2]
    tr = _tile(r, SSM_ROW_TILE)
    return pl.pallas_call(
        _ssm_contrib_kernel,
        out_shape=jax.ShapeDtypeStruct((slabs, r, n), BF16),
        grid=(slabs, r // tr),
        in_specs=[pl.BlockSpec((None, tr, k), lambda s, i: (s, i, 0)),
                  pl.BlockSpec((None, k, n), lambda s, i: (s, 0, 0))],
        out_specs=pl.BlockSpec((None, tr, n), lambda s, i: (s, i, 0)),
        compiler_params=_params(2), name="ssm_contrib",
    )(u, bm)


SCAN_UNROLL = 8


def _ssm_scan_kernel(ar_ref, ai_ref, cr_ref, ci_ref, hr_ref, hi_ref, *, ncc, nct, rb):
    d = pl.program_id(1)
    lt = ar_ref.shape[-1]
    ncx = nct - ncc
    ar = jnp.broadcast_to(ar_ref[...], (rb, lt))
    ai = jnp.broadcast_to(ai_ref[...], (rb, lt))
    zero = jnp.zeros((rb, lt), F32)

    def advance(i_src, i_dst, hr, hi):
        r0 = pl.multiple_of(i_src * rb, rb)
        r1 = pl.multiple_of(i_dst * rb, rb)
        nr = ar * hr - ai * hi + cr_ref[pl.ds(r0, rb), :].astype(F32)
        ni = ar * hi + ai * hr + ci_ref[pl.ds(r0, rb), :].astype(F32)
        hr_ref[pl.ds(r1, rb), :] = nr.astype(BF16)
        hi_ref[pl.ds(r1, rb), :] = ni.astype(BF16)
        return nr, ni

    def start(i):
        hr_ref[pl.ds(i * rb, rb), :] = zero.astype(BF16)
        hi_ref[pl.ds(i * rb, rb), :] = zero.astype(BF16)
        return zero, zero

    @pl.when(d == 0)
    def _():
        c = start(ncx)
        c = lax.fori_loop(0, ncc - 1, lambda k, c: advance(ncx + k, ncx + k + 1, *c), c, unroll=SCAN_UNROLL)
        c = advance(nct - 1, 0, *c)
        lax.fori_loop(0, ncx - 1, lambda k, c: advance(k, k + 1, *c), c, unroll=SCAN_UNROLL)

    @pl.when(d == 1)
    def _():
        c = start(nct - 1)
        c = lax.fori_loop(0, ncc - 1, lambda k, c: advance(nct - 1 - k, nct - 2 - k, *c), c, unroll=SCAN_UNROLL)
        c = advance(ncx, ncx - 1, *c)
        lax.fori_loop(0, ncx - 1, lambda k, c: advance(ncx - 1 - k, ncx - 2 - k, *c), c, unroll=SCAN_UNROLL)


def _ssm_scan(contrib, at_re, at_im, ncc, nct, rb):
    slabs, r, n = contrib.shape
    lt = 2 * LANES
    q = SLAB_STATE // lt
    re_spec = pl.BlockSpec((None, r, lt), lambda s, d, j: (s, 0, d * 2 * q + j))
    im_spec = pl.BlockSpec((None, r, lt), lambda s, d, j: (s, 0, d * 2 * q + q + j))
    a_spec = pl.BlockSpec((None, 1, lt), lambda s, d, j: (s * 2 + d, 0, j))
    h_spec = pl.BlockSpec((None, r, lt), lambda s, d, j: (s, 0, d * q + j))
    out = jax.ShapeDtypeStruct((slabs, r, n // 2), BF16)
    return pl.pallas_call(
        functools.partial(_ssm_scan_kernel, ncc=ncc, nct=nct, rb=rb),
        out_shape=(out, out),
        grid=(slabs, 2, q),
        in_specs=[a_spec, a_spec, re_spec, im_spec],
        out_specs=(h_spec, h_spec),
        compiler_params=_params(3), name="ssm_scan",
    )(at_re, at_im, contrib, contrib)


def _ssm_read_kernel(u_ref, hr_ref, hi_ref, m_ref, wh_ref, o_ref):
    q = SLAB_STATE
    acc = jnp.dot(u_ref[...], m_ref[...], preferred_element_type=F32)
    for d in range(2):
        acc += jnp.dot(hr_ref[:, d * q:(d + 1) * q], wh_ref[2 * d * q:(2 * d + 1) * q, :],
                       preferred_element_type=F32)
        acc += jnp.dot(hi_ref[:, d * q:(d + 1) * q], wh_ref[(2 * d + 1) * q:(2 * d + 2) * q, :],
                       preferred_element_type=F32)
    o_ref[...] = acc.astype(BF16)


def _ssm_read(u, h_re, h_im, m, wh):
    slabs, r, k = u.shape
    ns = h_re.shape[2]
    nw = wh.shape[1]
    tr = _tile(r, SSM_ROW_TILE)
    tn = k // 2
    return pl.pallas_call(
        _ssm_read_kernel,
        out_shape=jax.ShapeDtypeStruct((slabs, r, k), BF16),
        grid=(slabs, 2, r // tr),
        in_specs=[pl.BlockSpec((None, tr, k), lambda s, j, i: (s, i, 0)),
                  pl.BlockSpec((None, tr, ns), lambda s, j, i: (s, i, 0)),
                  pl.BlockSpec((None, tr, ns), lambda s, j, i: (s, i, 0)),
                  pl.BlockSpec((None, k, tn), lambda s, j, i: (s, 0, j)),
                  pl.BlockSpec((None, nw, tn), lambda s, j, i: (s, 0, j))],
        out_specs=pl.BlockSpec((None, tr, tn), lambda s, j, i: (s, i, j)),
        compiler_params=_params(3), name="ssm_read",
    )(u, h_re, h_im, m, wh)


MLP_FF_CHUNK = 2048


def _mlp_tail(x, g_ref, sh_ref, sc_ref, gate_ref, w1_ref, w2_ref, o_ref):
    fc = MLP_FF_CHUNK
    h = _norm_mod(x, g_ref[...], sh_ref[...], sc_ref[...]).reshape(-1, D_MODEL).astype(BF16)
    acc = jnp.zeros(h.shape, F32)
    for k in range(D_FF // fc):
        a = jnp.dot(h, w1_ref[:, k * fc:(k + 1) * fc], preferred_element_type=F32)
        a = jnp.square(jnp.maximum(a, 0.0)).astype(BF16)
        acc += jnp.dot(a, w2_ref[k * fc:(k + 1) * fc, :], preferred_element_type=F32)
    o_ref[...] = x + gate_ref[...] * acc.reshape(x.shape)


def _even_tail_kernel(ya_ref, y_ref, x_ref, gate1_ref, wg_ref, bg_ref, wa_ref, wb_ref,
                      g_ref, sh_ref, sc_ref, gate2_ref, w1_ref, w2_ref, o_ref, *ys_scr):
    nb, t_len, _ = x_ref.shape
    for s in range(SLABS):
        for t in range(t_len):
            ys_scr[s][pl.ds(t, nb, stride=t_len), :] = y_ref[s, :, t * LANES:(t + 1) * LANES].astype(F32)
    y = _gelu_tanh(jnp.concatenate([r[...] for r in ys_scr], axis=1))
    glu = y * _sigmoid(jnp.dot(y.astype(BF16), wg_ref[...], preferred_element_type=F32) + bg_ref[...])
    acc = jnp.dot(ya_ref[...].reshape(nb * t_len, D_FOURIER), wa_ref[...], preferred_element_type=F32)
    acc += jnp.dot(glu.astype(BF16), wb_ref[...], preferred_element_type=F32)
    x1 = x_ref[...] + gate1_ref[...] * acc.reshape(nb, t_len, D_MODEL)
    _mlp_tail(x1, g_ref, sh_ref, sc_ref, gate2_ref, w1_ref, w2_ref, o_ref)


def _odd_tail_kernel(a_ref, x_ref, gate1_ref, wo_ref, g_ref, sh_ref, sc_ref, gate2_ref, w1_ref, w2_ref, o_ref):
    a = a_ref[...].reshape(x_ref.shape)
    x1 = x_ref[...] + gate1_ref[...] * jnp.dot(a, wo_ref[...], preferred_element_type=F32)
    _mlp_tail(x1, g_ref, sh_ref, sc_ref, gate2_ref, w1_ref, w2_ref, o_ref)


def _mlp_specs(rows):
    return [_const_spec((1, D_MODEL)), rows.mod(3), rows.mod(4), rows.mod(5),
            _const_spec((D_MODEL, D_FF)), _const_spec((D_FF, D_MODEL))]


def _even_tail(ya, y, first_chunk, x, rows, mod, w_glu, b_glu, w_out, norm_g, w1, w2):
    return pl.pallas_call(
        _even_tail_kernel, out_shape=jax.ShapeDtypeStruct(x.shape, F32), grid=rows.grid,
        in_specs=[rows.rows(D_FOURIER), rows.chunk(first_chunk), rows.rows(D_MODEL), rows.mod(2),
                  _const_spec((D_SSM, D_SSM)), _const_spec((1, D_SSM)),
                  _const_spec((D_FOURIER, D_MODEL)), _const_spec((D_SSM, D_MODEL))] + _mlp_specs(rows),
        out_specs=rows.rows(D_MODEL),
        scratch_shapes=[pltpu.VMEM((rows.tm, LANES), F32)] * SLABS,
        compiler_params=_params(1), name="even_tail",
    )(ya, y, x, mod, w_glu, b_glu, w_out[:D_FOURIER], w_out[D_FOURIER:], norm_g, mod, mod, mod, w1, w2)


def _odd_tail(a, x, rows, mod, w_out, norm_g, w1, w2):
    return pl.pallas_call(
        _odd_tail_kernel, out_shape=jax.ShapeDtypeStruct(x.shape, F32), grid=rows.grid,
        in_specs=[rows.rows3(a.shape[1], D_MODEL), rows.rows(D_MODEL), rows.mod(2),
                  _const_spec((D_MODEL, D_MODEL))] + _mlp_specs(rows),
        out_specs=rows.rows(D_MODEL),
        compiler_params=_params(2), name="odd_tail",
    )(a, x, mod, w_out, norm_g, mod, mod, mod, w1, w2)


KEY_CHUNK = 256
V_ROWS = DA_V_DIM + 16
Q_SCALE = DA_HEAD_DIM ** -0.5 * math.log2(math.e)


def _qkv_kernel(*refs, rope):
    if rope:
        (x_ref, g_ref, sh_ref, sc_ref, w_ref, gqk_ref, ones_ref,
         cos_ref, sa_ref, sb_ref, qt_ref, k_ref, vt_ref) = refs
    else:
        x_ref, g_ref, sh_ref, sc_ref, w_ref, gqk_ref, ones_ref, qt_ref, k_ref, vt_ref = refs
    h = _norm_mod(x_ref[...], g_ref[...], sh_ref[...], sc_ref[...]).astype(BF16)
    y = jnp.dot(h, w_ref[...], preferred_element_type=F32)
    tm = y.shape[0]
    for s in range(2 * DA_HEADS):
        blk = y[:, s * LANES:(s + 1) * LANES]
        ssq = jnp.dot((blk * blk).astype(BF16), ones_ref[...], preferred_element_type=F32)
        nrm = blk * gqk_ref[s // DA_HEADS]
        if rope:
            nrm = (nrm * cos_ref[...] + pltpu.roll(nrm, LANES - 16, 1) * sa_ref[...]
                   + pltpu.roll(nrm, 16, 1) * sb_ref[...])
        nrm = nrm * lax.rsqrt(ssq * (1.0 / DA_HEAD_DIM) + EPS)
        if s < DA_HEADS:
            qt_ref[s] = nrm.T.astype(BF16)
        else:
            hd = s - DA_HEADS
            k_ref[:, hd * LANES:(hd + 1) * LANES] = nrm.astype(BF16)
    ones = jnp.ones((V_ROWS - DA_V_DIM, tm), BF16)
    for hd in range(DA_HEADS):
        v = y[:, (2 * DA_HEADS + hd) * LANES:(2 * DA_HEADS + hd + 1) * LANES]
        vt_ref[hd, :DA_V_DIM, :] = v.T.astype(BF16)
        vt_ref[hd, DA_V_DIM:, :] = ones


def _qkv(x, rows, norm_g, mod, w_in, gqk, ones_bd, rope_tabs):
    nb, rpb, tm = rows.nb, rows.rpb, rows.tm
    assert tm % KEY_CHUNK == 0
    rope = rope_tabs is not None
    in_specs = [rows.rows(D_MODEL), _const_spec((1, D_MODEL)), rows.mod(0), rows.mod(1),
                _const_spec((D_MODEL, 3 * D_MODEL)), _const_spec((2, 1, LANES)),
                _const_spec((LANES, LANES))]
    args = [x, norm_g, mod, mod, w_in, gqk, ones_bd]
    if rope:
        tab = pl.BlockSpec((tm, LANES), lambda b, i: (i, 0))
        in_specs += [tab, tab, tab]
        args += list(rope_tabs)
    out_shape = (jax.ShapeDtypeStruct((nb, DA_HEADS, LANES, rpb), BF16),
                 jax.ShapeDtypeStruct((nb, rpb, D_MODEL), BF16),
                 jax.ShapeDtypeStruct((nb, DA_HEADS, V_ROWS, rpb), BF16))
    out_specs = (pl.BlockSpec((None, DA_HEADS, LANES, tm), lambda b, i: (b, 0, 0, i)),
                 pl.BlockSpec((None, tm, D_MODEL), lambda b, i: (b, i, 0)),
                 pl.BlockSpec((None, DA_HEADS, V_ROWS, tm), lambda b, i: (b, 0, 0, i)))
    return pl.pallas_call(
        functools.partial(_qkv_kernel, rope=rope),
        out_shape=out_shape, grid=rows.grid, in_specs=in_specs, out_specs=out_specs,
        compiler_params=_params(2), name="qkv",
    )(*args)


def _rope_tables(length):
    pos = jnp.arange(length, dtype=jnp.int32)
    row = (pos // GRID_W).astype(F32)
    col = (pos % GRID_W).astype(F32)
    half = DA_HEAD_DIM // 2
    inv_freq = ROPE_THETA ** (-jnp.arange(0, half, 2, dtype=F32) / half)
    lane = jnp.arange(LANES)
    f = lane % 16
    is_col = (lane % DA_HEAD_DIM) >= half
    second = (lane % half) >= 16
    ang = jnp.where(is_col[None, :], col[:, None], row[:, None]) * inv_freq[f][None, :]
    cos, sin = jnp.cos(ang), jnp.sin(ang)
    sin_a = jnp.where(second[None, :], 0.0, -sin)
    sin_b = jnp.where(second[None, :], sin, 0.0)
    return cos, sin_a, sin_b


def _attn_kernel(*refs, n_kv, lam_init, tq):
    lp_ref, gh_ref, qt_ref = refs[:3]
    kv_refs = refs[3:3 + 2 * n_kv]
    o_ref = refs[3 + 2 * n_kv]
    bufs = refs[4 + 2 * n_kv:]
    lp = lp_ref[...]
    lam = (jnp.exp(jnp.sum(lp[0:1] * lp[1:2], axis=-1, keepdims=True))
           - jnp.exp(jnp.sum(lp[2:3] * lp[3:4], axis=-1, keepdims=True)) + lam_init)
    hb = qt_ref.shape[0]
    items = [(hh, j) for hh in range(hb) for j in range(qt_ref.shape[2] // tq)]
    kc = KEY_CHUNK

    key_chunks = []
    for i in range(n_kv):
        for c in range(kv_refs[2 * i].shape[0] // kc):
            key_chunks.append((kv_refs[2 * i], kv_refs[2 * i + 1], c))

    def masked_q(item):
        hh, j = item
        qt = qt_ref[hh, :, j * tq:(j + 1) * tq]
        row = lax.broadcasted_iota(jnp.int32, qt.shape, 0)
        zero = jnp.zeros_like(qt)
        return jnp.concatenate([jnp.where(row < DA_HEAD_DIM, qt, zero),
                                jnp.where(row >= DA_HEAD_DIM, qt, zero)], axis=1)

    def score_chunk(n, hh, qst, s_scr, m):
        k_ref, _, c = key_chunks[n]
        s = jnp.dot(k_ref[c * kc:(c + 1) * kc, hh * LANES:(hh + 1) * LANES], qst,
                    preferred_element_type=F32)
        s_scr[n * kc:(n + 1) * kc, :] = s
        return jnp.maximum(m, s.reshape(kc // 8, 8, 2 * tq).max(axis=0))

    def value_chunk(n, hh, s_scr, m, acc):
        _, vt_ref, c = key_chunks[n]
        r = pl.multiple_of(n * kc + jnp.minimum(pl.program_id(0), 0) * kc, kc)
        p = jnp.exp2((s_scr[pl.ds(r, kc), :] - m).astype(BF16))
        return acc + jnp.dot(vt_ref[hh, :, c * kc:(c + 1) * kc], p, preferred_element_type=F32)

    def finish(item, acc):
        hh, j = item
        o2 = acc[:DA_V_DIM] / acc[DA_V_DIM:DA_V_DIM + 1]
        o = o2[:, :tq] - lam * o2[:, tq:]
        o = o * lax.rsqrt(jnp.mean(o * o, axis=0, keepdims=True) + EPS) * gh_ref[...]
        o_ref[j * tq:(j + 1) * tq, hh * LANES:(hh + 1) * LANES] = (o * (1.0 - lam_init)).T.astype(BF16)

    neg = jnp.full((8, 2 * tq), -jnp.inf, F32)
    qst = masked_q(items[0])
    m8 = neg
    for n in range(len(key_chunks)):
        m8 = score_chunk(n, items[0][0], qst, bufs[0], m8)
    for i, item in enumerate(items):
        nxt = items[i + 1] if i + 1 < len(items) else None
        m = m8.max(axis=0, keepdims=True)
        acc = jnp.zeros((V_ROWS, 2 * tq), F32)
        if nxt is not None:
            qst = masked_q(nxt)
            m8 = neg
        for n in range(len(key_chunks)):
            if nxt is not None:
                m8 = score_chunk(n, nxt[0], qst, bufs[(i + 1) % 2], m8)
            acc = value_chunk(n, item[0], bufs[i % 2], m, acc)
        finish(item, acc)


def _attention(qt, kv_list, lam_p, g_head, lam_init, tq, hb):
    nb, _, _, lq = qt.shape
    in_specs = [_const_spec((4, DA_HEAD_DIM)), _const_spec((DA_V_DIM, 1)),
                pl.BlockSpec((None, hb, LANES, lq), lambda b, h: (b, h, 0, 0))]
    args = [lam_p, g_head, qt]
    total = 0
    for k, vt in kv_list:
        lk = k.shape[1]
        total += lk
        in_specs.append(pl.BlockSpec((None, lk, hb * LANES), lambda b, h: (b, 0, h)))
        in_specs.append(pl.BlockSpec((None, hb, V_ROWS, lk), lambda b, h: (b, h, 0, 0)))
        args += [k, vt]
    n_bufs = min(2, hb * (lq // tq))
    return pl.pallas_call(
        functools.partial(_attn_kernel, n_kv=len(kv_list), lam_init=lam_init, tq=tq),
        out_shape=jax.ShapeDtypeStruct((nb, lq, D_MODEL), BF16),
        grid=(nb, DA_HEADS // hb),
        in_specs=in_specs,
        out_specs=pl.BlockSpec((None, lq, hb * LANES), lambda b, h: (b, 0, h)),
        scratch_shapes=[pltpu.VMEM((total, 2 * tq), F32)] * n_bufs,
        compiler_params=_params(2), name="diff_attn",
    )(*args)


def kernel(x, c, ctx, c_ctx, norm1_g, norm2_g, ada_w, ada_b, mlp_w1, mlp_w2, ev_w_in, ev_w_out, ssm_a_re, ssm_a_im, ssm_log_dt, ssm_b_re, ssm_b_im, ssm_c_re, ssm_c_im, ssm_d, ssm_w_glu, ssm_b_glu, od_w_in, od_w_out, od_q_norm, od_k_norm, od_lambda, od_head_norm):
    nb, length, d = x.shape
    clen = ctx.shape[1]
    assert d == D_MODEL and length % GRID_W == 0
    assert length % SSM_CHUNK == 0 and clen % SSM_CHUNK == 0

    xs = x.reshape(nb * length, d)
    cs = ctx.reshape(nb * clen, d)
    rows_x = _Rows(nb, length, _tile(length, ROW_TILE), None)
    rows_c = _Rows(1, nb * clen, _tile(nb * clen, ROW_TILE), nb)
    rows_cb = _Rows(nb, clen, clen, nb)
    chunks_x = _ChunkRows(nb, length // SSM_CHUNK, None)
    chunks_c = _ChunkRows(nb, clen // SSM_CHUNK, nb)

    n_cond = nb + 1
    pad = (-n_cond) % 8
    cond = jnp.concatenate([c, c_ctx[None, :], jnp.zeros((pad, d), F32)], axis=0)
    mods = _adaln(cond, ada_w, ada_b)[:, :n_cond].reshape(DEPTH, n_cond, 1, 6 * d)

    kf = lax.iota(jnp.int32, FOURIER_CH)
    ang = ((kf[:, None] * kf[None, :]) % FOURIER_CH).astype(F32) * (2.0 * math.pi / FOURIER_CH)
    cs_ch = jnp.concatenate([jnp.cos(ang), jnp.sin(ang)], axis=1).astype(BF16)
    dft_x = _dft_mats(length)
    dft_c = _dft_mats(clen)
    rope_tabs = _rope_tables(length)
    lane = jnp.arange(LANES)
    ones_bd = (lane[:, None] // DA_HEAD_DIM == lane[None, :] // DA_HEAD_DIM).astype(BF16)

    ncc = clen // SSM_CHUNK
    ncx = length // SSM_CHUNK
    nct = ncx + ncc

    for i in range(DEPTH):
        last = i == DEPTH - 1
        j = i // 2
        mod = mods[i]
        n1 = norm1_g[i].reshape(1, d)
        n2 = norm2_g[i].reshape(1, d)
        if i % 2 == 0:
            w_in = ev_w_in[j].astype(BF16)
            u_shape = (SLABS, nct, nb, SSM_CHUNK * LANES)
            x3 = xs.reshape(nb, length, d)
            c3 = cs.reshape(nb, clen, d)
            u = _zero_chunks(u_shape)
            vc_x, vs_x, u = _even_in(x3, chunks_x, n1, mod, w_in, cs_ch, u, 0)
            vc_c, vs_c, u = _even_in(c3, chunks_c, n1, mod, w_in, cs_ch, u, ncx)
            ya_x = _fourier(vc_x, vs_x, dft_x)
            m, bm, wh, at_re, at_im = _ssm_ops(ssm_a_re[j], ssm_a_im[j], ssm_log_dt[j], ssm_b_re[j],
                                                ssm_b_im[j], ssm_c_re[j], ssm_c_im[j], ssm_d[j])
            u = u.reshape(SLABS, nct * nb, SSM_CHUNK * LANES)
            contrib = _ssm_contrib(u, bm)
            h_re, h_im = _ssm_scan(contrib, at_re, at_im, ncc, nct, nb)
            y = _ssm_read(u, h_re, h_im, m, wh).reshape(u_shape)
            w_glu = ssm_w_glu[j].astype(BF16)
            b_glu = ssm_b_glu[j].reshape(1, D_SSM)
            w_out = ev_w_out[j].astype(BF16)
            w1 = mlp_w1[i].astype(BF16)
            w2 = mlp_w2[i].astype(BF16)
            xs = _even_tail(ya_x, y, 0, x3, chunks_x, mod, w_glu, b_glu, w_out, n2, w1, w2).reshape(nb * length, d)
            if not last:
                ya_c = _fourier(vc_c, vs_c, dft_c)
                cs = _even_tail(ya_c, y, ncx, c3, chunks_c, mod, w_glu, b_glu, w_out, n2, w1, w2).reshape(nb * clen, d)
        else:
            lam_init = 0.8 - 0.6 * math.exp(-0.3 * i)
            w_in = od_w_in[j].astype(BF16)
            gqk = jnp.stack([jnp.tile(od_q_norm[j], 2) * Q_SCALE,
                             jnp.tile(od_k_norm[j], 2)]).reshape(2, 1, LANES)
            qt_x, k_x, vt_x = _qkv(xs, rows_x, n1, mod, w_in, gqk, ones_bd, rope_tabs)
            qt_c, k_c, vt_c = _qkv(cs, rows_cb, n1, mod, w_in, gqk, ones_bd, None)
            g_head = od_head_norm[j].reshape(DA_V_DIM, 1)
            w_out = od_w_out[j].astype(BF16)
            o_x = _attention(qt_x, [(k_x, vt_x), (k_c, vt_c)], od_lambda[j], g_head, lam_init,
                             _tile(length, ATTN_Q_TILE, LANES), ATTN_HEADS_PER_STEP)
            w1 = mlp_w1[i].astype(BF16)
            w2 = mlp_w2[i].astype(BF16)
            xs = _odd_tail(o_x, xs, rows_x, mod, w_out, n2, w1, w2)
            if not last:
                o_c = _attention(qt_c, [(k_c, vt_c)], od_lambda[j], g_head, lam_init, clen, DA_HEADS)
                cs = _odd_tail(o_c, cs, rows_c, mod, w_out, n2, w1, w2)
    return xs.reshape(nb, length, d)
```

```python
import functools
import math

import jax
import jax.numpy as jnp
from jax import lax
from jax.experimental import pallas as pl
from jax.experimental.pallas import tpu as pltpu

F32 = jnp.float32
BF16 = jnp.bfloat16

D_MODEL = 1024
DEPTH = 4
GRID_W = 64
EPS = 1e-6
D_FF = 4 * D_MODEL
D_FOURIER = D_MODEL // 2
FOURIER_GROUPS = 4
FOURIER_CH = D_FOURIER // FOURIER_GROUPS
D_SSM = D_MODEL - D_FOURIER
SSM_GROUP = 16
SSM_STATE = 64
DA_HEAD_DIM = 64
DA_V_DIM = 2 * DA_HEAD_DIM
DA_HEADS = D_MODEL // DA_V_DIM
ROPE_THETA = 10000.0

LANES = 128
SSM_CHUNK = 16
SLABS = D_SSM // LANES
SLAB_GROUPS = LANES // SSM_GROUP
SLAB_STATE = SLAB_GROUPS * SSM_STATE
VMEM_LIMIT = 56 * 1024 * 1024
ROW_TILE = 512
SSM_ROW_TILE = 576
ADALN_COL_TILE = 1536
ATTN_Q_TILE = 512
ATTN_HEADS_PER_STEP = 2


def _params(n_axes):
    return pltpu.CompilerParams(dimension_semantics=("arbitrary",) * n_axes, vmem_limit_bytes=VMEM_LIMIT)


def _tile(n, target, mult=8):
    t = min(n, target)
    while t > 1 and (n % t or t % mult):
        t -= 1
    return t if n % t == 0 else n


def _const_spec(shape):
    nd = len(shape)
    return pl.BlockSpec(shape, lambda *_: (0,) * nd, pipeline_mode=pl.Buffered(1))


def _sigmoid(x):
    return 1.0 / (1.0 + jnp.exp(-x))


def _gelu_tanh(x):
    c = math.sqrt(2.0 / math.pi)
    return x * (0.5 * (1.0 + jnp.tanh(c * (x + 0.044715 * (x * x * x)))))


def _norm_mod(x, g, shift, scale):
    ms = jnp.mean(x * x, axis=-1, keepdims=True)
    y = x * lax.rsqrt(ms + EPS) * g
    return y * (1.0 + scale) + shift


def _adaln_kernel(c_ref, w_ref, b_ref, o_ref):
    c = c_ref[...]
    s = (c * _sigmoid(c)).astype(BF16)
    o_ref[0] = jnp.dot(s, w_ref[0].astype(BF16), preferred_element_type=F32) + b_ref[0]


def _adaln(cond, ada_w, ada_b):
    depth, d, n = ada_w.shape
    r = cond.shape[0]
    tn = _tile(n, ADALN_COL_TILE, LANES)
    return pl.pallas_call(
        _adaln_kernel,
        out_shape=jax.ShapeDtypeStruct((depth, r, n), F32),
        grid=(depth, n // tn),
        in_specs=[pl.BlockSpec((r, d), lambda l, j: (0, 0)),
                  pl.BlockSpec((1, d, tn), lambda l, j: (l, 0, j)),
                  pl.BlockSpec((1, 1, tn), lambda l, j: (l, 0, j))],
        out_specs=pl.BlockSpec((1, r, tn), lambda l, j: (l, 0, j)),
        compiler_params=_params(2), name="adaln",
    )(cond, ada_w, ada_b.reshape(depth, 1, n))


class _Rows:
    def __init__(self, nb, rpb, tm, mod_row):
        self.nb, self.rpb, self.tm = nb, rpb, tm
        self.nt = rpb // tm
        self.grid = (nb, self.nt)
        self.mod_row = mod_row

    def rows(self, cols):
        nt = self.nt
        return pl.BlockSpec((self.tm, cols), lambda b, i: (b * nt + i, 0))

    def rows3(self, per, cols):
        nt, tm = self.nt, self.tm
        if tm <= per:
            k = per // tm
            return pl.BlockSpec((None, tm, cols), lambda b, i: ((b * nt + i) // k, (b * nt + i) % k, 0))
        return pl.BlockSpec((tm // per, per, cols), lambda b, i: (b * nt + i, 0, 0))

    def mod(self, k):
        if self.mod_row is None:
            return pl.BlockSpec((None, 1, D_MODEL), lambda b, i: (b, 0, k))
        r = self.mod_row
        return pl.BlockSpec((None, 1, D_MODEL), lambda b, i: (r, 0, k))


class _ChunkRows:
    def __init__(self, nb, n_chunks, mod_row):
        self.nb, self.grid, self.mod_row = nb, (n_chunks,), mod_row
        self.tm = nb * SSM_CHUNK

    def rows(self, cols):
        return pl.BlockSpec((self.nb, SSM_CHUNK, cols), lambda c: (0, c, 0))

    def mod(self, k):
        if self.mod_row is None:
            return pl.BlockSpec((self.nb, 1, D_MODEL), lambda c: (0, 0, k))
        r = self.mod_row
        return pl.BlockSpec((1, 1, D_MODEL), lambda c: (r, 0, k))

    def chunk(self, first_chunk):
        return pl.BlockSpec((SLABS, None, self.nb, SSM_CHUNK * LANES), lambda c: (0, first_chunk + c, 0, 0))


def _zero_chunks_kernel(o_ref):
    o_ref[...] = jnp.zeros(o_ref.shape, o_ref.dtype)


def _zero_chunks(u_shape):
    slabs, n_chunks, nb, width = u_shape
    tc = _tile(n_chunks, SSM_CHUNK, 1)
    return pl.pallas_call(
        _zero_chunks_kernel, out_shape=jax.ShapeDtypeStruct(u_shape, BF16), grid=(slabs, n_chunks // tc),
        out_specs=pl.BlockSpec((None, tc, nb, width), lambda s, c: (s, c, 0, 0)),
        compiler_params=_params(2), name="zero_chunks",
    )()


def _even_in_kernel(u_in_ref, x_ref, g_ref, sh_ref, sc_ref, w_ref, cs_ref, vc_ref, vs_ref, u_ref, *zs_scr):
    del u_in_ref
    _even_in_body(x_ref[...], g_ref, sh_ref, sc_ref, w_ref, cs_ref, vc_ref, vs_ref, u_ref, zs_scr)


def _even_in_body(x, g_ref, sh_ref, sc_ref, w_ref, cs_ref, vc_ref, vs_ref, u_ref, zs_scr):
    nb, t_len, _ = x.shape
    h = _norm_mod(x, g_ref[...], sh_ref[...], sc_ref[...])
    z = jnp.dot(h.reshape(nb * t_len, D_MODEL).astype(BF16), w_ref[...], preferred_element_type=F32)
    zf = z[:, :D_FOURIER].astype(BF16)
    vc, vs = [], []
    for gi in range(FOURIER_GROUPS):
        v = jnp.dot(zf[:, gi * FOURIER_CH:(gi + 1) * FOURIER_CH], cs_ref[...], preferred_element_type=F32)
        vc.append(v[:, :FOURIER_CH].astype(BF16))
        vs.append(v[:, FOURIER_CH:].astype(BF16))
    vc_ref[...] = jnp.concatenate(vc, axis=1).reshape(nb, t_len, D_FOURIER)
    vs_ref[...] = jnp.concatenate(vs, axis=1).reshape(nb, t_len, D_FOURIER)
    for s in range(SLABS):
        zs_scr[s][...] = z[:, D_FOURIER + s * LANES:D_FOURIER + (s + 1) * LANES]
        for t in range(t_len):
            u_ref[s, :, t * LANES:(t + 1) * LANES] = zs_scr[s][pl.ds(t, nb, stride=t_len), :].astype(BF16)


def _even_in(x, rows, norm_g, mod, w_in, cs, u, first_chunk):
    nb, n_tok, _ = x.shape
    outs = (jax.ShapeDtypeStruct((nb, n_tok, D_FOURIER), BF16), jax.ShapeDtypeStruct((nb, n_tok, D_FOURIER), BF16),
            jax.ShapeDtypeStruct(u.shape, BF16))
    return pl.pallas_call(
        _even_in_kernel, out_shape=outs, grid=rows.grid,
        in_specs=[pl.BlockSpec(memory_space=pl.ANY), rows.rows(D_MODEL), _const_spec((1, D_MODEL)),
                  rows.mod(0), rows.mod(1), _const_spec((D_MODEL, D_MODEL)),
                  _const_spec((FOURIER_CH, 2 * FOURIER_CH))],
        out_specs=(rows.rows(D_FOURIER), rows.rows(D_FOURIER), rows.chunk(first_chunk)),
        scratch_shapes=[pltpu.VMEM((rows.tm, LANES), F32)] * SLABS,
        input_output_aliases={0: 2},
        compiler_params=_params(1), name="even_in",
    )(u, x, norm_g, mod, mod, w_in, cs)


DFT_PAD = 16


def _fourier_kernel(cl_ref, sl_ref, rev_ref, vc_ref, vs_ref, o_ref, *, scale):
    half = rev_ref.shape[0]
    a = jnp.dot(cl_ref[...], vc_ref[0], preferred_element_type=F32)
    b = jnp.dot(sl_ref[...], vs_ref[0], preferred_element_type=F32)
    o_ref[0, :half, :] = ((a[:half] - b[:half]) * scale).astype(BF16)
    mirrored = ((a + b) * scale).astype(BF16)
    o_ref[0, half:, :] = jnp.dot(rev_ref[...], mirrored, preferred_element_type=F32).astype(BF16)


def _fourier(vc3, vs3, mats):
    cl, sl, rev = mats
    nb, length, _ = vc3.shape
    scale = 1.0 / math.sqrt(length * FOURIER_CH)
    blk = pl.BlockSpec((1, length, D_FOURIER), lambda b: (b, 0, 0))
    return pl.pallas_call(
        functools.partial(_fourier_kernel, scale=scale),
        out_shape=jax.ShapeDtypeStruct((nb, length, D_FOURIER), BF16),
        grid=(nb,),
        in_specs=[_const_spec(cl.shape), _const_spec(sl.shape), _const_spec(rev.shape), blk, blk],
        out_specs=blk,
        compiler_params=_params(1), name="fourier",
    )(cl, sl, rev, vc3, vs3)


def _dft_mats(n):
    half = n // 2
    k = lax.iota(jnp.int32, half + DFT_PAD)
    t = lax.iota(jnp.int32, n)
    ang = ((k[:, None] * t[None, :]) % n).astype(F32) * (2.0 * math.pi / n)
    cl = jnp.where((k <= half)[:, None], jnp.cos(ang), 0.0)
    sl = jnp.where((k < half)[:, None], jnp.sin(ang), 0.0)
    j = lax.iota(jnp.int32, half)
    rev = k[None, :] == (half - j)[:, None]
    return cl.astype(BF16), sl.astype(BF16), rev.astype(BF16)


def _ssm_ops_kernel(ar_ref, ai_ref, ldt_ref, btr_ref, bti_ref, ctr_ref, cti_ref,
                    bm_ref, wh_ref, atr_ref, ati_ref, e_ref):
    t = SSM_CHUNK
    n = SLAB_STATE
    rows = lax.broadcasted_iota(jnp.int32, (LANES, n), 0)
    cols = lax.broadcasted_iota(jnp.int32, (LANES, n), 1)
    same_group = (rows // SSM_GROUP) == (cols // SSM_STATE)
    lag = lax.broadcasted_iota(jnp.int32, (t + 8, n), 0).astype(F32)
    nt_dims = (((1,), (1,)), ((), ()))
    for d in range(2):
        ar, ai = ar_ref[d], ai_ref[d]
        dt = jnp.exp(ldt_ref[d])
        xr, xi = ar * dt, ai * dt
        mag = jnp.exp(lag * xr)
        pr = mag * jnp.cos(lag * xi)
        pi = mag * jnp.sin(lag * xi)
        abr, abi = pr[1:2], pi[1:2]
        den = ar * ar + ai * ai
        fr = ((abr - 1.0) * ar + abi * ai) / den
        fi = (abi * ar - (abr - 1.0) * ai) / den
        br = jnp.where(same_group, btr_ref[d], 0.0)
        bi = jnp.where(same_group, bti_ref[d], 0.0)
        bbr = fr * br - fi * bi
        bbi = fr * bi + fi * br
        cr = jnp.where(same_group, ctr_ref[d], 0.0)
        ci = jnp.where(same_group, cti_ref[d], 0.0)
        for k in range(t):
            pkr, pki = pr[k:k + 1], pi[k:k + 1]
            x_r = bbr * pkr - bbi * pki
            x_i = bbr * pki + bbi * pkr
            s = t - 1 - k if d == 0 else k
            bm_ref[s * LANES:(s + 1) * LANES, 2 * d * n:(2 * d + 1) * n] = x_r.astype(BF16)
            bm_ref[s * LANES:(s + 1) * LANES, (2 * d + 1) * n:(2 * d + 2) * n] = x_i.astype(BF16)
            e_ref[d, k] = (lax.dot_general(x_r, cr, nt_dims, precision=lax.Precision.HIGHEST,
                                           preferred_element_type=F32)
                           - lax.dot_general(x_i, ci, nt_dims, precision=lax.Precision.HIGHEST,
                                             preferred_element_type=F32))
        for k in range(1, t + 1):
            pkr, pki = pr[k:k + 1], pi[k:k + 1]
            w_r = cr * pkr - ci * pki
            w_i = cr * pki + ci * pkr
            tt = k - 1 if d == 0 else t - k
            wh_ref[2 * d * n:(2 * d + 1) * n, tt * LANES:(tt + 1) * LANES] = w_r.T.astype(BF16)
            wh_ref[(2 * d + 1) * n:(2 * d + 2) * n, tt * LANES:(tt + 1) * LANES] = (-w_i).T.astype(BF16)
        atr_ref[d] = pr[t:t + 1]
        ati_ref[d] = pi[t:t + 1]


def _ssm_toeplitz_kernel(e_ref, dsk_ref, m_ref):
    t = SSM_CHUNK
    s = pl.program_id(1)
    r = lax.broadcasted_iota(jnp.int32, (LANES, LANES), 0)
    c = lax.broadcasted_iota(jnp.int32, (LANES, LANES), 1)
    skip = jnp.where(r == c, jnp.broadcast_to(dsk_ref[...], (LANES, LANES)), 0.0)
    for tt in range(t):
        blk = ((tt >= s).astype(F32) * e_ref[0, jnp.maximum(tt - s, 0)]
               + (s >= tt).astype(F32) * e_ref[1, jnp.maximum(s - tt, 0)]
               + (s == tt).astype(F32) * skip)
        m_ref[:, tt * LANES:(tt + 1) * LANES] = blk.astype(BF16)


def _ssm_ops(a_re, a_im, log_dt, b_re, b_im, c_re, c_im, d_skip):
    t = SSM_CHUNK
    n = SLAB_STATE

    def per_state(x):
        return x.astype(F32).reshape(2, SLABS, 1, n)

    def per_channel(x):
        x = x.astype(F32).reshape(2, SLABS, LANES, SSM_STATE)
        return jnp.tile(x, (1, 1, 1, SLAB_GROUPS))

    ldt = jnp.broadcast_to(log_dt[..., None], a_re.shape)
    vec = pl.BlockSpec((2, None, 1, n), lambda z: (0, z, 0, 0))
    mat = pl.BlockSpec((2, None, LANES, n), lambda z: (0, z, 0, 0))
    bm, wh, at_re, at_im, e = pl.pallas_call(
        _ssm_ops_kernel,
        out_shape=(jax.ShapeDtypeStruct((SLABS, t * LANES, 4 * n), BF16),
                   jax.ShapeDtypeStruct((SLABS, 4 * n, t * LANES), BF16),
                   jax.ShapeDtypeStruct((SLABS, 2, 1, n), F32),
                   jax.ShapeDtypeStruct((SLABS, 2, 1, n), F32),
                   jax.ShapeDtypeStruct((SLABS, 2, t, LANES, LANES), F32)),
        grid=(SLABS,),
        in_specs=[vec, vec, vec, mat, mat, mat, mat],
        out_specs=(pl.BlockSpec((None, t * LANES, 4 * n), lambda z: (z, 0, 0)),
                   pl.BlockSpec((None, 4 * n, t * LANES), lambda z: (z, 0, 0)),
                   pl.BlockSpec((None, 2, 1, n), lambda z: (z, 0, 0, 0)),
                   pl.BlockSpec((None, 2, 1, n), lambda z: (z, 0, 0, 0)),
                   pl.BlockSpec((None, 2, t, LANES, LANES), lambda z: (z, 0, 0, 0, 0))),
        compiler_params=_params(1), name="ssm_ops",
    )(per_state(a_re), per_state(a_im), per_state(ldt),
      per_channel(b_re.transpose(0, 1, 3, 2)), per_channel(b_im.transpose(0, 1, 3, 2)),
      per_channel(c_re), per_channel(c_im))
    m = pl.pallas_call(
        _ssm_toeplitz_kernel,
        out_shape=jax.ShapeDtypeStruct((SLABS, t * LANES, t * LANES), BF16),
        grid=(SLABS, t),
        in_specs=[pl.BlockSpec((None, 2, t, LANES, LANES), lambda z, s: (z, 0, 0, 0, 0)),
                  pl.BlockSpec((None, 1, LANES), lambda z, s: (z, 0, 0))],
        out_specs=pl.BlockSpec((None, LANES, t * LANES), lambda z, s: (z, s, 0)),
        compiler_params=_params(2), name="ssm_toeplitz",
    )(e, d_skip.astype(F32).reshape(SLABS, 1, LANES))
    return m, bm, wh, at_re.reshape(SLABS * 2, 1, n), at_im.reshape(SLABS * 2, 1, n)


def _ssm_contrib_kernel(u_ref, bm_ref, o_ref):
    o_ref[...] = jnp.dot(u_ref[...], bm_ref[...], preferred_element_type=F32).astype(BF16)


def _ssm_contrib(u, bm):
    slabs, r, k = u.shape
    n = bm.shape[2]
    tr = _tile(r, SSM_ROW_TILE)
    return pl.pallas_call(
        _ssm_contrib_kernel,
        out_shape=jax.ShapeDtypeStruct((slabs, r, n), BF16),
        grid=(slabs, r // tr),
        in_specs=[pl.BlockSpec((None, tr, k), lambda s, i: (s, i, 0)),
                  pl.BlockSpec((None, k, n), lambda s, i: (s, 0, 0))],
        out_specs=pl.BlockSpec((None, tr, n), lambda s, i: (s, i, 0)),
        compiler_params=_params(2), name="ssm_contrib",
    )(u, bm)


SCAN_UNROLL = 8


def _ssm_scan_kernel(ar_ref, ai_ref, cr_ref, ci_ref, hr_ref, hi_ref, *, ncc, nct, rb):
    d = pl.program_id(1)
    lt = ar_ref.shape[-1]
    ncx = nct - ncc
    ar = jnp.broadcast_to(ar_ref[...], (rb, lt))
    ai = jnp.broadcast_to(ai_ref[...], (rb, lt))
    zero = jnp.zeros((rb, lt), F32)

    def advance(i_src, i_dst, hr, hi):
        r0 = pl.multiple_of(i_src * rb, rb)
        r1 = pl.multiple_of(i_dst * rb, rb)
        nr = ar * hr - ai * hi + cr_ref[pl.ds(r0, rb), :].astype(F32)
        ni = ar * hi + ai * hr + ci_ref[pl.ds(r0, rb), :].astype(F32)
        hr_ref[pl.ds(r1, rb), :] = nr.astype(BF16)
        hi_ref[pl.ds(r1, rb), :] = ni.astype(BF16)
        return nr, ni

    def start(i):
        hr_ref[pl.ds(i * rb, rb), :] = zero.astype(BF16)
        hi_ref[pl.ds(i * rb, rb), :] = zero.astype(BF16)
        return zero, zero

    @pl.when(d == 0)
    def _():
        c = start(ncx)
        c = lax.fori_loop(0, ncc - 1, lambda k, c: advance(ncx + k, ncx + k + 1, *c), c, unroll=SCAN_UNROLL)
        c = advance(nct - 1, 0, *c)
        lax.fori_loop(0, ncx - 1, lambda k, c: advance(k, k + 1, *c), c, unroll=SCAN_UNROLL)

    @pl.when(d == 1)
    def _():
        c = start(nct - 1)
        c = lax.fori_loop(0, ncc - 1, lambda k, c: advance(nct - 1 - k, nct - 2 - k, *c), c, unroll=SCAN_UNROLL)
        c = advance(ncx, ncx - 1, *c)
        lax.fori_loop(0, ncx - 1, lambda k, c: advance(ncx - 1 - k, ncx - 2 - k, *c), c, unroll=SCAN_UNROLL)


def _ssm_scan(contrib, at_re, at_im, ncc, nct, rb):
    slabs, r, n = contrib.shape
    lt = 2 * LANES
    q = SLAB_STATE // lt
    re_spec = pl.BlockSpec((None, r, lt), lambda s, d, j: (s, 0, d * 2 * q + j))
    im_spec = pl.BlockSpec((None, r, lt), lambda s, d, j: (s, 0, d * 2 * q + q + j))
    a_spec = pl.BlockSpec((None, 1, lt), lambda s, d, j: (s * 2 + d, 0, j))
    h_spec = pl.BlockSpec((None, r, lt), lambda s, d, j: (s, 0, d * q + j))
    out = jax.ShapeDtypeStruct((slabs, r, n // 2), BF16)
    return pl.pallas_call(
        functools.partial(_ssm_scan_kernel, ncc=ncc, nct=nct, rb=rb),
        out_shape=(out, out),
        grid=(slabs, 2, q),
        in_specs=[a_spec, a_spec, re_spec, im_spec],
        out_specs=(h_spec, h_spec),
        compiler_params=_params(3), name="ssm_scan",
    )(at_re, at_im, contrib, contrib)


def _ssm_read_kernel(u_ref, hr_ref, hi_ref, m_ref, wh_ref, o_ref):
    q = SLAB_STATE
    acc = jnp.dot(u_ref[...], m_ref[...], preferred_element_type=F32)
    for d in range(2):
        acc += jnp.dot(hr_ref[:, d * q:(d + 1) * q], wh_ref[2 * d * q:(2 * d + 1) * q, :],
                       preferred_element_type=F32)
        acc += jnp.dot(hi_ref[:, d * q:(d + 1) * q], wh_ref[(2 * d + 1) * q:(2 * d + 2) * q, :],
                       preferred_element_type=F32)
    o_ref[...] = acc.astype(BF16)


def _ssm_read(u, h_re, h_im, m, wh):
    slabs, r, k = u.shape
    ns = h_re.shape[2]
    nw = wh.shape[1]
    tr = _tile(r, SSM_ROW_TILE)
    tn = k // 2
    return pl.pallas_call(
        _ssm_read_kernel,
        out_shape=jax.ShapeDtypeStruct((slabs, r, k), BF16),
        grid=(slabs, 2, r // tr),
        in_specs=[pl.BlockSpec((None, tr, k), lambda s, j, i: (s, i, 0)),
                  pl.BlockSpec((None, tr, ns), lambda s, j, i: (s, i, 0)),
                  pl.BlockSpec((None, tr, ns), lambda s, j, i: (s, i, 0)),
                  pl.BlockSpec((None, k, tn), lambda s, j, i: (s, 0, j)),
                  pl.BlockSpec((None, nw, tn), lambda s, j, i: (s, 0, j))],
        out_specs=pl.BlockSpec((None, tr, tn), lambda s, j, i: (s, i, j)),
        compiler_params=_params(3), name="ssm_read",
    )(u, h_re, h_im, m, wh)


MLP_FF_CHUNK = 2048


def _mlp_tail(x, g_ref, sh_ref, sc_ref, gate_ref, w1_ref, w2_ref, o_ref):
    fc = MLP_FF_CHUNK
    h = _norm_mod(x, g_ref[...], sh_ref[...], sc_ref[...]).reshape(-1, D_MODEL).astype(BF16)
    acc = jnp.zeros(h.shape, F32)
    for k in range(D_FF // fc):
        a = jnp.dot(h, w1_ref[:, k * fc:(k + 1) * fc], preferred_element_type=F32)
        a = jnp.square(jnp.maximum(a, 0.0)).astype(BF16)
        acc += jnp.dot(a, w2_ref[k * fc:(k + 1) * fc, :], preferred_element_type=F32)
    out = x + gate_ref[...] * acc.reshape(x.shape)
    o_ref[...] = out
    return out


def _even_tail_kernel(ya_ref, y_ref, x_ref, gate1_ref, wg_ref, bg_ref, wa_ref, wb_ref,
                      g_ref, sh_ref, sc_ref, gate2_ref, w1_ref, w2_ref, o_ref, *ys_scr):
    nb, t_len, _ = x_ref.shape
    for s in range(SLABS):
        for t in range(t_len):
            ys_scr[s][pl.ds(t, nb, stride=t_len), :] = y_ref[s, :, t * LANES:(t + 1) * LANES].astype(F32)
    y = _gelu_tanh(jnp.concatenate([r[...] for r in ys_scr], axis=1))
    glu = y * _sigmoid(jnp.dot(y.astype(BF16), wg_ref[...], preferred_element_type=F32) + bg_ref[...])
    acc = jnp.dot(ya_ref[...].reshape(nb * t_len, D_FOURIER), wa_ref[...], preferred_element_type=F32)
    acc += jnp.dot(glu.astype(BF16), wb_ref[...], preferred_element_type=F32)
    x1 = x_ref[...] + gate1_ref[...] * acc.reshape(nb, t_len, D_MODEL)
    _mlp_tail(x1, g_ref, sh_ref, sc_ref, gate2_ref, w1_ref, w2_ref, o_ref)


def _odd_tail_kernel(a_ref, x_ref, gate1_ref, wo_ref, g_ref, sh_ref, sc_ref, gate2_ref, w1_ref, w2_ref, o_ref):
    a = a_ref[...].reshape(x_ref.shape)
    x1 = x_ref[...] + gate1_ref[...] * jnp.dot(a, wo_ref[...], preferred_element_type=F32)
    _mlp_tail(x1, g_ref, sh_ref, sc_ref, gate2_ref, w1_ref, w2_ref, o_ref)


def _mlp_specs(rows):
    return [_const_spec((1, D_MODEL)), rows.mod(3), rows.mod(4), rows.mod(5),
            _const_spec((D_MODEL, D_FF)), _const_spec((D_FF, D_MODEL))]


def _even_tail(ya, y, first_chunk, x, rows, mod, w_glu, b_glu, w_out, norm_g, w1, w2):
    return pl.pallas_call(
        _even_tail_kernel, out_shape=jax.ShapeDtypeStruct(x.shape, F32), grid=rows.grid,
        in_specs=[rows.rows(D_FOURIER), rows.chunk(first_chunk), rows.rows(D_MODEL), rows.mod(2),
                  _const_spec((D_SSM, D_SSM)), _const_spec((1, D_SSM)),
                  _const_spec((D_FOURIER, D_MODEL)), _const_spec((D_SSM, D_MODEL))] + _mlp_specs(rows),
        out_specs=rows.rows(D_MODEL),
        scratch_shapes=[pltpu.VMEM((rows.tm, LANES), F32)] * SLABS,
        compiler_params=_params(1), name="even_tail",
    )(ya, y, x, mod, w_glu, b_glu, w_out[:D_FOURIER], w_out[D_FOURIER:], norm_g, mod, mod, mod, w1, w2)


def _odd_tail(a, x, rows, mod, w_out, norm_g, w1, w2):
    return pl.pallas_call(
        _odd_tail_kernel, out_shape=jax.ShapeDtypeStruct(x.shape, F32), grid=rows.grid,
        in_specs=[rows.rows3(a.shape[1], D_MODEL), rows.rows(D_MODEL), rows.mod(2),
                  _const_spec((D_MODEL, D_MODEL))] + _mlp_specs(rows),
        out_specs=rows.rows(D_MODEL),
        compiler_params=_params(2), name="odd_tail",
    )(a, x, mod, w_out, norm_g, mod, mod, mod, w1, w2)


def _odd_tail_even_in_kernel(u_in_ref, a_ref, x_ref, gate1_ref, wo_ref, g_ref, sh_ref, sc_ref, gate2_ref,
                             w1_ref, w2_ref, gn_ref, shn_ref, scn_ref, win_ref, cs_ref,
                             o_ref, vc_ref, vs_ref, u_ref, *zs_scr):
    del u_in_ref
    x = x_ref[...]
    a = a_ref[...].reshape(-1, D_MODEL)
    x1 = x + gate1_ref[...] * jnp.dot(a, wo_ref[...], preferred_element_type=F32).reshape(x.shape)
    x2 = _mlp_tail(x1, g_ref, sh_ref, sc_ref, gate2_ref, w1_ref, w2_ref, o_ref)
    _even_in_body(x2, gn_ref, shn_ref, scn_ref, win_ref, cs_ref, vc_ref, vs_ref, u_ref, zs_scr)


def _odd_tail_even_in(a, x, rows, mod, w_out, norm_g, w1, w2, next_norm_g, next_mod, next_w_in, cs, u, first_chunk):
    nb, n_tok, _ = x.shape
    four = jax.ShapeDtypeStruct((nb, n_tok, D_FOURIER), BF16)
    return pl.pallas_call(
        _odd_tail_even_in_kernel,
        out_shape=(jax.ShapeDtypeStruct(x.shape, F32), four, four, jax.ShapeDtypeStruct(u.shape, BF16)),
        grid=rows.grid,
        in_specs=[pl.BlockSpec(memory_space=pl.ANY), rows.rows(D_MODEL), rows.rows(D_MODEL), rows.mod(2),
                  _const_spec((D_MODEL, D_MODEL))] + _mlp_specs(rows)
                 + [_const_spec((1, D_MODEL)), rows.mod(0), rows.mod(1), _const_spec((D_MODEL, D_MODEL)),
                    _const_spec((FOURIER_CH, 2 * FOURIER_CH))],
        out_specs=(rows.rows(D_MODEL), rows.rows(D_FOURIER), rows.rows(D_FOURIER), rows.chunk(first_chunk)),
        scratch_shapes=[pltpu.VMEM((rows.tm, LANES), F32)] * SLABS,
        input_output_aliases={0: 3},
        compiler_params=_params(1), name="odd_tail_even_in",
    )(u, a, x, mod, w_out, norm_g, mod, mod, mod, w1, w2, next_norm_g, next_mod, next_mod, next_w_in, cs)


KEY_CHUNK = 256
V_ROWS = DA_V_DIM + 16
Q_SCALE = DA_HEAD_DIM ** -0.5 * math.log2(math.e)


def _qkv_kernel(*refs, rope):
    if rope:
        (x_ref, g_ref, sh_ref, sc_ref, w_ref, gqk_ref, ones_ref,
         cos_ref, sa_ref, sb_ref, qt_ref, k_ref, vt_ref) = refs
    else:
        x_ref, g_ref, sh_ref, sc_ref, w_ref, gqk_ref, ones_ref, qt_ref, k_ref, vt_ref = refs
    h = _norm_mod(x_ref[...], g_ref[...], sh_ref[...], sc_ref[...]).astype(BF16)
    y = jnp.dot(h, w_ref[...], preferred_element_type=F32)
    tm = y.shape[0]
    for s in range(2 * DA_HEADS):
        blk = y[:, s * LANES:(s + 1) * LANES]
        ssq = jnp.dot((blk * blk).astype(BF16), ones_ref[...], preferred_element_type=F32)
        nrm = blk * gqk_ref[s // DA_HEADS]
        if rope:
            nrm = (nrm * cos_ref[...] + pltpu.roll(nrm, LANES - 16, 1) * sa_ref[...]
                   + pltpu.roll(nrm, 16, 1) * sb_ref[...])
        nrm = nrm * lax.rsqrt(ssq * (1.0 / DA_HEAD_DIM) + EPS)
        if s < DA_HEADS:
            qt_ref[s] = nrm.T.astype(BF16)
        else:
            hd = s - DA_HEADS
            k_ref[:, hd * LANES:(hd + 1) * LANES] = nrm.astype(BF16)
    ones = jnp.ones((V_ROWS - DA_V_DIM, tm), BF16)
    for hd in range(DA_HEADS):
        v = y[:, (2 * DA_HEADS + hd) * LANES:(2 * DA_HEADS + hd + 1) * LANES]
        vt_ref[hd, :DA_V_DIM, :] = v.T.astype(BF16)
        vt_ref[hd, DA_V_DIM:, :] = ones


def _qkv(x, rows, norm_g, mod, w_in, gqk, ones_bd, rope_tabs):
    nb, rpb, tm = rows.nb, rows.rpb, rows.tm
    assert tm % KEY_CHUNK == 0
    rope = rope_tabs is not None
    in_specs = [rows.rows(D_MODEL), _const_spec((1, D_MODEL)), rows.mod(0), rows.mod(1),
                _const_spec((D_MODEL, 3 * D_MODEL)), _const_spec((2, 1, LANES)),
                _const_spec((LANES, LANES))]
    args = [x, norm_g, mod, mod, w_in, gqk, ones_bd]
    if rope:
        tab = pl.BlockSpec((tm, LANES), lambda b, i: (i, 0))
        in_specs += [tab, tab, tab]
        args += list(rope_tabs)
    out_shape = (jax.ShapeDtypeStruct((nb, DA_HEADS, LANES, rpb), BF16),
                 jax.ShapeDtypeStruct((nb, rpb, D_MODEL), BF16),
                 jax.ShapeDtypeStruct((nb, DA_HEADS, V_ROWS, rpb), BF16))
    out_specs = (pl.BlockSpec((None, DA_HEADS, LANES, tm), lambda b, i: (b, 0, 0, i)),
                 pl.BlockSpec((None, tm, D_MODEL), lambda b, i: (b, i, 0)),
                 pl.BlockSpec((None, DA_HEADS, V_ROWS, tm), lambda b, i: (b, 0, 0, i)))
    return pl.pallas_call(
        functools.partial(_qkv_kernel, rope=rope),
        out_shape=out_shape, grid=rows.grid, in_specs=in_specs, out_specs=out_specs,
        compiler_params=_params(2), name="qkv",
    )(*args)


def _rope_tables(length):
    pos = jnp.arange(length, dtype=jnp.int32)
    row = (pos // GRID_W).astype(F32)
    col = (pos % GRID_W).astype(F32)
    half = DA_HEAD_DIM // 2
    inv_freq = ROPE_THETA ** (-jnp.arange(0, half, 2, dtype=F32) / half)
    lane = jnp.arange(LANES)
    f = lane % 16
    is_col = (lane % DA_HEAD_DIM) >= half
    second = (lane % half) >= 16
    ang = jnp.where(is_col[None, :], col[:, None], row[:, None]) * inv_freq[f][None, :]
    cos, sin = jnp.cos(ang), jnp.sin(ang)
    sin_a = jnp.where(second[None, :], 0.0, -sin)
    sin_b = jnp.where(second[None, :], sin, 0.0)
    return cos, sin_a, sin_b


def _attn_kernel(*refs, n_kv, lam_init, tq):
    lp_ref, gh_ref, qt_ref = refs[:3]
    kv_refs = refs[3:3 + 2 * n_kv]
    o_ref = refs[3 + 2 * n_kv]
    bufs = refs[4 + 2 * n_kv:]
    lp = lp_ref[...]
    lam = (jnp.exp(jnp.sum(lp[0:1] * lp[1:2], axis=-1, keepdims=True))
           - jnp.exp(jnp.sum(lp[2:3] * lp[3:4], axis=-1, keepdims=True)) + lam_init)
    hb = qt_ref.shape[0]
    items = [(hh, j) for hh in range(hb) for j in range(qt_ref.shape[2] // tq)]
    kc = KEY_CHUNK

    key_chunks = []
    for i in range(n_kv):
        for c in range(kv_refs[2 * i].shape[0] // kc):
            key_chunks.append((kv_refs[2 * i], kv_refs[2 * i + 1], c))

    def masked_q(item):
        hh, j = item
        qt = qt_ref[hh, :, j * tq:(j + 1) * tq]
        row = lax.broadcasted_iota(jnp.int32, qt.shape, 0)
        zero = jnp.zeros_like(qt)
        return jnp.concatenate([jnp.where(row < DA_HEAD_DIM, qt, zero),
                                jnp.where(row >= DA_HEAD_DIM, qt, zero)], axis=1)

    def score_chunk(n, hh, qst, s_scr, m):
        k_ref, _, c = key_chunks[n]
        s = jnp.dot(k_ref[c * kc:(c + 1) * kc, hh * LANES:(hh + 1) * LANES], qst,
                    preferred_element_type=F32)
        s_scr[n * kc:(n + 1) * kc, :] = s
        return jnp.maximum(m, s.reshape(kc // 8, 8, 2 * tq).max(axis=0))

    def value_chunk(n, hh, s_scr, m, acc):
        _, vt_ref, c = key_chunks[n]
        r = pl.multiple_of(n * kc + jnp.minimum(pl.program_id(0), 0) * kc, kc)
        p = jnp.exp2((s_scr[pl.ds(r, kc), :] - m).astype(BF16))
        return acc + jnp.dot(vt_ref[hh, :, c * kc:(c + 1) * kc], p, preferred_element_type=F32)

    def finish(item, acc):
        hh, j = item
        o2 = acc[:DA_V_DIM] / acc[DA_V_DIM:DA_V_DIM + 1]
        o = o2[:, :tq] - lam * o2[:, tq:]
        o = o * lax.rsqrt(jnp.mean(o * o, axis=0, keepdims=True) + EPS) * gh_ref[...]
        o_ref[j * tq:(j + 1) * tq, hh * LANES:(hh + 1) * LANES] = (o * (1.0 - lam_init)).T.astype(BF16)

    neg = jnp.full((8, 2 * tq), -jnp.inf, F32)
    qst = masked_q(items[0])
    m8 = neg
    for n in range(len(key_chunks)):
        m8 = score_chunk(n, items[0][0], qst, bufs[0], m8)
    for i, item in enumerate(items):
        nxt = items[i + 1] if i + 1 < len(items) else None
        m = m8.max(axis=0, keepdims=True)
        acc = jnp.zeros((V_ROWS, 2 * tq), F32)
        if nxt is not None:
            qst = masked_q(nxt)
            m8 = neg
        for n in range(len(key_chunks)):
            if nxt is not None:
                m8 = score_chunk(n, nxt[0], qst, bufs[(i + 1) % 2], m8)
            acc = value_chunk(n, item[0], bufs[i % 2], m, acc)
        finish(item, acc)


def _attention(qt, kv_list, lam_p, g_head, lam_init, tq, hb):
    nb, _, _, lq = qt.shape
    in_specs = [_const_spec((4, DA_HEAD_DIM)), _const_spec((DA_V_DIM, 1)),
                pl.BlockSpec((None, hb, LANES, lq), lambda b, h: (b, h, 0, 0))]
    args = [lam_p, g_head, qt]
    total = 0
    for k, vt in kv_list:
        lk = k.shape[1]
        total += lk
        in_specs.append(pl.BlockSpec((None, lk, hb * LANES), lambda b, h: (b, 0, h)))
        in_specs.append(pl.BlockSpec((None, hb, V_ROWS, lk), lambda b, h: (b, h, 0, 0)))
        args += [k, vt]
    n_bufs = min(2, hb * (lq // tq))
    return pl.pallas_call(
        functools.partial(_attn_kernel, n_kv=len(kv_list), lam_init=lam_init, tq=tq),
        out_shape=jax.ShapeDtypeStruct((nb, lq, D_MODEL), BF16),
        grid=(nb, DA_HEADS // hb),
        in_specs=in_specs,
        out_specs=pl.BlockSpec((None, lq, hb * LANES), lambda b, h: (b, 0, h)),
        scratch_shapes=[pltpu.VMEM((total, 2 * tq), F32)] * n_bufs,
        compiler_params=_params(2), name="diff_attn",
    )(*args)


def kernel(x, c, ctx, c_ctx, norm1_g, norm2_g, ada_w, ada_b, mlp_w1, mlp_w2, ev_w_in, ev_w_out, ssm_a_re, ssm_a_im, ssm_log_dt, ssm_b_re, ssm_b_im, ssm_c_re, ssm_c_im, ssm_d, ssm_w_glu, ssm_b_glu, od_w_in, od_w_out, od_q_norm, od_k_norm, od_lambda, od_head_norm):
    nb, length, d = x.shape
    clen = ctx.shape[1]
    assert d == D_MODEL and length % GRID_W == 0
    assert length % SSM_CHUNK == 0 and clen % SSM_CHUNK == 0

    xs = x.reshape(nb * length, d)
    cs = ctx.reshape(nb * clen, d)
    rows_x = _Rows(nb, length, _tile(length, ROW_TILE), None)
    rows_c = _Rows(1, nb * clen, _tile(nb * clen, ROW_TILE), nb)
    rows_cb = _Rows(nb, clen, clen, nb)
    chunks_x = _ChunkRows(nb, length // SSM_CHUNK, None)
    chunks_c = _ChunkRows(nb, clen // SSM_CHUNK, nb)

    n_cond = nb + 1
    pad = (-n_cond) % 8
    cond = jnp.concatenate([c, c_ctx[None, :], jnp.zeros((pad, d), F32)], axis=0)
    mods = _adaln(cond, ada_w, ada_b)[:, :n_cond].reshape(DEPTH, n_cond, 1, 6 * d)

    kf = lax.iota(jnp.int32, FOURIER_CH)
    ang = ((kf[:, None] * kf[None, :]) % FOURIER_CH).astype(F32) * (2.0 * math.pi / FOURIER_CH)
    cs_ch = jnp.concatenate([jnp.cos(ang), jnp.sin(ang)], axis=1).astype(BF16)
    dft_x = _dft_mats(length)
    dft_c = _dft_mats(clen)
    rope_tabs = _rope_tables(length)
    lane = jnp.arange(LANES)
    ones_bd = (lane[:, None] // DA_HEAD_DIM == lane[None, :] // DA_HEAD_DIM).astype(BF16)

    ncc = clen // SSM_CHUNK
    ncx = length // SSM_CHUNK
    nct = ncx + ncc

    pre = None
    for i in range(DEPTH):
        last = i == DEPTH - 1
        j = i // 2
        mod = mods[i]
        n1 = norm1_g[i].reshape(1, d)
        n2 = norm2_g[i].reshape(1, d)
        if i % 2 == 0:
            w_in = ev_w_in[j].astype(BF16)
            u_shape = (SLABS, nct, nb, SSM_CHUNK * LANES)
            x3 = xs.reshape(nb, length, d)
            c3 = cs.reshape(nb, clen, d)
            if pre is None:
                u = _zero_chunks(u_shape)
                vc_x, vs_x, u = _even_in(x3, chunks_x, n1, mod, w_in, cs_ch, u, 0)
                vc_c, vs_c, u = _even_in(c3, chunks_c, n1, mod, w_in, cs_ch, u, ncx)
            else:
                vc_x, vs_x, vc_c, vs_c, u = pre
                pre = None
            ya_x = _fourier(vc_x, vs_x, dft_x)
            m, bm, wh, at_re, at_im = _ssm_ops(ssm_a_re[j], ssm_a_im[j], ssm_log_dt[j], ssm_b_re[j],
                                                ssm_b_im[j], ssm_c_re[j], ssm_c_im[j], ssm_d[j])
            u = u.reshape(SLABS, nct * nb, SSM_CHUNK * LANES)
            contrib = _ssm_contrib(u, bm)
            h_re, h_im = _ssm_scan(contrib, at_re, at_im, ncc, nct, nb)
            y = _ssm_read(u, h_re, h_im, m, wh).reshape(u_shape)
            w_glu = ssm_w_glu[j].astype(BF16)
            b_glu = ssm_b_glu[j].reshape(1, D_SSM)
            w_out = ev_w_out[j].astype(BF16)
            w1 = mlp_w1[i].astype(BF16)
            w2 = mlp_w2[i].astype(BF16)
            xs = _even_tail(ya_x, y, 0, x3, chunks_x, mod, w_glu, b_glu, w_out, n2, w1, w2).reshape(nb * length, d)
            if not last:
                ya_c = _fourier(vc_c, vs_c, dft_c)
                cs = _even_tail(ya_c, y, ncx, c3, chunks_c, mod, w_glu, b_glu, w_out, n2, w1, w2).reshape(nb * clen, d)
        else:
            lam_init = 0.8 - 0.6 * math.exp(-0.3 * i)
            w_in = od_w_in[j].astype(BF16)
            gqk = jnp.stack([jnp.tile(od_q_norm[j], 2) * Q_SCALE,
                             jnp.tile(od_k_norm[j], 2)]).reshape(2, 1, LANES)
            qt_x, k_x, vt_x = _qkv(xs, rows_x, n1, mod, w_in, gqk, ones_bd, rope_tabs)
            qt_c, k_c, vt_c = _qkv(cs, rows_cb, n1, mod, w_in, gqk, ones_bd, None)
            g_head = od_head_norm[j].reshape(DA_V_DIM, 1)
            w_out = od_w_out[j].astype(BF16)
            o_x = _attention(qt_x, [(k_x, vt_x), (k_c, vt_c)], od_lambda[j], g_head, lam_init,
                             _tile(length, ATTN_Q_TILE, LANES), ATTN_HEADS_PER_STEP)
            w1 = mlp_w1[i].astype(BF16)
            w2 = mlp_w2[i].astype(BF16)
            if last:
                xs = _odd_tail(o_x, xs, rows_x, mod, w_out, n2, w1, w2)
            else:
                o_c = _attention(qt_c, [(k_c, vt_c)], od_lambda[j], g_head, lam_init, clen, DA_HEADS)
                nxt = (norm1_g[i + 1].reshape(1, d), mods[i + 1], ev_w_in[(i + 1) // 2].astype(BF16), cs_ch)
                u = _zero_chunks((SLABS, nct, nb, SSM_CHUNK * LANES))
                xs, vc_x, vs_x, u = _odd_tail_even_in(o_x, xs.reshape(nb, length, d), chunks_x, mod, w_out, n2,
                                                      w1, w2, *nxt, u, 0)
                cs, vc_c, vs_c, u = _odd_tail_even_in(o_c, cs.reshape(nb, clen, d), chunks_c, mod, w_out, n2,
                                                      w1, w2, *nxt, u, ncx)
                xs, cs = xs.reshape(nb * length, d), cs.reshape(nb * clen, d)
                pre = (vc_x, vs_x, vc_c, vs_c, u)
    return xs.reshape(nb, length, d)
```
